```python
import jax
import jax.numpy as jnp
from jax import lax
import numpy as np

D_MODEL = 1024
BATCH = 8
SEQ = 4096
DEPTH = 4

HEAD_DIM = 64
N_HEADS_A = 8
N_HEADS_B = 8
AB_WIDTH = (N_HEADS_A + N_HEADS_B) * HEAD_DIM
ROT_DIM = HEAD_DIM // 4
ROPE_THETA = 500000.0
MOBA_BLOCK = 256
MOBA_TOPK = 3
MOBA_QCHUNK = 16
DILATED_BRANCHES = ((128, 1), (512, 4), (2048, 16))
DILATED_QCHUNK = 32
HGRN_EXPAND = 128
N_HEADS_C = D_MODEL // HGRN_EXPAND
HGRN_DV = D_MODEL // N_HEADS_C
HGRN_CHUNK = 64
N_GROUPS = 4
EXPERTS_PER_GROUP = 8
N_EXPERTS = N_GROUPS * EXPERTS_PER_GROUP
TOPK_IN_GROUP = 2
D_EXPERT = D_MODEL // 2
MOE_BLOCK = 128
N_EVEN = (DEPTH + 1) // 2
N_ODD = DEPTH // 2
DEEPNORM_ALPHA = (2.0 * DEPTH) ** 0.25
DEEPNORM_BETA = (8.0 * DEPTH) ** -0.25
LN_EPS = 1e-5
RMS_EPS = 1e-6

kernel_name = 'hybrid_moba_dilated_hgrn2_hmoe'


def layer_norm(x, g, b):
    xf = x.astype(jnp.float32)
    mu = jnp.mean(xf, axis=-1, keepdims=True)
    var = jnp.mean(jnp.square(xf - mu), axis=-1, keepdims=True)
    return ((xf - mu) * lax.rsqrt(var + LN_EPS) * g + b).astype(x.dtype)


def rope_tables(seq):
    half = ROT_DIM // 2
    inv = ROPE_THETA ** (-jnp.arange(half, dtype=jnp.float32) / half)
    ang = jnp.arange(seq, dtype=jnp.float32)[:, None] * inv[None, :]
    return jnp.cos(ang), jnp.sin(ang)


def apply_partial_rope(x, cos, sin):
    half = ROT_DIM // 2
    c = cos.astype(x.dtype)
    s = sin.astype(x.dtype)
    x1 = x[..., :half]
    x2 = x[..., half:ROT_DIM]
    return jnp.concatenate([x1 * c - x2 * s, x2 * c + x1 * s, x[..., ROT_DIM:]], axis=-1)


def moba_attention(q, k, v):
    B, H, S, hd = q.shape
    nb = -(-S // MOBA_BLOCK)
    pad = nb * MOBA_BLOCK - S
    k_blk = jnp.pad(k, ((0, 0), (0, 0), (0, pad), (0, 0))).reshape(B, H, nb, MOBA_BLOCK, hd)
    v_blk = jnp.pad(v, ((0, 0), (0, 0), (0, pad), (0, 0))).reshape(B, H, nb, MOBA_BLOCK, hd)
    k_mean = jnp.mean(k_blk.astype(jnp.float32), axis=3)
    topk = min(MOBA_TOPK, nb - 1)
    scale = hd ** -0.5
    b_idx = jnp.arange(B)[:, None, None, None]
    h_idx = jnp.arange(H)[None, :, None, None]
    blk_ids = jnp.arange(nb)
    in_blk = jnp.arange(MOBA_BLOCK)

    def one_chunk(c):
        start = c * MOBA_QCHUNK
        qc = lax.dynamic_slice_in_dim(q, start, MOBA_QCHUNK, axis=2)
        t = start + jnp.arange(MOBA_QCHUNK)
        own = start // MOBA_BLOCK
        k_own = lax.dynamic_index_in_dim(k_blk, own, axis=2, keepdims=False)
        v_own = lax.dynamic_index_in_dim(v_blk, own, axis=2, keepdims=False)
        s_own = jnp.einsum('bhqd,bhkd->bhqk', qc, k_own).astype(jnp.float32) * scale
        s_own = jnp.where((own * MOBA_BLOCK + in_blk)[None, :] <= t[:, None], s_own, -jnp.inf)
        if topk == 0:
            p = jax.nn.softmax(s_own, axis=-1).astype(v.dtype)
            return jnp.einsum('bhqk,bhkd->bhqd', p, v_own)
        gate = jnp.einsum('bhqd,bhnd->bhqn', qc.astype(jnp.float32), k_mean)
        gate = jnp.where(blk_ids < own, gate, -jnp.inf)
        _, sel = lax.top_k(gate, topk)
        sel_ok = jnp.arange(topk) < own
        k_sel = k_blk[b_idx, h_idx, sel]
        v_sel = v_blk[b_idx, h_idx, sel]
        s_sel = jnp.einsum('bhqd,bhqnkd->bhqnk', qc, k_sel).astype(jnp.float32) * scale
        s_sel = jnp.where(sel_ok[:, None], s_sel, -jnp.inf)
        s = jnp.concatenate([s_sel.reshape(B, H, MOBA_QCHUNK, topk * MOBA_BLOCK), s_own], axis=-1)
        p = jax.nn.softmax(s, axis=-1).astype(v.dtype)
        p_sel = p[..., :topk * MOBA_BLOCK].reshape(B, H, MOBA_QCHUNK, topk, MOBA_BLOCK)
        p_own = p[..., topk * MOBA_BLOCK:]
        return (jnp.einsum('bhqnk,bhqnkd->bhqd', p_sel, v_sel)
                + jnp.einsum('bhqk,bhkd->bhqd', p_own, v_own))

    out = lax.map(one_chunk, jnp.arange(S // MOBA_QCHUNK))
    return jnp.moveaxis(out, 0, 2).reshape(B, H, S, hd)


def dilated_attention(q, k, v):
    B, H, S, hd = q.shape
    scale = hd ** -0.5

    def one_chunk(c):
        start = c * DILATED_QCHUNK
        qc = lax.dynamic_slice_in_dim(q, start, DILATED_QCHUNK, axis=2)
        t = start + jnp.arange(DILATED_QCHUNK)
        outs = []
        lses = []
        for window, dil in DILATED_BRANCHES:
            offs = dil * jnp.arange(window // dil + 1)
            idx = t[:, None] - offs[None, :]
            valid = idx >= 0
            idx = jnp.maximum(idx, 0)
            k_g = k[:, :, idx]
            v_g = v[:, :, idx]
            s = jnp.einsum('bhqd,bhqnd->bhqn', qc, k_g).astype(jnp.float32) * scale
            s = jnp.where(valid, s, -jnp.inf)
            m = jnp.max(s, axis=-1, keepdims=True)
            p = jnp.exp(s - m)
            den = jnp.sum(p, axis=-1, keepdims=True)
            outs.append(jnp.einsum('bhqn,bhqnd->bhqd', (p / den).astype(v.dtype), v_g))
            lses.append(m + jnp.log(den))
        w = jax.nn.softmax(jnp.concatenate(lses, axis=-1), axis=-1).astype(v.dtype)
        return jnp.einsum('bhqr,rbhqd->bhqd', w, jnp.stack(outs))

    out = lax.map(one_chunk, jnp.arange(S // DILATED_QCHUNK))
    return jnp.moveaxis(out, 0, 2).reshape(B, H, S, hd)


def attn_ab_mixer(x, w_in, w_out, cos, sin):
    B, S, _ = x.shape
    wa = N_HEADS_A * HEAD_DIM
    wb = N_HEADS_B * HEAD_DIM
    h = x @ w_in
    cuts = [wa, 2 * wa, 3 * wa, 3 * wa + wb, 3 * wa + 2 * wb]
    q_a, k_a, v_a, q_b, k_b, v_b = jnp.split(h, cuts, axis=-1)

    def heads(z, n):
        return z.reshape(B, S, n, HEAD_DIM).transpose(0, 2, 1, 3)

    o_a = moba_attention(apply_partial_rope(heads(q_a, N_HEADS_A), cos, sin),
                         apply_partial_rope(heads(k_a, N_HEADS_A), cos, sin),
                         heads(v_a, N_HEADS_A))
    o_b = dilated_attention(apply_partial_rope(heads(q_b, N_HEADS_B), cos, sin),
                            apply_partial_rope(heads(k_b, N_HEADS_B), cos, sin),
                            heads(v_b, N_HEADS_B))
    o = jnp.concatenate([o_a, o_b], axis=1).transpose(0, 2, 1, 3).reshape(B, S, AB_WIDTH)
    return o @ w_out


def hgrn2_mixer(x, w_in, w_out, norm_g, lb):
    B, S, D = x.shape
    H, dk, dv, C = N_HEADS_C, HGRN_EXPAND, HGRN_DV, HGRN_CHUNK
    h = x @ w_in
    q, fz, i, g = jnp.split(h, 4, axis=-1)

    def heads(z, d):
        return z.reshape(B, S, H, d).transpose(0, 2, 1, 3).astype(jnp.float32)

    q = heads(q, dk)
    fz = heads(fz, dk)
    v = heads(i, dv)
    lb_h = lb.astype(jnp.float32).reshape(H, dk)[None, :, None, :]
    log_f = jnp.logaddexp(jnp.log(lb_h), jnp.log1p(-lb_h) + jax.nn.log_sigmoid(fz))
    kk = (1.0 - lb_h) * jax.nn.sigmoid(-fz)
    nc = S // C

    def to_chunks(z):
        return z.reshape(B, H, nc, C, z.shape[-1]).transpose(2, 0, 1, 3, 4)

    causal = jnp.tril(jnp.ones((C, C), dtype=bool))

    def step(state, inp):
        qc, kc, vc, lfc = inp
        b = jnp.cumsum(lfc, axis=2)
        inter = jnp.einsum('bhcd,bhde->bhce', qc * jnp.exp(b), state)
        diff = b[:, :, :, None, :] - b[:, :, None, :, :]
        decay = jnp.exp(jnp.where(causal[:, :, None], diff, -jnp.inf))
        attn = jnp.einsum('bhid,bhijd,bhjd->bhij', qc, decay, kc)
        intra = jnp.einsum('bhij,bhje->bhie', attn, vc)
        b_last = b[:, :, -1:, :]
        state = (jnp.exp(b_last[:, :, 0, :])[..., None] * state
                 + jnp.einsum('bhjd,bhje->bhde', kc * jnp.exp(b_last - b), vc))
        return state, inter + intra

    s0 = jnp.zeros((B, H, dk, dv), jnp.float32)
    _, o = lax.scan(step, s0, (to_chunks(q), to_chunks(kk), to_chunks(v), to_chunks(log_f)))
    o = o.transpose(1, 2, 0, 3, 4).reshape(B, H, S, dv)
    o = o * lax.rsqrt(jnp.mean(jnp.square(o), axis=-1, keepdims=True) + RMS_EPS) * norm_g
    o = o.transpose(0, 2, 1, 3).reshape(B, S, D) * jax.nn.silu(g.astype(jnp.float32))
    return o.astype(x.dtype) @ w_out


def hier_moe(x, wg, bg, we, be, w1, w3, w2):
    B, S, D = x.shape
    xt = x.reshape(-1, D)
    T = xt.shape[0]
    lg = (xt @ wg).astype(jnp.float32) + bg
    grp = jnp.argmax(lg, axis=-1)
    pg = jnp.take_along_axis(jax.nn.softmax(lg, axis=-1), grp[:, None], axis=-1)
    le = ((xt @ we).astype(jnp.float32) + be).reshape(T, N_GROUPS, EXPERTS_PER_GROUP)
    le_g = jnp.take_along_axis(le, grp[:, None, None], axis=1)[:, 0]
    top_l, top_i = lax.top_k(le_g, TOPK_IN_GROUP)
    gate = pg * jax.nn.softmax(top_l, axis=-1)
    eid = grp[:, None] * EXPERTS_PER_GROUP + top_i
    A = T * TOPK_IN_GROUP
    eid_f = eid.reshape(-1)
    tok_f = jnp.repeat(jnp.arange(T), TOPK_IN_GROUP)
    wt_f = gate.reshape(-1)
    order = jnp.argsort(eid_f)
    eid_s = eid_f[order]
    tok_s = tok_f[order]
    wt_s = wt_f[order]
    counts = jax.ops.segment_sum(jnp.ones((A,), jnp.int32), eid_f, num_segments=N_EXPERTS)
    starts = jnp.cumsum(counts) - counts
    padded = (counts + MOE_BLOCK - 1) // MOE_BLOCK * MOE_BLOCK
    pends = jnp.cumsum(padded)
    pstarts = pends - padded
    dest = pstarts[eid_s] + jnp.arange(A) - starts[eid_s]
    n_blk = (A + N_EXPERTS * (MOE_BLOCK - 1) + MOE_BLOCK - 1) // MOE_BLOCK
    P = n_blk * MOE_BLOCK
    buf_tok = jnp.zeros((P,), jnp.int32).at[dest].set(tok_s.astype(jnp.int32))
    buf_wt = jnp.zeros((P,), jnp.float32).at[dest].set(wt_s)
    blk_exp = jnp.minimum(jnp.searchsorted(pends, jnp.arange(n_blk) * MOE_BLOCK, side='right'),
                          N_EXPERTS - 1)

    def expert_block(args):
        e, toks, wts = args
        xb = xt[toks]
        hb = jax.nn.silu(xb @ w1[e]) * (xb @ w3[e])
        return (hb @ w2[e]) * wts[:, None].astype(xb.dtype)

    yb = lax.map(expert_block, (blk_exp, buf_tok.reshape(n_blk, MOE_BLOCK),
                                buf_wt.reshape(n_blk, MOE_BLOCK)))
    y = jax.ops.segment_sum(yb.reshape(P, D), buf_tok, num_segments=T)
    return y.reshape(B, S, D)


def setup_inputs(seed: int = 0) -> dict:
    key = jax.random.key(seed)
    ks = jax.random.split(key, 17)
    f32 = jnp.float32
    D, F = D_MODEL, D_EXPERT
    nrm = lambda k, shape: jax.random.normal(k, shape, f32)
    return {
        'x': nrm(ks[0], (BATCH, SEQ, D)),
        'ab_w_in': nrm(ks[1], (N_EVEN, D, 3 * AB_WIDTH)) * D ** -0.5,
        'ab_w_out': nrm(ks[2], (N_EVEN, AB_WIDTH, D)) * AB_WIDTH ** -0.5 * DEEPNORM_BETA,
        'c_w_in': nrm(ks[3], (N_ODD, D, 4 * D)) * D ** -0.5,
        'c_w_out': nrm(ks[4], (N_ODD, D, D)) * D ** -0.5 * DEEPNORM_BETA,
        'c_norm_g': 1.0 + 0.02 * nrm(ks[5], (N_ODD, HGRN_DV)),
        'hgrn_lb_logits': 0.1 * nrm(ks[6], (N_ODD, D)),
        'ln_g': 1.0 + 0.02 * nrm(ks[7], (DEPTH, 2, D)),
        'ln_b': 0.02 * nrm(ks[8], (DEPTH, 2, D)),
        'router_g_w': nrm(ks[9], (DEPTH, D, N_GROUPS)) * D ** -0.5,
        'router_g_b': 0.01 * nrm(ks[10], (DEPTH, N_GROUPS)),
        'router_e_w': nrm(ks[11], (DEPTH, D, N_EXPERTS)) * D ** -0.5,
        'router_e_b': 0.01 * nrm(ks[12], (DEPTH, N_EXPERTS)),
        'exp_w1': nrm(ks[13], (DEPTH, N_EXPERTS, D, F)) * D ** -0.5,
        'exp_w3': nrm(ks[14], (DEPTH, N_EXPERTS, D, F)) * D ** -0.5,
        'exp_w2': nrm(ks[15], (DEPTH, N_EXPERTS, F, D)) * F ** -0.5 * DEEPNORM_BETA,
    }


def reference(x, ab_w_in, ab_w_out, c_w_in, c_w_out, c_norm_g, hgrn_lb_logits, ln_g, ln_b,
              router_g_w, router_g_b, router_e_w, router_e_b, exp_w1, exp_w3, exp_w2):
    S = x.shape[1]
    cos, sin = rope_tables(S)
    lb_all = jnp.cumsum(jax.nn.softmax(hgrn_lb_logits.astype(jnp.float32), axis=0), axis=0)
    lb_all = lb_all - lb_all[0:1]
    for l in range(DEPTH):
        j = l // 2
        if l % 2 == 0:
            mix = attn_ab_mixer(x, ab_w_in[j], ab_w_out[j], cos, sin)
        else:
            mix = hgrn2_mixer(x, c_w_in[j], c_w_out[j], c_norm_g[j], lb_all[j])
        x = layer_norm(DEEPNORM_ALPHA * x + mix, ln_g[l, 0], ln_b[l, 0])
        ffn = hier_moe(x, router_g_w[l], router_g_b[l], router_e_w[l], router_e_b[l],
                       exp_w1[l], exp_w3[l], exp_w2[l])
        x = layer_norm(DEEPNORM_ALPHA * x + ffn, ln_g[l, 1], ln_b[l, 1])
    return x
```

```python
import functools
import math

import numpy as np
import jax
import jax.numpy as jnp
from jax import lax
from jax.experimental import pallas as pl
from jax.experimental.pallas import tpu as pltpu

F32 = jnp.float32
BF16 = jnp.bfloat16

D_MODEL = 1024
DEPTH = 4
HEAD_DIM = 64
N_HEADS_A = 8
N_HEADS_B = 8
AB_WIDTH = (N_HEADS_A + N_HEADS_B) * HEAD_DIM
ROT_DIM = HEAD_DIM // 4
ROPE_THETA = 500000.0
MOBA_BLOCK = 256
MOBA_TOPK = 3
DILATED_BRANCHES = ((128, 1), (512, 4), (2048, 16))
HGRN_EXPAND = 128
N_HEADS_C = D_MODEL // HGRN_EXPAND
HGRN_CHUNK = 64
HGRN_SUB = 16
N_GROUPS = 4
EXPERTS_PER_GROUP = 8
N_EXPERTS = N_GROUPS * EXPERTS_PER_GROUP
TOPK_IN_GROUP = 2
D_EXPERT = D_MODEL // 2
DEEPNORM_ALPHA = (2.0 * DEPTH) ** 0.25
LN_EPS = 1e-5
RMS_EPS = 1e-6

LANES = 128
ATT_BLOCK = 256
NEG_BIG = -1e30
MOE_TILE = 256
VMEM_LIMIT = 56 * 1024 * 1024

_NT = (((1,), (1,)), ((), ()))
_TN = (((0,), (0,)), ((), ()))


def _params(*sem):
    return pltpu.CompilerParams(dimension_semantics=sem, vmem_limit_bytes=VMEM_LIMIT)


def _proj_ab_kernel(x_ref, w_ref, cos_ref, sa_ref, sb_ref, o_ref):
    xb = x_ref[...].astype(BF16)
    sec = AB_WIDTH // 2
    for c in range(6):
        acc = jnp.dot(xb, w_ref[:, c * sec:(c + 1) * sec], preferred_element_type=F32)
        if c % 3 == 2:
            o_ref[:, c * sec:(c + 1) * sec] = acc.astype(BF16)
            continue
        cosv = cos_ref[...]
        sa = sa_ref[...]
        sb = sb_ref[...]
        for k in range(sec // LANES):
            a = acc[:, k * LANES:(k + 1) * LANES]
            r = a * cosv + pltpu.roll(a, LANES - ROT_DIM // 2, 1) * sa + pltpu.roll(a, ROT_DIM // 2, 1) * sb
            if c % 3 == 0:
                r = r * (HEAD_DIM ** -0.5)
            o_ref[:, c * sec + k * LANES:c * sec + (k + 1) * LANES] = r.astype(BF16)


def _rope_lane_tables(seq):
    half = ROT_DIM // 2
    inv = ROPE_THETA ** (-jnp.arange(half, dtype=F32) / half)
    ang = jnp.arange(seq, dtype=F32)[:, None] * inv[None, :]
    cos, sin = jnp.cos(ang), jnp.sin(ang)
    ones = jnp.ones((seq, HEAD_DIM - ROT_DIM), F32)
    zeros = jnp.zeros((seq, HEAD_DIM - ROT_DIM), F32)
    zh = jnp.zeros((seq, half), F32)
    cos_t = jnp.concatenate([cos, cos, ones], axis=1)
    sa_t = jnp.concatenate([-sin, zh, zeros], axis=1)
    sb_t = jnp.concatenate([zh, sin, zeros], axis=1)
    rep = LANES // HEAD_DIM
    return tuple(jnp.tile(t, (1, rep)) for t in (cos_t, sa_t, sb_t))


def _proj_ab(x2d, w_bf16, tables, seq, tm=512):
    t, d = x2d.shape
    n = w_bf16.shape[1]
    nseq = seq // tm
    tab_spec = pl.BlockSpec((tm, LANES), lambda i: (i % nseq, 0))
    return pl.pallas_call(
        _proj_ab_kernel,
        grid=(t // tm,),
        in_specs=[pl.BlockSpec((tm, d), lambda i: (i, 0)),
                  pl.BlockSpec((d, n), lambda i: (0, 0)),
                  tab_spec, tab_spec, tab_spec],
        out_specs=pl.BlockSpec((tm, n), lambda i: (i, 0)),
        out_shape=jax.ShapeDtypeStruct((t, n), BF16),
        compiler_params=_params("parallel"),
        name="proj_ab",
    )(x2d, w_bf16, *tables)


def _softmax_step(s, v_blk, h, m_scr, l_scr, acc_scr):
    m_prev = m_scr[h]
    m_new = jnp.maximum(m_prev, jnp.max(s, axis=1, keepdims=True))
    alpha = jnp.exp(m_prev - m_new)
    p = jnp.exp(s - m_new[:, :1])
    l_scr[h] = alpha * l_scr[h] + jnp.sum(p, axis=1, keepdims=True)
    acc_scr[h] = alpha * acc_scr[h] + jnp.dot(p.astype(BF16), v_blk, preferred_element_type=F32)
    m_scr[h] = m_new


def _softmax_init(m_scr, l_scr, acc_scr):
    m_scr[...] = jnp.full(m_scr.shape, -jnp.inf, F32)
    l_scr[...] = jnp.zeros(l_scr.shape, F32)
    acc_scr[...] = jnp.zeros(acc_scr.shape, F32)


def _softmax_finish(o_ref, l_scr, acc_scr):
    lane = lax.broadcasted_iota(jnp.int32, acc_scr.shape[1:], 1)
    o0 = acc_scr[0] / l_scr[0]
    o1 = acc_scr[1] / l_scr[1]
    o_ref[0] = jnp.where(lane < HEAD_DIM, o0, o1).astype(o_ref.dtype)


def _moba_kernel(q_ref, k_ref, v_ref, o_ref, kmean_scr, qaug_scr, m_scr, l_scr, acc_scr, *, seq):
    bq = ATT_BLOCK
    i = pl.program_id(2)

    @pl.when(i == 0)
    def _block_means():
        rows = lax.broadcasted_iota(jnp.int32, (LANES, seq), 0)
        cols = lax.broadcasted_iota(jnp.int32, (LANES, seq), 1)
        lo = rows * MOBA_BLOCK
        ind = jnp.where((cols >= lo) & (cols < lo + MOBA_BLOCK), 1.0, 0.0).astype(BF16)
        kmean_scr[...] = jnp.dot(ind, k_ref[0], preferred_element_type=F32) * (1.0 / MOBA_BLOCK)

    q = q_ref[0]
    lane = lax.broadcasted_iota(jnp.int32, (bq, LANES), 1)
    lane_f = lane.astype(F32)
    km = kmean_scr[...]
    km_hi = km.astype(BF16)
    km_lo = (km - km_hi.astype(F32)).astype(BF16)
    for h in range(2):
        qh = jnp.where((lane >= h * HEAD_DIM) & (lane < (h + 1) * HEAD_DIM), q, jnp.zeros_like(q))
        g = (lax.dot_general(qh, km_hi, _NT, preferred_element_type=F32)
             + lax.dot_general(qh, km_lo, _NT, preferred_element_type=F32))
        g = jnp.where(lane < i, g, -jnp.inf)
        bias = jnp.where(lane == i, 0.0, NEG_BIG)
        for _ in range(MOBA_TOPK):
            mx = jnp.max(g, axis=1, keepdims=True)
            first = jnp.min(jnp.where(g == mx, lane_f, float(LANES)), axis=1, keepdims=True)
            pick = (lane_f == first) & (mx > -jnp.inf)
            bias = jnp.where(pick, 0.0, bias)
            g = jnp.where(pick, -jnp.inf, g)
        qaug_scr[h] = jnp.concatenate([qh, bias.astype(BF16)], axis=1)

    _softmax_init(m_scr, l_scr, acc_scr)

    def block(j, causal):
        start = pl.multiple_of(j * bq, bq)
        kj = k_ref[0, pl.ds(start, bq), :]
        vj = v_ref[0, pl.ds(start, bq), :]
        onehot = jnp.where(lane == j, 1.0, 0.0).astype(BF16)
        kaug = jnp.concatenate([kj, onehot], axis=1)
        for h in range(2):
            s = lax.dot_general(qaug_scr[h], kaug, _NT, preferred_element_type=F32)
            if causal:
                qi = lax.broadcasted_iota(jnp.int32, s.shape, 0)
                ki = lax.broadcasted_iota(jnp.int32, s.shape, 1)
                s = jnp.where(ki <= qi, s, NEG_BIG)
            _softmax_step(s, vj, h, m_scr, l_scr, acc_scr)

    block(i, True)

    def past(t, carry):
        block(i - 1 - t, False)
        return carry

    lax.fori_loop(0, i, past, 0)
    _softmax_finish(o_ref, l_scr, acc_scr)


def _moba(h3, n_pairs):
    b, seq, _ = h3.shape
    sec_blocks = (AB_WIDTH // 2) // LANES
    nq = seq // ATT_BLOCK
    return pl.pallas_call(
        functools.partial(_moba_kernel, seq=seq),
        grid=(b, n_pairs, nq),
        in_specs=[pl.BlockSpec((1, ATT_BLOCK, LANES), lambda bb, p, i: (bb, i, p)),
                  pl.BlockSpec((1, seq, LANES), lambda bb, p, i: (bb, 0, sec_blocks + p)),
                  pl.BlockSpec((1, seq, LANES), lambda bb, p, i: (bb, 0, 2 * sec_blocks + p))],
        out_specs=pl.BlockSpec((1, ATT_BLOCK, LANES), lambda bb, p, i: (bb, i, p)),
        out_shape=jax.ShapeDtypeStruct((b, seq, n_pairs * LANES), BF16),
        scratch_shapes=[pltpu.VMEM((LANES, LANES), F32),
                        pltpu.VMEM((2, ATT_BLOCK, 2 * LANES), BF16),
                        pltpu.VMEM((2, ATT_BLOCK, LANES), F32),
                        pltpu.VMEM((2, ATT_BLOCK, LANES), F32),
                        pltpu.VMEM((2, ATT_BLOCK, LANES), F32)],
        compiler_params=_params("parallel", "parallel", "arbitrary"),
        name="moba",
    )(h3, h3, h3)


def _dilated_bias_table():
    max_window = max(w for w, _ in DILATED_BRANCHES)
    n_diff = max_window // ATT_BLOCK + 1
    qi = np.arange(ATT_BLOCK)[:, None]
    ki = np.arange(ATT_BLOCK)[None, :]
    tabs = []
    for t in range(n_diff):
        d = qi - ki + t * ATT_BLOCK
        cnt = np.zeros(d.shape, np.int64)
        for window, dil in DILATED_BRANCHES:
            cnt += ((d >= 0) & (d <= window) & (d % dil == 0)).astype(np.int64)
        with np.errstate(divide="ignore"):
            tabs.append(np.where(cnt > 0, np.log(np.maximum(cnt, 1).astype(np.float64)), NEG_BIG))
    return np.stack(tabs).astype(np.float32)


def _dilated_kernel(q_ref, k_ref, v_ref, bias_ref, o_ref, m_scr, l_scr, acc_scr, *, n_diff):
    bq = ATT_BLOCK
    i = pl.program_id(2)
    q = q_ref[0]
    lane = lax.broadcasted_iota(jnp.int32, (bq, LANES), 1)
    qh = [jnp.where((lane >= h * HEAD_DIM) & (lane < (h + 1) * HEAD_DIM), q, jnp.zeros_like(q))
          for h in range(2)]
    _softmax_init(m_scr, l_scr, acc_scr)

    def step(t, carry):
        start = pl.multiple_of((i - t) * bq, bq)
        kj = k_ref[0, pl.ds(start, bq), :]
        vj = v_ref[0, pl.ds(start, bq), :]
        bias = bias_ref[t]
        for h in range(2):
            s = lax.dot_general(qh[h], kj, _NT, preferred_element_type=F32) + bias
            _softmax_step(s, vj, h, m_scr, l_scr, acc_scr)
        return carry

    lax.fori_loop(0, jnp.minimum(i, n_diff - 1) + 1, step, 0)
    _softmax_finish(o_ref, l_scr, acc_scr)


def _dilated(h3, bias_tab, n_pairs):
    b, seq, _ = h3.shape
    sec_blocks = (AB_WIDTH // 2) // LANES
    base = 3 * sec_blocks
    nq = seq // ATT_BLOCK
    n_diff = bias_tab.shape[0]
    return pl.pallas_call(
        functools.partial(_dilated_kernel, n_diff=n_diff),
        grid=(b, n_pairs, nq),
        in_specs=[pl.BlockSpec((1, ATT_BLOCK, LANES), lambda bb, p, i: (bb, i, base + p)),
                  pl.BlockSpec((1, seq, LANES), lambda bb, p, i: (bb, 0, base + sec_blocks + p)),
                  pl.BlockSpec((1, seq, LANES), lambda bb, p, i: (bb, 0, base + 2 * sec_blocks + p)),
                  pl.BlockSpec((n_diff, ATT_BLOCK, ATT_BLOCK), lambda bb, p, i: (0, 0, 0))],
        out_specs=pl.BlockSpec((1, ATT_BLOCK, LANES), lambda bb, p, i: (bb, i, p)),
        out_shape=jax.ShapeDtypeStruct((b, seq, n_pairs * LANES), BF16),
        scratch_shapes=[pltpu.VMEM((2, ATT_BLOCK, LANES), F32),
                        pltpu.VMEM((2, ATT_BLOCK, LANES), F32),
                        pltpu.VMEM((2, ATT_BLOCK, LANES), F32)],
        compiler_params=_params("parallel", "parallel", "arbitrary"),
        name="dilated",
    )(h3, h3, h3, bias_tab)


def _proj_c_kernel(x_ref, w_ref, loglb_ref, log1mlb_ref, omlb_ref, q_ref, lf_ref, kk_ref, v_ref, g_ref):
    xb = x_ref[...].astype(BF16)
    d = D_MODEL

    def sec(c):
        return jnp.dot(xb, w_ref[:, c * d:(c + 1) * d], preferred_element_type=F32)

    q_ref[...] = sec(0)
    z = sec(1)
    log_sig = jnp.minimum(z, 0.0) - jnp.log1p(jnp.exp(-jnp.abs(z)))
    a = loglb_ref[...]
    c = log1mlb_ref[...] + log_sig
    lf_ref[...] = jnp.maximum(a, c) + jnp.log1p(jnp.exp(-jnp.abs(a - c)))
    kk_ref[...] = omlb_ref[...] / (1.0 + jnp.exp(z))
    v_ref[...] = sec(2)
    g_ref[...] = sec(3)


def _proj_c(x2d, w_bf16, lb, tm=256):
    t, d = x2d.shape
    n = w_bf16.shape[1]
    lb = lb.astype(F32).reshape(1, d)
    vec_spec = pl.BlockSpec((1, d), lambda i: (0, 0))
    out_spec = pl.BlockSpec((tm, d), lambda i: (i, 0))
    sds = jax.ShapeDtypeStruct((t, d), F32)
    return pl.pallas_call(
        _proj_c_kernel,
        grid=(t // tm,),
        in_specs=[pl.BlockSpec((tm, d), lambda i: (i, 0)),
                  pl.BlockSpec((d, n), lambda i: (0, 0)),
                  vec_spec, vec_spec, vec_spec],
        out_specs=[out_spec] * 5,
        out_shape=[sds] * 5,
        compiler_params=_params("parallel"),
        name="proj_c",
    )(x2d, w_bf16, jnp.log(lb), jnp.log1p(-lb), 1.0 - lb)


def _split3_bf16(x):
    h1 = x.astype(BF16)
    r1 = x - h1.astype(F32)
    h2 = r1.astype(BF16)
    h3 = (r1 - h2.astype(F32)).astype(BF16)
    return h1, h2, h3


def _hgrn_kernel(q_ref, lf_ref, kk_ref, v_ref, g_ref, ng_ref, o_ref, st_scr, *, seq):
    C, SUB = HGRN_CHUNK, HGRN_SUB
    st_scr[...] = jnp.zeros(st_scr.shape, F32)
    ri = lax.broadcasted_iota(jnp.int32, (C, C), 0)
    ci = lax.broadcasted_iota(jnp.int32, (C, C), 1)
    tri = jnp.where(ci <= ri, 1.0, 0.0).astype(BF16)
    row_c = lax.broadcasted_iota(jnp.int32, (C, LANES), 0)
    row_s = lax.broadcasted_iota(jnp.int32, (SUB, C), 0)
    lane_s = lax.broadcasted_iota(jnp.int32, (SUB, C), 1)
    ng = ng_ref[...]

    def chunk(c, carry):
        r0 = pl.multiple_of(c * C, C)
        qc = q_ref[0, pl.ds(r0, C), :]
        kc = kk_ref[0, pl.ds(r0, C), :]
        vc = v_ref[0, pl.ds(r0, C), :]
        l1, l2, l3 = _split3_bf16(lf_ref[0, pl.ds(r0, C), :])
        b = (jnp.dot(tri, l1, preferred_element_type=F32)
             + jnp.dot(tri, l2, preferred_element_type=F32)
             + jnp.dot(tri, l3, preferred_element_type=F32))
        b_last = b[C - 1:C, :]
        st = st_scr[...]
        inter = lax.dot_general((qc * jnp.exp(b)).astype(BF16), st.astype(BF16), _NT,
                                preferred_element_type=F32)
        vb = vc.astype(BF16)
        rows_a = []
        for sidx in range(C // SUB):
            lo = sidx * SUB
            q_i = qc[lo:lo + SUB, :]
            b_i = b[lo:lo + SUB, :]
            if sidx == 0:
                a_blk = jnp.zeros((SUB, C), F32)
            else:
                ref = b[lo - 1:lo, :]
                qt = (q_i * jnp.exp(b_i - ref)).astype(BF16)
                kt = jnp.where(row_c < lo, kc * jnp.exp(jnp.minimum(ref - b, 0.0)), 0.0).astype(BF16)
                a_blk = lax.dot_general(qt, kt, _NT, preferred_element_type=F32)
            for j in range(SUB):
                r_lo = 0 if j < 8 else 8
                bj = b[lo + j:lo + j + 1, :]
                kj = kc[lo + j:lo + j + 1, :]
                pj = q_i[r_lo:, :] * (jnp.exp(jnp.minimum(b_i[r_lo:, :] - bj, 0.0)) * kj)
                col = jnp.sum(pj, axis=1, keepdims=True)
                if r_lo:
                    col = jnp.concatenate([jnp.zeros((r_lo, 1), F32), col], axis=0)
                a_blk = jnp.where((lane_s == lo + j) & (row_s >= j), col, a_blk)
            rows_a.append(a_blk)
        a_full = jnp.concatenate(rows_a, axis=0)
        o = inter + jnp.dot(a_full.astype(BF16), vb, preferred_element_type=F32)
        kd = (kc * jnp.exp(b_last - b)).astype(BF16)
        st_scr[...] = st * jnp.exp(b_last) + lax.dot_general(vb, kd, _TN, preferred_element_type=F32)
        o = o * lax.rsqrt(jnp.mean(o * o, axis=1, keepdims=True) + RMS_EPS) * ng
        gc = g_ref[0, pl.ds(r0, C), :]
        o_ref[0, pl.ds(r0, C), :] = (o * (gc / (1.0 + jnp.exp(-gc)))).astype(o_ref.dtype)
        return carry

    lax.fori_loop(0, seq // C, chunk, 0)


def _hgrn(q, lf, kk, v, g, norm_g, batch, seq):
    shp = (batch, seq, D_MODEL)
    args = [a.reshape(shp) for a in (q, lf, kk, v, g)]
    spec = pl.BlockSpec((1, seq, LANES), lambda bb, h: (bb, 0, h))
    return pl.pallas_call(
        functools.partial(_hgrn_kernel, seq=seq),
        grid=(batch, N_HEADS_C),
        in_specs=[spec] * 5 + [pl.BlockSpec((1, LANES), lambda bb, h: (0, 0))],
        out_specs=spec,
        out_shape=jax.ShapeDtypeStruct(shp, BF16),
        scratch_shapes=[pltpu.VMEM((LANES, LANES), F32)],
        compiler_params=_params("parallel", "parallel"),
        name="hgrn",
    )(*args, norm_g.astype(F32).reshape(1, LANES))


def _layer_norm_rows(z, g, b):
    mu = jnp.mean(z, axis=1, keepdims=True)
    zc = z - mu
    var = jnp.mean(zc * zc, axis=1, keepdims=True)
    return zc * lax.rsqrt(var + LN_EPS) * g + b


def _outproj_ln_kernel(o_ref, x_ref, w_ref, g_ref, b_ref, wr_hi_ref, wr_lo_ref, rb_ref, y_ref, lg_ref):
    mix = jnp.dot(o_ref[...], w_ref[...], preferred_element_type=F32)
    y = _layer_norm_rows(DEEPNORM_ALPHA * x_ref[...] + mix, g_ref[...], b_ref[...])
    y_ref[...] = y
    y_hi = y.astype(BF16)
    y_lo = (y - y_hi.astype(F32)).astype(BF16)
    lg_ref[...] = (jnp.dot(y_hi, wr_hi_ref[...], preferred_element_type=F32)
                   + jnp.dot(y_lo, wr_hi_ref[...], preferred_element_type=F32)
                   + jnp.dot(y_hi, wr_lo_ref[...], preferred_element_type=F32)
                   + rb_ref[...])


def _outproj_ln(o2d, x2d, w_bf16, ln_g, ln_b, wr, rb, tm=512):
    t, d = x2d.shape
    kdim = o2d.shape[1]
    wr_hi = wr.astype(BF16)
    wr_lo = (wr - wr_hi.astype(F32)).astype(BF16)
    vec = lambda n: pl.BlockSpec((1, n), lambda i: (0, 0))
    return pl.pallas_call(
        _outproj_ln_kernel,
        grid=(t // tm,),
        in_specs=[pl.BlockSpec((tm, kdim), lambda i: (i, 0)),
                  pl.BlockSpec((tm, d), lambda i: (i, 0)),
                  pl.BlockSpec((kdim, d), lambda i: (0, 0)),
                  vec(d), vec(d),
                  pl.BlockSpec((d, LANES), lambda i: (0, 0)),
                  pl.BlockSpec((d, LANES), lambda i: (0, 0)),
                  vec(LANES)],
        out_specs=[pl.BlockSpec((tm, d), lambda i: (i, 0)),
                   pl.BlockSpec((tm, LANES), lambda i: (i, 0))],
        out_shape=[jax.ShapeDtypeStruct((t, d), F32),
                   jax.ShapeDtypeStruct((t, LANES), F32)],
        compiler_params=_params("parallel"),
        name="outproj_ln",
    )(o2d, x2d, w_bf16, ln_g.reshape(1, d), ln_b.reshape(1, d), wr_hi, wr_lo, rb)


def _expert_kernel(te_ref, nv_ref, xs_ref, wt_ref, w1_ref, w3_ref, w2_ref, o_ref, w1_s, w3_s, w2_s):
    i = pl.program_id(0)
    prev = te_ref[jnp.maximum(i - 1, 0)]

    @pl.when((i == 0) | (te_ref[i] != prev))
    def _cast_weights():
        w1_s[...] = w1_ref[0].astype(BF16)
        w3_s[...] = w3_ref[0].astype(BF16)
        w2_s[...] = w2_ref[0].astype(BF16)

    @pl.when(i < nv_ref[0])
    def _ffn():
        xb = xs_ref[...]
        a = jnp.dot(xb, w1_s[...], preferred_element_type=F32)
        u = jnp.dot(xb, w3_s[...], preferred_element_type=F32)
        hb = (a / (1.0 + jnp.exp(-a))) * u
        y = jnp.dot(hb.astype(BF16), w2_s[...], preferred_element_type=F32)
        o_ref[...] = (y * wt_ref[...]).astype(o_ref.dtype)

    @pl.when(i >= nv_ref[0])
    def _pad():
        o_ref[...] = jnp.zeros(o_ref.shape, o_ref.dtype)


def _expert_ffn(tile_expert, n_valid, xs, wt, w1, w3, w2):
    p, d = xs.shape
    f = w1.shape[2]
    n_tiles = p // MOE_TILE
    grid_spec = pltpu.PrefetchScalarGridSpec(
        num_scalar_prefetch=2,
        grid=(n_tiles,),
        in_specs=[pl.BlockSpec((MOE_TILE, d), lambda i, te, nv: (i, 0)),
                  pl.BlockSpec((MOE_TILE, 1), lambda i, te, nv: (i, 0)),
                  pl.BlockSpec((1, d, f), lambda i, te, nv: (te[i], 0, 0)),
                  pl.BlockSpec((1, d, f), lambda i, te, nv: (te[i], 0, 0)),
                  pl.BlockSpec((1, f, d), lambda i, te, nv: (te[i], 0, 0))],
        out_specs=pl.BlockSpec((MOE_TILE, d), lambda i, te, nv: (i, 0)),
        scratch_shapes=[pltpu.VMEM((d, f), BF16), pltpu.VMEM((d, f), BF16), pltpu.VMEM((f, d), BF16)],
    )
    return pl.pallas_call(
        _expert_kernel,
        grid_spec=grid_spec,
        out_shape=jax.ShapeDtypeStruct((p, d), BF16),
        compiler_params=_params("arbitrary"),
        name="expert_ffn",
    )(tile_expert, n_valid, xs, wt, w1, w3, w2)


def _combine_ln_kernel(x_ref, y0_ref, y1_ref, g_ref, b_ref, o_ref):
    ffn = y0_ref[...].astype(F32) + y1_ref[...].astype(F32)
    o_ref[...] = _layer_norm_rows(DEEPNORM_ALPHA * x_ref[...] + ffn, g_ref[...], b_ref[...])


def _combine_ln(x2d, y0, y1, ln_g, ln_b, tm=512):
    t, d = x2d.shape
    row = pl.BlockSpec((tm, d), lambda i: (i, 0))
    vec = pl.BlockSpec((1, d), lambda i: (0, 0))
    return pl.pallas_call(
        _combine_ln_kernel,
        grid=(t // tm,),
        in_specs=[row, row, row, vec, vec],
        out_specs=row,
        out_shape=jax.ShapeDtypeStruct((t, d), F32),
        compiler_params=_params("parallel"),
        name="combine_ln",
    )(x2d, y0, y1, ln_g.reshape(1, d), ln_b.reshape(1, d))


def _route(logits):
    t = logits.shape[0]
    lg = logits[:, :N_GROUPS]
    le = logits[:, N_GROUPS:N_GROUPS + N_EXPERTS].reshape(t, N_GROUPS, EXPERTS_PER_GROUP)
    grp = jnp.argmax(lg, axis=-1)
    pg = jnp.take_along_axis(jax.nn.softmax(lg, axis=-1), grp[:, None], axis=-1)
    le_g = jnp.take_along_axis(le, grp[:, None, None], axis=1)[:, 0]
    top_l, top_i = lax.top_k(le_g, TOPK_IN_GROUP)
    gate = pg * jax.nn.softmax(top_l, axis=-1)
    eid = (grp[:, None] * EXPERTS_PER_GROUP + top_i).astype(jnp.int32)

    a = t * TOPK_IN_GROUP
    eid_f = eid.reshape(-1)
    wt_f = gate.reshape(-1)
    order = jnp.argsort(eid_f).astype(jnp.int32)
    rank = jnp.argsort(order).astype(jnp.int32)
    counts = jnp.sum((eid_f[:, None] == jnp.arange(N_EXPERTS, dtype=jnp.int32)[None, :]).astype(jnp.int32), axis=0)
    starts = jnp.cumsum(counts) - counts
    padded = (counts + MOE_TILE - 1) // MOE_TILE * MOE_TILE
    pends = jnp.cumsum(padded)
    pstarts = pends - padded
    dest = pstarts[eid_f] + rank - starts[eid_f]

    n_tiles = a // MOE_TILE + N_EXPERTS
    tile_start = jnp.arange(n_tiles, dtype=jnp.int32) * MOE_TILE
    n_valid = (pends[-1] // MOE_TILE).astype(jnp.int32)
    tile_expert = jnp.minimum(jnp.searchsorted(pends, tile_start, side="right"), N_EXPERTS - 1).astype(jnp.int32)
    last_valid_expert = tile_expert[jnp.maximum(n_valid - 1, 0)]
    tile_expert = jnp.where(jnp.arange(n_tiles) < n_valid, tile_expert, last_valid_expert)

    slot = jnp.arange(n_tiles * MOE_TILE, dtype=jnp.int32)
    slot_e = tile_expert[slot // MOE_TILE]
    r = slot - pstarts[slot_e]
    ok = (r < counts[slot_e]) & (slot < pends[-1])
    src = order[jnp.clip(starts[slot_e] + r, 0, a - 1)]
    slot_tok = jnp.where(ok, src // TOPK_IN_GROUP, 0)
    slot_wt = jnp.where(ok, wt_f[src], 0.0)
    return tile_expert, n_valid.reshape(1), slot_tok, slot_wt, dest.reshape(t, TOPK_IN_GROUP)


def _moe_and_norm(x1, logits, w1, w3, w2, ln_g, ln_b):
    tile_expert, n_valid, slot_tok, slot_wt, dest = _route(logits)
    xs = x1.astype(BF16)[slot_tok]
    yb = _expert_ffn(tile_expert, n_valid, xs, slot_wt[:, None], w1, w3, w2)
    return _combine_ln(x1, yb[dest[:, 0]], yb[dest[:, 1]], ln_g, ln_b)


def kernel(x, ab_w_in, ab_w_out, c_w_in, c_w_out, c_norm_g, hgrn_lb_logits, ln_g, ln_b,
           router_g_w, router_g_b, router_e_w, router_e_b, exp_w1, exp_w3, exp_w2):
    batch, seq, d = x.shape
    t = batch * seq
    tables = _rope_lane_tables(seq)
    bias_tab = jnp.asarray(_dilated_bias_table())
    lb_all = jnp.cumsum(jax.nn.softmax(hgrn_lb_logits.astype(F32), axis=0), axis=0)
    lb_all = lb_all - lb_all[0:1]
    n_pairs = (AB_WIDTH // 2) // LANES

    xc = x.reshape(t, d)
    for l in range(DEPTH):
        j = l // 2
        if l % 2 == 0:
            h = _proj_ab(xc, ab_w_in[j].astype(BF16), tables, seq).reshape(batch, seq, 3 * AB_WIDTH)
            o = jnp.concatenate([_moba(h, n_pairs), _dilated(h, bias_tab, n_pairs)], axis=-1)
            o = o.reshape(t, AB_WIDTH)
            w_out = ab_w_out[j]
        else:
            q, lf, kk, v, g = _proj_c(xc, c_w_in[j].astype(BF16), lb_all[j])
            o = _hgrn(q, lf, kk, v, g, c_norm_g[j], batch, seq).reshape(t, d)
            w_out = c_w_out[j]
        wr = jnp.zeros((d, LANES), F32)
        wr = wr.at[:, :N_GROUPS].set(router_g_w[l]).at[:, N_GROUPS:N_GROUPS + N_EXPERTS].set(router_e_w[l])
        rb = jnp.zeros((1, LANES), F32)
        rb = rb.at[0, :N_GROUPS].set(router_g_b[l]).at[0, N_GROUPS:N_GROUPS + N_EXPERTS].set(router_e_b[l])
        x1, logits = _outproj_ln(o, xc, w_out.astype(BF16), ln_g[l, 0], ln_b[l, 0], wr, rb)
        xc = _moe_and_norm(x1, logits, exp_w1[l], exp_w3[l], exp_w2[l], ln_g[l, 1], ln_b[l, 1])
    return xc.reshape(batch, seq, d)
```

```python
import functools
import math

import numpy as np
import jax
import jax.numpy as jnp
from jax import lax
from jax.experimental import pallas as pl
from jax.experimental.pallas import tpu as pltpu

F32 = jnp.float32
BF16 = jnp.bfloat16

D_MODEL = 1024
DEPTH = 4
HEAD_DIM = 64
N_HEADS_A = 8
N_HEADS_B = 8
AB_WIDTH = (N_HEADS_A + N_HEADS_B) * HEAD_DIM
ROT_DIM = HEAD_DIM // 4
ROPE_THETA = 500000.0
MOBA_BLOCK = 256
MOBA_TOPK = 3
DILATED_BRANCHES = ((128, 1), (512, 4), (2048, 16))
HGRN_EXPAND = 128
N_HEADS_C = D_MODEL // HGRN_EXPAND
HGRN_CHUNK = 64
HGRN_SUB = 16
HGRN_UNROLL = 4
N_GROUPS = 4
EXPERTS_PER_GROUP = 8
N_EXPERTS = N_GROUPS * EXPERTS_PER_GROUP
TOPK_IN_GROUP = 2
D_EXPERT = D_MODEL // 2
DEEPNORM_ALPHA = (2.0 * DEPTH) ** 0.25
LN_EPS = 1e-5
RMS_EPS = 1e-6

LANES = 128
ATT_BLOCK = 256
MOBA_STEP = 4
DIL_STEP = 3
NEG_BIG = -1e30
MOE_TILE = 256
VMEM_LIMIT = 56 * 1024 * 1024

_NT = (((1,), (1,)), ((), ()))
_TN = (((0,), (0,)), ((), ()))


def _params(*sem):
    return pltpu.CompilerParams(dimension_semantics=sem, vmem_limit_bytes=VMEM_LIMIT)


def _proj_ab_kernel(x_ref, w_ref, cos_ref, sa_ref, sb_ref, o_ref):
    xb = x_ref[...].astype(BF16)
    sec = AB_WIDTH // 2
    for c in range(6):
        acc = jnp.dot(xb, w_ref[:, c * sec:(c + 1) * sec], preferred_element_type=F32)
        if c % 3 == 2:
            o_ref[:, c * sec:(c + 1) * sec] = acc.astype(BF16)
            continue
        cosv = cos_ref[...]
        sa = sa_ref[...]
        sb = sb_ref[...]
        for k in range(sec // LANES):
            a = acc[:, k * LANES:(k + 1) * LANES]
            r = a * cosv + pltpu.roll(a, LANES - ROT_DIM // 2, 1) * sa + pltpu.roll(a, ROT_DIM // 2, 1) * sb
            if c % 3 == 0:
                r = r * (HEAD_DIM ** -0.5)
            o_ref[:, c * sec + k * LANES:c * sec + (k + 1) * LANES] = r.astype(BF16)


def _rope_lane_tables(seq):
    half = ROT_DIM // 2
    inv = ROPE_THETA ** (-jnp.arange(half, dtype=F32) / half)
    ang = jnp.arange(seq, dtype=F32)[:, None] * inv[None, :]
    cos, sin = jnp.cos(ang), jnp.sin(ang)
    ones = jnp.ones((seq, HEAD_DIM - ROT_DIM), F32)
    zeros = jnp.zeros((seq, HEAD_DIM - ROT_DIM), F32)
    zh = jnp.zeros((seq, half), F32)
    cos_t = jnp.concatenate([cos, cos, ones], axis=1)
    sa_t = jnp.concatenate([-sin, zh, zeros], axis=1)
    sb_t = jnp.concatenate([zh, sin, zeros], axis=1)
    rep = LANES // HEAD_DIM
    return tuple(jnp.tile(t, (1, rep)) for t in (cos_t, sa_t, sb_t))


def _proj_ab(x2d, w_bf16, tables, seq, tm=512):
    t, d = x2d.shape
    n = w_bf16.shape[1]
    nseq = seq // tm
    tab_spec = pl.BlockSpec((tm, LANES), lambda i: (i % nseq, 0))
    return pl.pallas_call(
        _proj_ab_kernel,
        grid=(t // tm,),
        in_specs=[pl.BlockSpec((tm, d), lambda i: (i, 0)),
                  pl.BlockSpec((d, n), lambda i: (0, 0)),
                  tab_spec, tab_spec, tab_spec],
        out_specs=pl.BlockSpec((tm, n), lambda i: (i, 0)),
        out_shape=jax.ShapeDtypeStruct((t, n), BF16),
        compiler_params=_params("parallel"),
        name="proj_ab",
    )(x2d, w_bf16, *tables)


def _softmax_step(parts, vaug, h, m_scr, l_scr, acc_scr):
    m_prev = m_scr[h]
    mx = functools.reduce(jnp.maximum, parts)
    m_new = jnp.maximum(m_prev, jnp.max(mx, axis=1, keepdims=True))
    alpha = jnp.exp(m_prev - m_new)
    p = jnp.concatenate([jnp.exp(part - m_new).astype(BF16) for part in parts], axis=1)
    pv = jnp.dot(p, vaug, preferred_element_type=F32)
    acc_scr[h] = alpha * acc_scr[h] + pv[:, :LANES]
    l_scr[h] = alpha * l_scr[h] + pv[:, LANES:]
    m_scr[h] = m_new


def _lane_parts(s):
    return [s[:, c * LANES:(c + 1) * LANES] for c in range(s.shape[1] // LANES)]


def _softmax_init(m_scr, l_scr, acc_scr):
    m_scr[...] = jnp.full(m_scr.shape, -jnp.inf, F32)
    l_scr[...] = jnp.zeros(l_scr.shape, F32)
    acc_scr[...] = jnp.zeros(acc_scr.shape, F32)


def _softmax_finish(o_ref, l_scr, acc_scr):
    lane = lax.broadcasted_iota(jnp.int32, acc_scr.shape[1:], 1)
    o0 = acc_scr[0] / l_scr[0]
    o1 = acc_scr[1] / l_scr[1]
    o_ref[0] = jnp.where(lane < HEAD_DIM, o0, o1).astype(o_ref.dtype)


def _moba_kernel(q_ref, k_ref, v_ref, o_ref, kmean_scr, kaug_scr, vaug_scr, qaug_scr, m_scr, l_scr, acc_scr,
                 *, seq):
    bq = ATT_BLOCK
    pad = (MOBA_STEP - 1) * bq
    i = pl.program_id(2)

    @pl.when(i == 0)
    def _per_sequence_setup():
        rows = lax.broadcasted_iota(jnp.int32, (LANES, seq), 0)
        cols = lax.broadcasted_iota(jnp.int32, (LANES, seq), 1)
        lo = rows * MOBA_BLOCK
        ind = jnp.where((cols >= lo) & (cols < lo + MOBA_BLOCK), 1.0, 0.0).astype(BF16)
        kmean_scr[...] = jnp.dot(ind, k_ref[0], preferred_element_type=F32) * (1.0 / MOBA_BLOCK)
        blk_lane = lax.broadcasted_iota(jnp.int32, (bq, LANES), 1)
        kaug_scr[0:pad, 0:LANES] = jnp.zeros((pad, LANES), BF16)
        vaug_scr[0:pad, 0:LANES] = jnp.zeros((pad, LANES), BF16)
        for jb in range(-(MOBA_STEP - 1), seq // bq):
            tgt = jb if jb >= 0 else LANES - 1
            kaug_scr[pad + jb * bq:pad + (jb + 1) * bq, LANES:] = jnp.where(blk_lane == tgt, 1.0, 0.0).astype(BF16)
        kaug_scr[pad:, 0:LANES] = k_ref[0]
        vaug_scr[pad:, 0:LANES] = v_ref[0]
        vaug_scr[:, LANES:] = jnp.ones((seq + pad, LANES), BF16)

    q = q_ref[0]
    lane = lax.broadcasted_iota(jnp.int32, (bq, LANES), 1)
    lane_f = lane.astype(F32)
    km = kmean_scr[...]
    km_hi = km.astype(BF16)
    km_lo = (km - km_hi.astype(F32)).astype(BF16)
    for h in range(2):
        qh = jnp.where((lane >= h * HEAD_DIM) & (lane < (h + 1) * HEAD_DIM), q, jnp.zeros_like(q))
        g = (lax.dot_general(qh, km_hi, _NT, preferred_element_type=F32)
             + lax.dot_general(qh, km_lo, _NT, preferred_element_type=F32))
        g = jnp.where(lane < i, g, -jnp.inf)
        bias = jnp.where(lane == i, 0.0, NEG_BIG)
        for _ in range(MOBA_TOPK):
            mx = jnp.max(g, axis=1, keepdims=True)
            first = jnp.min(jnp.where(g == mx, lane_f, float(LANES)), axis=1, keepdims=True)
            pick = (lane_f == first) & (mx > -jnp.inf)
            bias = jnp.where(pick, 0.0, bias)
            g = jnp.where(pick, -jnp.inf, g)
        qaug_scr[h] = jnp.concatenate([qh, bias.astype(BF16)], axis=1)

    _softmax_init(m_scr, l_scr, acc_scr)
    n_parts = MOBA_STEP * bq // LANES
    own_parts = bq // LANES
    qi = lax.broadcasted_iota(jnp.int32, (bq, LANES), 0)

    def step(t, first):
        start = pl.multiple_of((i - MOBA_STEP * t) * bq, bq)
        kaug = kaug_scr[pl.ds(start, MOBA_STEP * bq), :]
        vaug = vaug_scr[pl.ds(start, MOBA_STEP * bq), :]
        for h in range(2):
            parts = _lane_parts(lax.dot_general(qaug_scr[h], kaug, _NT, preferred_element_type=F32))
            if first:
                for c in range(own_parts):
                    cc = n_parts - own_parts + c
                    parts[cc] = jnp.where(lane + c * LANES <= qi, parts[cc], NEG_BIG)
            _softmax_step(parts, vaug, h, m_scr, l_scr, acc_scr)

    step(0, True)

    def later(t, carry):
        step(t, False)
        return carry

    lax.fori_loop(1, (i + MOBA_STEP) // MOBA_STEP, later, 0)
    _softmax_finish(o_ref, l_scr, acc_scr)


def _moba(h3, n_pairs):
    b, seq, _ = h3.shape
    sec_blocks = (AB_WIDTH // 2) // LANES
    nq = seq // ATT_BLOCK
    return pl.pallas_call(
        functools.partial(_moba_kernel, seq=seq),
        grid=(b, n_pairs, nq),
        in_specs=[pl.BlockSpec((1, ATT_BLOCK, LANES), lambda bb, p, i: (bb, i, p)),
                  pl.BlockSpec((1, seq, LANES), lambda bb, p, i: (bb, 0, sec_blocks + p)),
                  pl.BlockSpec((1, seq, LANES), lambda bb, p, i: (bb, 0, 2 * sec_blocks + p))],
        out_specs=pl.BlockSpec((1, ATT_BLOCK, LANES), lambda bb, p, i: (bb, i, p)),
        out_shape=jax.ShapeDtypeStruct((b, seq, n_pairs * LANES), BF16),
        scratch_shapes=[pltpu.VMEM((LANES, LANES), F32),
                        pltpu.VMEM((seq + (MOBA_STEP - 1) * ATT_BLOCK, 2 * LANES), BF16),
                        pltpu.VMEM((seq + (MOBA_STEP - 1) * ATT_BLOCK, 2 * LANES), BF16),
                        pltpu.VMEM((2, ATT_BLOCK, 2 * LANES), BF16),
                        pltpu.VMEM((2, ATT_BLOCK, LANES), F32),
                        pltpu.VMEM((2, ATT_BLOCK, LANES), F32),
                        pltpu.VMEM((2, ATT_BLOCK, LANES), F32)],
        compiler_params=_params("parallel", "parallel", "arbitrary"),
        name="moba",
    )(h3, h3, h3)


def _dilated_n_blocks():
    return max(w for w, _ in DILATED_BRANCHES) // ATT_BLOCK + 1


def _dilated_bias_table():
    n_steps = -(-_dilated_n_blocks() // DIL_STEP)
    qi = np.arange(ATT_BLOCK)[:, None]
    ki = np.arange(ATT_BLOCK)[None, :]
    tabs = []
    for t in range(n_steps):
        groups = []
        for g in range(DIL_STEP):
            d = qi - ki + (DIL_STEP * t + DIL_STEP - 1 - g) * ATT_BLOCK
            cnt = np.zeros(d.shape, np.int64)
            for window, dil in DILATED_BRANCHES:
                cnt += ((d >= 0) & (d <= window) & (d % dil == 0)).astype(np.int64)
            groups.append(np.where(cnt > 0, np.log(np.maximum(cnt, 1).astype(np.float64)), NEG_BIG))
        tabs.append(np.concatenate(groups, axis=1))
    return np.stack(tabs).astype(np.float32)


def _dilated_kernel(q_ref, k_ref, v_ref, bias_ref, o_ref, kaug_scr, vaug_scr, m_scr, l_scr, acc_scr, *, seq):
    bq = ATT_BLOCK
    pad = (DIL_STEP - 1) * bq
    i = pl.program_id(2)

    @pl.when(i == 0)
    def _per_sequence_setup():
        flag_lane = lax.broadcasted_iota(jnp.int32, (pad, LANES), 1)
        kaug_scr[0:pad, 0:LANES] = jnp.zeros((pad, LANES), BF16)
        kaug_scr[0:pad, LANES:] = jnp.where(flag_lane == 0, 1.0, 0.0).astype(BF16)
        kaug_scr[pad:, 0:LANES] = k_ref[0]
        kaug_scr[pad:, LANES:] = jnp.zeros((seq, LANES), BF16)
        vaug_scr[0:pad, 0:LANES] = jnp.zeros((pad, LANES), BF16)
        vaug_scr[pad:, 0:LANES] = v_ref[0]
        vaug_scr[:, LANES:] = jnp.ones((seq + pad, LANES), BF16)

    q = q_ref[0]
    lane = lax.broadcasted_iota(jnp.int32, (bq, LANES), 1)
    neg_lane = jnp.where(lane == 0, NEG_BIG, 0.0).astype(BF16)
    qaug = [jnp.concatenate(
        [jnp.where((lane >= h * HEAD_DIM) & (lane < (h + 1) * HEAD_DIM), q, jnp.zeros_like(q)), neg_lane], axis=1)
        for h in range(2)]
    _softmax_init(m_scr, l_scr, acc_scr)

    def step(t, carry):
        start = pl.multiple_of((i - DIL_STEP * t) * bq, bq)
        kaug = kaug_scr[pl.ds(start, DIL_STEP * bq), :]
        vaug = vaug_scr[pl.ds(start, DIL_STEP * bq), :]
        bias = bias_ref[t]
        for h in range(2):
            s = lax.dot_general(qaug[h], kaug, _NT, preferred_element_type=F32) + bias
            _softmax_step(_lane_parts(s), vaug, h, m_scr, l_scr, acc_scr)
        return carry

    n_steps = (jnp.minimum(i, _dilated_n_blocks() - 1) + DIL_STEP) // DIL_STEP
    lax.fori_loop(0, n_steps, step, 0)
    _softmax_finish(o_ref, l_scr, acc_scr)


def _dilated(h3, bias_tab, n_pairs):
    b, seq, _ = h3.shape
    sec_blocks = (AB_WIDTH // 2) // LANES
    base = 3 * sec_blocks
    nq = seq // ATT_BLOCK
    pad = (DIL_STEP - 1) * ATT_BLOCK
    return pl.pallas_call(
        functools.partial(_dilated_kernel, seq=seq),
        grid=(b, n_pairs, nq),
        in_specs=[pl.BlockSpec((1, ATT_BLOCK, LANES), lambda bb, p, i: (bb, i, base + p)),
                  pl.BlockSpec((1, seq, LANES), lambda bb, p, i: (bb, 0, base + sec_blocks + p)),
                  pl.BlockSpec((1, seq, LANES), lambda bb, p, i: (bb, 0, base + 2 * sec_blocks + p)),
                  pl.BlockSpec(bias_tab.shape, lambda bb, p, i: (0, 0, 0))],
        out_specs=pl.BlockSpec((1, ATT_BLOCK, LANES), lambda bb, p, i: (bb, i, p)),
        out_shape=jax.ShapeDtypeStruct((b, seq, n_pairs * LANES), BF16),
        scratch_shapes=[pltpu.VMEM((seq + pad, 2 * LANES), BF16),
                        pltpu.VMEM((seq + pad, 2 * LANES), BF16),
                        pltpu.VMEM((2, ATT_BLOCK, LANES), F32),
                        pltpu.VMEM((2, ATT_BLOCK, LANES), F32),
                        pltpu.VMEM((2, ATT_BLOCK, LANES), F32)],
        compiler_params=_params("parallel", "parallel", "arbitrary"),
        name="dilated",
    )(h3, h3, h3, bias_tab)


def _proj_c_kernel(x_ref, w_ref, loglb_ref, log1mlb_ref, omlb_ref, q_ref, lf_ref, kk_ref, v_ref, g_ref):
    xb = x_ref[...].astype(BF16)
    d = D_MODEL

    def sec(c):
        return jnp.dot(xb, w_ref[:, c * d:(c + 1) * d], preferred_element_type=F32)

    q_ref[...] = sec(0)
    z = sec(1)
    log_sig = jnp.minimum(z, 0.0) - jnp.log1p(jnp.exp(-jnp.abs(z)))
    a = loglb_ref[...]
    c = log1mlb_ref[...] + log_sig
    lf_ref[...] = jnp.maximum(a, c) + jnp.log1p(jnp.exp(-jnp.abs(a - c)))
    kk_ref[...] = omlb_ref[...] / (1.0 + jnp.exp(z))
    v_ref[...] = sec(2)
    g_ref[...] = sec(3)


def _proj_c(x2d, w_bf16, lb, tm=256):
    t, d = x2d.shape
    n = w_bf16.shape[1]
    lb = lb.astype(F32).reshape(1, d)
    vec_spec = pl.BlockSpec((1, d), lambda i: (0, 0))
    out_spec = pl.BlockSpec((tm, d), lambda i: (i, 0))
    sds = jax.ShapeDtypeStruct((t, d), F32)
    return pl.pallas_call(
        _proj_c_kernel,
        grid=(t // tm,),
        in_specs=[pl.BlockSpec((tm, d), lambda i: (i, 0)),
                  pl.BlockSpec((d, n), lambda i: (0, 0)),
                  vec_spec, vec_spec, vec_spec],
        out_specs=[out_spec] * 5,
        out_shape=[sds] * 5,
        compiler_params=_params("parallel"),
        name="proj_c",
    )(x2d, w_bf16, jnp.log(lb), jnp.log1p(-lb), 1.0 - lb)


def _split3_bf16(x):
    h1 = x.astype(BF16)
    r1 = x - h1.astype(F32)
    h2 = r1.astype(BF16)
    h3 = (r1 - h2.astype(F32)).astype(BF16)
    return h1, h2, h3


def _hgrn_kernel(q_ref, lf_ref, kk_ref, v_ref, g_ref, ng_ref, o_ref, st_scr, *, seq):
    C, SUB = HGRN_CHUNK, HGRN_SUB
    st_scr[...] = jnp.zeros(st_scr.shape, F32)
    ri = lax.broadcasted_iota(jnp.int32, (C, C), 0)
    ci = lax.broadcasted_iota(jnp.int32, (C, C), 1)
    tri = jnp.where(ci <= ri, 1.0, 0.0).astype(BF16)
    row_c = lax.broadcasted_iota(jnp.int32, (C, LANES), 0)
    row_s = lax.broadcasted_iota(jnp.int32, (SUB, C), 0)
    lane_s = lax.broadcasted_iota(jnp.int32, (SUB, C), 1)
    ng = ng_ref[...]

    def chunk(c, carry):
        r0 = pl.multiple_of(c * C, C)
        qc = q_ref[0, pl.ds(r0, C), :]
        kc = kk_ref[0, pl.ds(r0, C), :]
        vc = v_ref[0, pl.ds(r0, C), :]
        l1, l2, l3 = _split3_bf16(lf_ref[0, pl.ds(r0, C), :])
        b = (jnp.dot(tri, l1, preferred_element_type=F32)
             + jnp.dot(tri, l2, preferred_element_type=F32)
             + jnp.dot(tri, l3, preferred_element_type=F32))
        b_last = b[C - 1:C, :]
        st = st_scr[...]
        inter = lax.dot_general((qc * jnp.exp(b)).astype(BF16), st.astype(BF16), _NT,
                                preferred_element_type=F32)
        vb = vc.astype(BF16)
        rows_a = []
        for sidx in range(C // SUB):
            lo = sidx * SUB
            q_i = qc[lo:lo + SUB, :]
            b_i = b[lo:lo + SUB, :]
            if sidx == 0:
                a_blk = jnp.zeros((SUB, C), F32)
            else:
                ref = b[lo - 1:lo, :]
                qt = (q_i * jnp.exp(b_i - ref)).astype(BF16)
                kt = jnp.where(row_c < lo, kc * jnp.exp(jnp.minimum(ref - b, 0.0)), 0.0).astype(BF16)
                a_blk = lax.dot_general(qt, kt, _NT, preferred_element_type=F32)
            for j in range(SUB):
                r_lo = 0 if j < 8 else 8
                bj = b[lo + j:lo + j + 1, :]
                kj = kc[lo + j:lo + j + 1, :]
                pj = q_i[r_lo:, :] * (jnp.exp(jnp.minimum(b_i[r_lo:, :] - bj, 0.0)) * kj)
                col = jnp.sum(pj, axis=1, keepdims=True)
                if r_lo:
                    col = jnp.concatenate([jnp.zeros((r_lo, 1), F32), col], axis=0)
                a_blk = jnp.where((lane_s == lo + j) & (row_s >= j), col, a_blk)
            rows_a.append(a_blk)
        a_full = jnp.concatenate(rows_a, axis=0)
        o = inter + jnp.dot(a_full.astype(BF16), vb, preferred_element_type=F32)
        kd = (kc * jnp.exp(b_last - b)).astype(BF16)
        st_scr[...] = st * jnp.exp(b_last) + lax.dot_general(vb, kd, _TN, preferred_element_type=F32)
        o = o * lax.rsqrt(jnp.mean(o * o, axis=1, keepdims=True) + RMS_EPS) * ng
        gc = g_ref[0, pl.ds(r0, C), :]
        o_ref[0, pl.ds(r0, C), :] = (o * (gc / (1.0 + jnp.exp(-gc)))).astype(o_ref.dtype)
        return carry

    lax.fori_loop(0, seq // C, chunk, 0, unroll=HGRN_UNROLL)


def _hgrn(q, lf, kk, v, g, norm_g, batch, seq):
    shp = (batch, seq, D_MODEL)
    args = [a.reshape(shp) for a in (q, lf, kk, v, g)]
    spec = pl.BlockSpec((1, seq, LANES), lambda bb, h: (bb, 0, h))
    return pl.pallas_call(
        functools.partial(_hgrn_kernel, seq=seq),
        grid=(batch, N_HEADS_C),
        in_specs=[spec] * 5 + [pl.BlockSpec((1, LANES), lambda bb, h: (0, 0))],
        out_specs=spec,
        out_shape=jax.ShapeDtypeStruct(shp, BF16),
        scratch_shapes=[pltpu.VMEM((LANES, LANES), F32)],
        compiler_params=_params("parallel", "parallel"),
        name="hgrn",
    )(*args, norm_g.astype(F32).reshape(1, LANES))


def _layer_norm_rows(z, g, b):
    mu = jnp.mean(z, axis=1, keepdims=True)
    zc = z - mu
    var = jnp.mean(zc * zc, axis=1, keepdims=True)
    return zc * lax.rsqrt(var + LN_EPS) * g + b


def _first_lane_of_max(vals, lane_f):
    mx = jnp.max(vals, axis=1, keepdims=True)
    return mx, jnp.min(jnp.where(vals == mx, lane_f, float(LANES)), axis=1, keepdims=True)


def _outproj_ln_kernel(o_ref, x_ref, w_ref, g_ref, b_ref, wr_hi_ref, wr_lo_ref, rb_ref,
                       y_ref, yb_ref, fields_ref, info_ref, cnt_ref):
    mix = jnp.dot(o_ref[...], w_ref[...], preferred_element_type=F32)
    y = _layer_norm_rows(DEEPNORM_ALPHA * x_ref[...] + mix, g_ref[...], b_ref[...])
    y_ref[...] = y
    y_hi = y.astype(BF16)
    yb_ref[...] = y_hi
    y_lo = (y - y_hi.astype(F32)).astype(BF16)
    lg = (jnp.dot(y_hi, wr_hi_ref[...], preferred_element_type=F32)
          + jnp.dot(y_lo, wr_hi_ref[...], preferred_element_type=F32)
          + jnp.dot(y_hi, wr_lo_ref[...], preferred_element_type=F32)
          + rb_ref[...])
    lane = lax.broadcasted_iota(jnp.int32, lg.shape, 1)
    lane_f = lane.astype(F32)
    is_g = lane < N_GROUPS
    mg, grp = _first_lane_of_max(jnp.where(is_g, lg, -jnp.inf), lane_f)
    pg = 1.0 / jnp.sum(jnp.where(is_g, jnp.exp(lg - mg), 0.0), axis=1, keepdims=True)
    lo = float(N_GROUPS) + float(EXPERTS_PER_GROUP) * grp
    le = jnp.where((lane_f >= lo) & (lane_f < lo + float(EXPERTS_PER_GROUP)), lg, -jnp.inf)
    m1, i1 = _first_lane_of_max(le, lane_f)
    m2, i2 = _first_lane_of_max(jnp.where(lane_f == i1, -jnp.inf, le), lane_f)
    e21 = jnp.exp(m2 - m1)
    w0 = pg / (1.0 + e21)
    w1 = pg * e21 / (1.0 + e21)
    eid0 = i1 - float(N_GROUPS)
    eid1 = i2 - float(N_GROUPS)

    @pl.when(pl.program_id(0) == 0)
    def _():
        cnt_ref[...] = jnp.zeros(cnt_ref.shape, F32)

    hits = jnp.where(lane_f == eid0, 1.0, 0.0) + jnp.where(lane_f == eid1, 1.0, 0.0)
    cnt_ref[...] += jnp.sum(hits, axis=0, keepdims=True)
    fields = jnp.where(lane == 0, eid0, jnp.where(lane == 1, eid1, jnp.where(lane == 2, w0,
                       jnp.where(lane == 3, w1, 0.0))))
    fields_ref[...] = fields
    info_ref[...] = fields.T[:8, :]


def _outproj_ln(o2d, x2d, w_bf16, ln_g, ln_b, wr, rb, tm=512):
    t, d = x2d.shape
    kdim = o2d.shape[1]
    wr_hi = wr.astype(BF16)
    wr_lo = (wr - wr_hi.astype(F32)).astype(BF16)
    vec = lambda n: pl.BlockSpec((1, n), lambda i: (0, 0))
    return pl.pallas_call(
        _outproj_ln_kernel,
        grid=(t // tm,),
        in_specs=[pl.BlockSpec((tm, kdim), lambda i: (i, 0)),
                  pl.BlockSpec((tm, d), lambda i: (i, 0)),
                  pl.BlockSpec((kdim, d), lambda i: (0, 0)),
                  vec(d), vec(d),
                  pl.BlockSpec((d, LANES), lambda i: (0, 0)),
                  pl.BlockSpec((d, LANES), lambda i: (0, 0)),
                  vec(LANES)],
        out_specs=[pl.BlockSpec((tm, d), lambda i: (i, 0)),
                   pl.BlockSpec((tm, d), lambda i: (i, 0)),
                   pl.BlockSpec((tm, LANES), lambda i: (i, 0)),
                   pl.BlockSpec((8, tm), lambda i: (0, i)),
                   vec(LANES)],
        out_shape=[jax.ShapeDtypeStruct((t, d), F32),
                   jax.ShapeDtypeStruct((t, d), BF16),
                   jax.ShapeDtypeStruct((t, LANES), F32),
                   jax.ShapeDtypeStruct((8, t), F32),
                   jax.ShapeDtypeStruct((1, LANES), F32)],
        compiler_params=_params("arbitrary"),
        name="outproj_ln",
    )(o2d, x2d, w_bf16, ln_g.reshape(1, d), ln_b.reshape(1, d), wr_hi, wr_lo, rb)


def _expert_kernel(te_ref, nv_ref, xs_ref, w1_ref, w3_ref, w2_ref, o_ref, w1_s, w3_s, w2_s):
    i = pl.program_id(0)
    prev = te_ref[jnp.maximum(i - 1, 0)]

    @pl.when((i == 0) | (te_ref[i] != prev))
    def _cast_weights():
        w1_s[...] = w1_ref[0].astype(BF16)
        w3_s[...] = w3_ref[0].astype(BF16)
        w2_s[...] = w2_ref[0].astype(BF16)

    @pl.when(i < nv_ref[0])
    def _ffn():
        xb = xs_ref[...]
        a = jnp.dot(xb, w1_s[...], preferred_element_type=F32)
        u = jnp.dot(xb, w3_s[...], preferred_element_type=F32)
        hb = (a / (1.0 + jnp.exp(-a))) * u
        o_ref[...] = jnp.dot(hb.astype(BF16), w2_s[...], preferred_element_type=F32).astype(o_ref.dtype)

    @pl.when(i >= nv_ref[0])
    def _pad():
        o_ref[...] = jnp.zeros(o_ref.shape, o_ref.dtype)


def _expert_ffn(tile_expert, n_valid, xs, w1, w3, w2):
    p, d = xs.shape
    f = w1.shape[2]
    n_tiles = p // MOE_TILE
    grid_spec = pltpu.PrefetchScalarGridSpec(
        num_scalar_prefetch=2,
        grid=(n_tiles,),
        in_specs=[pl.BlockSpec((MOE_TILE, d), lambda i, te, nv: (i, 0)),
                  pl.BlockSpec((1, d, f), lambda i, te, nv: (te[i], 0, 0)),
                  pl.BlockSpec((1, d, f), lambda i, te, nv: (te[i], 0, 0)),
                  pl.BlockSpec((1, f, d), lambda i, te, nv: (te[i], 0, 0))],
        out_specs=pl.BlockSpec((MOE_TILE, d), lambda i, te, nv: (i, 0)),
        scratch_shapes=[pltpu.VMEM((d, f), BF16), pltpu.VMEM((d, f), BF16), pltpu.VMEM((f, d), BF16)],
    )
    return pl.pallas_call(
        _expert_kernel,
        grid_spec=grid_spec,
        out_shape=jax.ShapeDtypeStruct((p, d), BF16),
        compiler_params=_params("arbitrary"),
        name="expert_ffn",
    )(tile_expert, n_valid, xs, w1, w3, w2)


def _combine_ln_kernel(x_ref, y0_ref, y1_ref, fields_ref, g_ref, b_ref, o_ref):
    fields = fields_ref[...]
    ffn = fields[:, 2:3] * y0_ref[...].astype(F32) + fields[:, 3:4] * y1_ref[...].astype(F32)
    o_ref[...] = _layer_norm_rows(DEEPNORM_ALPHA * x_ref[...] + ffn, g_ref[...], b_ref[...])


def _combine_ln(x2d, y0, y1, fields, ln_g, ln_b, tm=512):
    t, d = x2d.shape
    row = pl.BlockSpec((tm, d), lambda i: (i, 0))
    vec = pl.BlockSpec((1, d), lambda i: (0, 0))
    return pl.pallas_call(
        _combine_ln_kernel,
        grid=(t // tm,),
        in_specs=[row, row, row, pl.BlockSpec((tm, LANES), lambda i: (i, 0)), vec, vec],
        out_specs=row,
        out_shape=jax.ShapeDtypeStruct((t, d), F32),
        compiler_params=_params("parallel"),
        name="combine_ln",
    )(x2d, y0, y1, fields, ln_g.reshape(1, d), ln_b.reshape(1, d))


def _dispatch_plan(info, counts_f):
    t = info.shape[1]
    a = t * TOPK_IN_GROUP
    n_fill = N_EXPERTS * MOE_TILE
    n_tiles = a // MOE_TILE + N_EXPERTS
    experts = jnp.arange(N_EXPERTS, dtype=jnp.int32)
    counts = counts_f[0, :N_EXPERTS].astype(jnp.int32)
    padded = (counts + MOE_TILE - 1) // MOE_TILE * MOE_TILE
    pends = jnp.cumsum(padded)
    fill_ends = jnp.cumsum(padded - counts)
    fill = jnp.arange(n_fill, dtype=jnp.int32)
    fill_key = jnp.sum((fill_ends[None, :] <= fill[:, None]).astype(jnp.int32), axis=1)
    tok = jnp.arange(t, dtype=jnp.int32)
    keys = jnp.concatenate([info[0].astype(jnp.int32), info[1].astype(jnp.int32), fill_key])
    toks = jnp.concatenate([tok, tok, fill % t])
    flat = jnp.arange(a + n_fill, dtype=jnp.int32)
    _, slot_tok, slot_flat = lax.sort((keys, toks, flat), num_keys=1, is_stable=True)
    _, slot_of = lax.sort((slot_flat, flat), num_keys=1)
    tile_start = jnp.arange(n_tiles, dtype=jnp.int32) * MOE_TILE
    tile_expert = jnp.sum((pends[None, :] <= tile_start[:, None]).astype(jnp.int32), axis=1)
    last_used = jnp.max(jnp.where(counts > 0, experts, 0))
    tile_expert = jnp.minimum(tile_expert, last_used)
    n_valid = (pends[-1] // MOE_TILE).astype(jnp.int32).reshape(1)
    return tile_expert, n_valid, slot_tok, slot_of[:t], slot_of[t:a]


def _moe_and_norm(x1, x1_bf16, fields, info, counts_f, w1, w3, w2, ln_g, ln_b):
    tile_expert, n_valid, slot_tok, dest0, dest1 = _dispatch_plan(info, counts_f)
    yb = _expert_ffn(tile_expert, n_valid, x1_bf16[slot_tok], w1, w3, w2)
    return _combine_ln(x1, yb[dest0], yb[dest1], fields, ln_g, ln_b)


def kernel(x, ab_w_in, ab_w_out, c_w_in, c_w_out, c_norm_g, hgrn_lb_logits, ln_g, ln_b,
           router_g_w, router_g_b, router_e_w, router_e_b, exp_w1, exp_w3, exp_w2):
    batch, seq, d = x.shape
    t = batch * seq
    tables = _rope_lane_tables(seq)
    bias_tab = jnp.asarray(_dilated_bias_table())
    lb_all = jnp.cumsum(jax.nn.softmax(hgrn_lb_logits.astype(F32), axis=0), axis=0)
    lb_all = lb_all - lb_all[0:1]
    n_pairs = (AB_WIDTH // 2) // LANES

    xc = x.reshape(t, d)
    for l in range(DEPTH):
        j = l // 2
        if l % 2 == 0:
            h = _proj_ab(xc, ab_w_in[j].astype(BF16), tables, seq).reshape(batch, seq, 3 * AB_WIDTH)
            o = jnp.concatenate([_moba(h, n_pairs), _dilated(h, bias_tab, n_pairs)], axis=-1)
            o = o.reshape(t, AB_WIDTH)
            w_out = ab_w_out[j]
        else:
            q, lf, kk, v, g = _proj_c(xc, c_w_in[j].astype(BF16), lb_all[j])
            o = _hgrn(q, lf, kk, v, g, c_norm_g[j], batch, seq).reshape(t, d)
            w_out = c_w_out[j]
        wr = jnp.zeros((d, LANES), F32)
        wr = wr.at[:, :N_GROUPS].set(router_g_w[l]).at[:, N_GROUPS:N_GROUPS + N_EXPERTS].set(router_e_w[l])
        rb = jnp.zeros((1, LANES), F32)
        rb = rb.at[0, :N_GROUPS].set(router_g_b[l]).at[0, N_GROUPS:N_GROUPS + N_EXPERTS].set(router_e_b[l])
        x1, x1_bf16, fields, info, counts_f = _outproj_ln(o, xc, w_out.astype(BF16), ln_g[l, 0], ln_b[l, 0], wr, rb)
        xc = _moe_and_norm(x1, x1_bf16, fields, info, counts_f, exp_w1[l], exp_w3[l], exp_w2[l],
                           ln_g[l, 1], ln_b[l, 1])
    return xc.reshape(batch, seq, d)
```

```python
import functools
import math

import numpy as np
import jax
import jax.numpy as jnp
from jax import lax
from jax.experimental import pallas as pl
from jax.experimental.pallas import tpu as pltpu

F32 = jnp.float32
BF16 = jnp.bfloat16

D_MODEL = 1024
DEPTH = 4
HEAD_DIM = 64
N_HEADS_A = 8
N_HEADS_B = 8
AB_WIDTH = (N_HEADS_A + N_HEADS_B) * HEAD_DIM
ROT_DIM = HEAD_DIM // 4
ROPE_THETA = 500000.0
MOBA_BLOCK = 256
MOBA_TOPK = 3
DILATED_BRANCHES = ((128, 1), (512, 4), (2048, 16))
HGRN_EXPAND = 128
N_HEADS_C = D_MODEL // HGRN_EXPAND
HGRN_CHUNK = 64
HGRN_SUB = 16
HGRN_UNROLL = 4
HGRN_FAST_LIMIT = 60.0
N_GROUPS = 4
EXPERTS_PER_GROUP = 8
N_EXPERTS = N_GROUPS * EXPERTS_PER_GROUP
TOPK_IN_GROUP = 2
D_EXPERT = D_MODEL // 2
DEEPNORM_ALPHA = (2.0 * DEPTH) ** 0.25
LN_EPS = 1e-5
RMS_EPS = 1e-6

LANES = 128
ATT_BLOCK = 256
MOBA_STEP = 4
MOBA_SEL_ROWS = 16
DIL_STEP = 3
NEG_BIG = -1e30
MOE_TILE = 256
VMEM_LIMIT = 56 * 1024 * 1024

_NT = (((1,), (1,)), ((), ()))
_TN = (((0,), (0,)), ((), ()))


def _params(*sem):
    return pltpu.CompilerParams(dimension_semantics=sem, vmem_limit_bytes=VMEM_LIMIT)


def _proj_ab_kernel(x_ref, w_ref, cos_ref, sa_ref, sb_ref, o_ref):
    xb = x_ref[...].astype(BF16)
    sec = AB_WIDTH // 2
    for c in range(6):
        acc = jnp.dot(xb, w_ref[:, c * sec:(c + 1) * sec], preferred_element_type=F32)
        if c % 3 == 2:
            o_ref[:, c * sec:(c + 1) * sec] = acc.astype(BF16)
            continue
        cosv = cos_ref[...]
        sa = sa_ref[...]
        sb = sb_ref[...]
        for k in range(sec // LANES):
            a = acc[:, k * LANES:(k + 1) * LANES]
            r = a * cosv + pltpu.roll(a, LANES - ROT_DIM // 2, 1) * sa + pltpu.roll(a, ROT_DIM // 2, 1) * sb
            if c % 3 == 0:
                r = r * (HEAD_DIM ** -0.5)
            o_ref[:, c * sec + k * LANES:c * sec + (k + 1) * LANES] = r.astype(BF16)


def _rope_lane_tables(seq):
    half = ROT_DIM // 2
    inv = ROPE_THETA ** (-jnp.arange(half, dtype=F32) / half)
    ang = jnp.arange(seq, dtype=F32)[:, None] * inv[None, :]
    cos, sin = jnp.cos(ang), jnp.sin(ang)
    ones = jnp.ones((seq, HEAD_DIM - ROT_DIM), F32)
    zeros = jnp.zeros((seq, HEAD_DIM - ROT_DIM), F32)
    zh = jnp.zeros((seq, half), F32)
    cos_t = jnp.concatenate([cos, cos, ones], axis=1)
    sa_t = jnp.concatenate([-sin, zh, zeros], axis=1)
    sb_t = jnp.concatenate([zh, sin, zeros], axis=1)
    rep = LANES // HEAD_DIM
    return tuple(jnp.tile(t, (1, rep)) for t in (cos_t, sa_t, sb_t))


def _proj_ab(x2d, w_bf16, tables, seq, tm=512):
    t, d = x2d.shape
    n = w_bf16.shape[1]
    nseq = seq // tm
    tab_spec = pl.BlockSpec((tm, LANES), lambda i: (i % nseq, 0))
    return pl.pallas_call(
        _proj_ab_kernel,
        grid=(t // tm,),
        in_specs=[pl.BlockSpec((tm, d), lambda i: (i, 0)),
                  pl.BlockSpec((d, n), lambda i: (0, 0)),
                  tab_spec, tab_spec, tab_spec],
        out_specs=pl.BlockSpec((tm, n), lambda i: (i, 0)),
        out_shape=jax.ShapeDtypeStruct((t, n), BF16),
        compiler_params=_params("parallel"),
        name="proj_ab",
    )(x2d, w_bf16, *tables)


def _softmax_step(parts, vaug, h, m_scr, l_scr, acc_scr):
    m_prev = m_scr[h]
    mx = functools.reduce(jnp.maximum, parts)
    m_new = jnp.maximum(m_prev, jnp.max(mx, axis=1, keepdims=True))
    alpha = jnp.exp(m_prev - m_new)
    p = jnp.concatenate([jnp.exp(part - m_new).astype(BF16) for part in parts], axis=1)
    pv = jnp.dot(p, vaug, preferred_element_type=F32)
    acc_scr[h] = alpha * acc_scr[h] + pv[:, :LANES]
    l_scr[h] = alpha * l_scr[h] + pv[:, LANES:]
    m_scr[h] = m_new


def _lane_parts(s):
    return [s[:, c * LANES:(c + 1) * LANES] for c in range(s.shape[1] // LANES)]


def _softmax_init(m_scr, l_scr, acc_scr):
    m_scr[...] = jnp.full(m_scr.shape, -jnp.inf, F32)
    l_scr[...] = jnp.zeros(l_scr.shape, F32)
    acc_scr[...] = jnp.zeros(acc_scr.shape, F32)


def _softmax_finish(o_ref, l_scr, acc_scr):
    lane = lax.broadcasted_iota(jnp.int32, acc_scr.shape[1:], 1)
    o0 = acc_scr[0] / l_scr[0]
    o1 = acc_scr[1] / l_scr[1]
    o_ref[0] = jnp.where(lane < HEAD_DIM, o0, o1).astype(o_ref.dtype)


def _moba_kernel(q_ref, k_ref, v_ref, o_ref, kmean_scr, kaug_scr, vaug_scr, qaug_scr, m_scr, l_scr, acc_scr,
                 *, seq):
    bq = ATT_BLOCK
    pad = (MOBA_STEP - 1) * bq
    i = pl.program_id(2)

    @pl.when(i == 0)
    def _per_sequence_setup():
        rows = lax.broadcasted_iota(jnp.int32, (LANES, seq), 0)
        cols = lax.broadcasted_iota(jnp.int32, (LANES, seq), 1)
        lo = rows * MOBA_BLOCK
        ind = jnp.where((cols >= lo) & (cols < lo + MOBA_BLOCK), 1.0, 0.0).astype(BF16)
        kmean_scr[...] = jnp.dot(ind, k_ref[0], preferred_element_type=F32) * (1.0 / MOBA_BLOCK)
        blk_lane = lax.broadcasted_iota(jnp.int32, (bq, LANES), 1)
        kaug_scr[0:pad, 0:LANES] = jnp.zeros((pad, LANES), BF16)
        vaug_scr[0:pad, 0:LANES] = jnp.zeros((pad, LANES), BF16)
        for jb in range(-(MOBA_STEP - 1), seq // bq):
            tgt = jb if jb >= 0 else LANES - 1
            kaug_scr[pad + jb * bq:pad + (jb + 1) * bq, LANES:] = jnp.where(blk_lane == tgt, 1.0, 0.0).astype(BF16)
        kaug_scr[pad:, 0:LANES] = k_ref[0]
        vaug_scr[pad:, 0:LANES] = v_ref[0]
        vaug_scr[:, LANES:] = jnp.ones((seq + pad, LANES), BF16)

    q = q_ref[0]
    lane = lax.broadcasted_iota(jnp.int32, (bq, LANES), 1)
    km = kmean_scr[0:MOBA_SEL_ROWS, :]
    km_hi = km.astype(BF16)
    km_lo = (km - km_hi.astype(F32)).astype(BF16)
    blk = lax.broadcasted_iota(jnp.int32, (MOBA_SEL_ROWS, bq), 0)
    blk_f = blk.astype(F32)
    for h in range(2):
        qh = jnp.where((lane >= h * HEAD_DIM) & (lane < (h + 1) * HEAD_DIM), q, jnp.zeros_like(q))
        g = (lax.dot_general(km_hi, qh, _NT, preferred_element_type=F32)
             + lax.dot_general(km_lo, qh, _NT, preferred_element_type=F32))
        g = jnp.where(blk < i, g, -jnp.inf)
        bias = jnp.where(blk == i, 0.0, NEG_BIG)
        for _ in range(MOBA_TOPK):
            mx = jnp.max(g, axis=0, keepdims=True)
            first = jnp.min(jnp.where(g == mx, blk_f, float(MOBA_SEL_ROWS)), axis=0, keepdims=True)
            pick = (blk_f == first) & (mx > -jnp.inf)
            bias = jnp.where(pick, 0.0, bias)
            g = jnp.where(pick, -jnp.inf, g)
        bias = jnp.concatenate([bias, jnp.full((LANES - MOBA_SEL_ROWS, bq), NEG_BIG, F32)], axis=0).T
        qaug_scr[h] = jnp.concatenate([qh, bias.astype(BF16)], axis=1)

    _softmax_init(m_scr, l_scr, acc_scr)
    n_parts = MOBA_STEP * bq // LANES
    own_parts = bq // LANES
    qi = lax.broadcasted_iota(jnp.int32, (bq, LANES), 0)

    def step(t, first):
        start = pl.multiple_of((i - MOBA_STEP * t) * bq, bq)
        kaug = kaug_scr[pl.ds(start, MOBA_STEP * bq), :]
        vaug = vaug_scr[pl.ds(start, MOBA_STEP * bq), :]
        for h in range(2):
            parts = _lane_parts(lax.dot_general(qaug_scr[h], kaug, _NT, preferred_element_type=F32))
            if first:
                for c in range(own_parts):
                    cc = n_parts - own_parts + c
                    parts[cc] = jnp.where(lane + c * LANES <= qi, parts[cc], NEG_BIG)
            _softmax_step(parts, vaug, h, m_scr, l_scr, acc_scr)

    step(0, True)

    def later(t, carry):
        step(t, False)
        return carry

    lax.fori_loop(1, (i + MOBA_STEP) // MOBA_STEP, later, 0)
    _softmax_finish(o_ref, l_scr, acc_scr)


def _moba(h3, n_pairs):
    b, seq, _ = h3.shape
    sec_blocks = (AB_WIDTH // 2) // LANES
    nq = seq // ATT_BLOCK
    assert seq % ATT_BLOCK == 0 and nq <= MOBA_SEL_ROWS
    return pl.pallas_call(
        functools.partial(_moba_kernel, seq=seq),
        grid=(b, n_pairs, nq),
        in_specs=[pl.BlockSpec((1, ATT_BLOCK, LANES), lambda bb, p, i: (bb, i, p)),
                  pl.BlockSpec((1, seq, LANES), lambda bb, p, i: (bb, 0, sec_blocks + p)),
                  pl.BlockSpec((1, seq, LANES), lambda bb, p, i: (bb, 0, 2 * sec_blocks + p))],
        out_specs=pl.BlockSpec((1, ATT_BLOCK, LANES), lambda bb, p, i: (bb, i, p)),
        out_shape=jax.ShapeDtypeStruct((b, seq, n_pairs * LANES), BF16),
        scratch_shapes=[pltpu.VMEM((LANES, LANES), F32),
                        pltpu.VMEM((seq + (MOBA_STEP - 1) * ATT_BLOCK, 2 * LANES), BF16),
                        pltpu.VMEM((seq + (MOBA_STEP - 1) * ATT_BLOCK, 2 * LANES), BF16),
                        pltpu.VMEM((2, ATT_BLOCK, 2 * LANES), BF16),
                        pltpu.VMEM((2, ATT_BLOCK, LANES), F32),
                        pltpu.VMEM((2, ATT_BLOCK, LANES), F32),
                        pltpu.VMEM((2, ATT_BLOCK, LANES), F32)],
        compiler_params=_params("parallel", "parallel", "arbitrary"),
        name="moba",
    )(h3, h3, h3)


def _dilated_n_blocks():
    return max(w for w, _ in DILATED_BRANCHES) // ATT_BLOCK + 1


def _dilated_bias_table():
    n_steps = -(-_dilated_n_blocks() // DIL_STEP)
    qi = np.arange(ATT_BLOCK)[:, None]
    ki = np.arange(ATT_BLOCK)[None, :]
    tabs = []
    for t in range(n_steps):
        groups = []
        for g in range(DIL_STEP):
            d = qi - ki + (DIL_STEP * t + DIL_STEP - 1 - g) * ATT_BLOCK
            cnt = np.zeros(d.shape, np.int64)
            for window, dil in DILATED_BRANCHES:
                cnt += ((d >= 0) & (d <= window) & (d % dil == 0)).astype(np.int64)
            groups.append(np.where(cnt > 0, np.log(np.maximum(cnt, 1).astype(np.float64)), NEG_BIG))
        tabs.append(np.concatenate(groups, axis=1))
    return np.stack(tabs).astype(np.float32)


def _dilated_kernel(q_ref, k_ref, v_ref, bias_ref, o_ref, kaug_scr, vaug_scr, m_scr, l_scr, acc_scr, *, seq):
    bq = ATT_BLOCK
    pad = (DIL_STEP - 1) * bq
    i = pl.program_id(2)

    @pl.when(i == 0)
    def _per_sequence_setup():
        flag_lane = lax.broadcasted_iota(jnp.int32, (pad, LANES), 1)
        kaug_scr[0:pad, 0:LANES] = jnp.zeros((pad, LANES), BF16)
        kaug_scr[0:pad, LANES:] = jnp.where(flag_lane == 0, 1.0, 0.0).astype(BF16)
        kaug_scr[pad:, 0:LANES] = k_ref[0]
        kaug_scr[pad:, LANES:] = jnp.zeros((seq, LANES), BF16)
        vaug_scr[0:pad, 0:LANES] = jnp.zeros((pad, LANES), BF16)
        vaug_scr[pad:, 0:LANES] = v_ref[0]
        vaug_scr[:, LANES:] = jnp.ones((seq + pad, LANES), BF16)

    q = q_ref[0]
    lane = lax.broadcasted_iota(jnp.int32, (bq, LANES), 1)
    neg_lane = jnp.where(lane == 0, NEG_BIG, 0.0).astype(BF16)
    qaug = [jnp.concatenate(
        [jnp.where((lane >= h * HEAD_DIM) & (lane < (h + 1) * HEAD_DIM), q, jnp.zeros_like(q)), neg_lane], axis=1)
        for h in range(2)]
    _softmax_init(m_scr, l_scr, acc_scr)

    def step(t, carry):
        start = pl.multiple_of((i - DIL_STEP * t) * bq, bq)
        kaug = kaug_scr[pl.ds(start, DIL_STEP * bq), :]
        vaug = vaug_scr[pl.ds(start, DIL_STEP * bq), :]
        bias = bias_ref[t]
        for h in range(2):
            s = lax.dot_general(qaug[h], kaug, _NT, preferred_element_type=F32) + bias
            _softmax_step(_lane_parts(s), vaug, h, m_scr, l_scr, acc_scr)
        return carry

    n_steps = (jnp.minimum(i, _dilated_n_blocks() - 1) + DIL_STEP) // DIL_STEP
    lax.fori_loop(0, n_steps, step, 0)
    _softmax_finish(o_ref, l_scr, acc_scr)


def _dilated(h3, bias_tab, n_pairs):
    b, seq, _ = h3.shape
    sec_blocks = (AB_WIDTH // 2) // LANES
    base = 3 * sec_blocks
    nq = seq // ATT_BLOCK
    pad = (DIL_STEP - 1) * ATT_BLOCK
    return pl.pallas_call(
        functools.partial(_dilated_kernel, seq=seq),
        grid=(b, n_pairs, nq),
        in_specs=[pl.BlockSpec((1, ATT_BLOCK, LANES), lambda bb, p, i: (bb, i, base + p)),
                  pl.BlockSpec((1, seq, LANES), lambda bb, p, i: (bb, 0, base + sec_blocks + p)),
                  pl.BlockSpec((1, seq, LANES), lambda bb, p, i: (bb, 0, base + 2 * sec_blocks + p)),
                  pl.BlockSpec(bias_tab.shape, lambda bb, p, i: (0, 0, 0))],
        out_specs=pl.BlockSpec((1, ATT_BLOCK, LANES), lambda bb, p, i: (bb, i, p)),
        out_shape=jax.ShapeDtypeStruct((b, seq, n_pairs * LANES), BF16),
        scratch_shapes=[pltpu.VMEM((seq + pad, 2 * LANES), BF16),
                        pltpu.VMEM((seq + pad, 2 * LANES), BF16),
                        pltpu.VMEM((2, ATT_BLOCK, LANES), F32),
                        pltpu.VMEM((2, ATT_BLOCK, LANES), F32),
                        pltpu.VMEM((2, ATT_BLOCK, LANES), F32)],
        compiler_params=_params("parallel", "parallel", "arbitrary"),
        name="dilated",
    )(h3, h3, h3, bias_tab)


def _proj_c_kernel(x_ref, w_ref, loglb_ref, log1mlb_ref, omlb_ref, q_ref, lf_ref, kk_ref, v_ref, g_ref):
    xb = x_ref[...].astype(BF16)
    d = D_MODEL

    def sec(c):
        return jnp.dot(xb, w_ref[:, c * d:(c + 1) * d], preferred_element_type=F32)

    q_ref[...] = sec(0)
    z = sec(1)
    log_sig = jnp.minimum(z, 0.0) - jnp.log1p(jnp.exp(-jnp.abs(z)))
    a = loglb_ref[...]
    c = log1mlb_ref[...] + log_sig
    lf_ref[...] = jnp.maximum(a, c) + jnp.log1p(jnp.exp(-jnp.abs(a - c)))
    kk_ref[...] = omlb_ref[...] / (1.0 + jnp.exp(z))
    v_ref[...] = sec(2).astype(v_ref.dtype)
    g_ref[...] = sec(3)


def _proj_c(x2d, w_bf16, lb, tm=256):
    t, d = x2d.shape
    n = w_bf16.shape[1]
    lb = lb.astype(F32).reshape(1, d)
    vec_spec = pl.BlockSpec((1, d), lambda i: (0, 0))
    out_spec = pl.BlockSpec((tm, d), lambda i: (i, 0))
    sds = jax.ShapeDtypeStruct((t, d), F32)
    return pl.pallas_call(
        _proj_c_kernel,
        grid=(t // tm,),
        in_specs=[pl.BlockSpec((tm, d), lambda i: (i, 0)),
                  pl.BlockSpec((d, n), lambda i: (0, 0)),
                  vec_spec, vec_spec, vec_spec],
        out_specs=[out_spec] * 5,
        out_shape=[sds, sds, sds, jax.ShapeDtypeStruct((t, d), BF16), sds],
        compiler_params=_params("parallel"),
        name="proj_c",
    )(x2d, w_bf16, jnp.log(lb), jnp.log1p(-lb), 1.0 - lb)


def _split3_bf16(x):
    h1 = x.astype(BF16)
    r1 = x - h1.astype(F32)
    h2 = r1.astype(BF16)
    h3 = (r1 - h2.astype(F32)).astype(BF16)
    return h1, h2, h3


def _hgrn_kernel(q_ref, lf_ref, kk_ref, v_ref, g_ref, ng_ref, o_ref, *, seq):
    C, SUB = HGRN_CHUNK, HGRN_SUB
    n_sub = C // SUB
    ri = lax.broadcasted_iota(jnp.int32, (C, C), 0)
    ci = lax.broadcasted_iota(jnp.int32, (C, C), 1)
    tri = jnp.where(ci <= ri, 1.0, 0.0).astype(BF16)
    row_c = lax.broadcasted_iota(jnp.int32, (C, LANES), 0)
    row_s = lax.broadcasted_iota(jnp.int32, (SUB, C), 0)
    lane_s = lax.broadcasted_iota(jnp.int32, (SUB, C), 1)
    ng = ng_ref[...]

    def scores_factored(qc, kc, b):
        refs = [jnp.zeros((1, LANES), F32)] + [b[s * SUB - 1:s * SUB, :] for s in range(1, n_sub)]
        ref_rows = jnp.concatenate([jnp.broadcast_to(r, (SUB, LANES)) for r in refs], axis=0)
        qt = (qc * jnp.exp(b - ref_rows)).astype(BF16)
        rows_a = []
        for s in range(n_sub):
            hi = (s + 1) * SUB
            kt = (kc[:hi, :] * jnp.exp(refs[s] - b[:hi, :])).astype(BF16)
            if hi < C:
                kt = jnp.concatenate([kt, jnp.zeros((C - hi, LANES), BF16)], axis=0)
            rows_a.append(lax.dot_general(qt[s * SUB:hi, :], kt, _NT, preferred_element_type=F32))
        return jnp.where(ci <= ri, jnp.concatenate(rows_a, axis=0), 0.0)

    def scores_direct(qc, kc, b):
        rows_a = []
        for sidx in range(n_sub):
            lo = sidx * SUB
            q_i = qc[lo:lo + SUB, :]
            b_i = b[lo:lo + SUB, :]
            if sidx == 0:
                a_blk = jnp.zeros((SUB, C), F32)
            else:
                ref = b[lo - 1:lo, :]
                qt = (q_i * jnp.exp(b_i - ref)).astype(BF16)
                kt = jnp.where(row_c < lo, kc * jnp.exp(jnp.minimum(ref - b, 0.0)), 0.0).astype(BF16)
                a_blk = lax.dot_general(qt, kt, _NT, preferred_element_type=F32)
            for j in range(SUB):
                r_lo = 0 if j < 8 else 8
                bj = b[lo + j:lo + j + 1, :]
                kj = kc[lo + j:lo + j + 1, :]
                pj = q_i[r_lo:, :] * (jnp.exp(jnp.minimum(b_i[r_lo:, :] - bj, 0.0)) * kj)
                col = jnp.sum(pj, axis=1, keepdims=True)
                if r_lo:
                    col = jnp.concatenate([jnp.zeros((r_lo, 1), F32), col], axis=0)
                a_blk = jnp.where((lane_s == lo + j) & (row_s >= j), col, a_blk)
            rows_a.append(a_blk)
        return jnp.concatenate(rows_a, axis=0)

    def make_chunk(scores):
        def chunk(c, st):
            r0 = pl.multiple_of(c * C, C)
            qc = q_ref[0, pl.ds(r0, C), :]
            kc = kk_ref[0, pl.ds(r0, C), :]
            vb = v_ref[0, pl.ds(r0, C), :].astype(BF16)
            l1, l2, l3 = _split3_bf16(lf_ref[0, pl.ds(r0, C), :])
            b = (jnp.dot(tri, l1, preferred_element_type=F32)
                 + jnp.dot(tri, l2, preferred_element_type=F32)
                 + jnp.dot(tri, l3, preferred_element_type=F32))
            b_last = b[C - 1:C, :]
            inter = lax.dot_general((qc * jnp.exp(b)).astype(BF16), st.astype(BF16), _NT,
                                    preferred_element_type=F32)
            o = inter + jnp.dot(scores(qc, kc, b).astype(BF16), vb, preferred_element_type=F32)
            o = o * lax.rsqrt(jnp.mean(o * o, axis=1, keepdims=True) + RMS_EPS) * ng
            gc = g_ref[0, pl.ds(r0, C), :]
            o_ref[0, pl.ds(r0, C), :] = (o * (gc / (1.0 + jnp.exp(-gc)))).astype(o_ref.dtype)
            kd = (kc * jnp.exp(b_last - b)).astype(BF16)
            return st * jnp.exp(b_last) + lax.dot_general(vb, kd, _TN, preferred_element_type=F32)
        return chunk

    sub_decay = jnp.sum(lf_ref[0].reshape(seq // SUB, SUB, LANES), axis=1)
    fast_ok = jnp.min(sub_decay) >= -HGRN_FAST_LIMIT
    st0 = jnp.zeros((LANES, LANES), F32)

    @pl.when(fast_ok)
    def _():
        lax.fori_loop(0, seq // C, make_chunk(scores_factored), st0, unroll=HGRN_UNROLL)

    @pl.when(jnp.logical_not(fast_ok))
    def _():
        lax.fori_loop(0, seq // C, make_chunk(scores_direct), st0)


def _hgrn(q, lf, kk, v, g, norm_g, batch, seq):
    shp = (batch, seq, D_MODEL)
    args = [a.reshape(shp) for a in (q, lf, kk, v, g)]
    spec = pl.BlockSpec((1, seq, LANES), lambda bb, h: (bb, 0, h))
    return pl.pallas_call(
        functools.partial(_hgrn_kernel, seq=seq),
        grid=(batch, N_HEADS_C),
        in_specs=[spec] * 5 + [pl.BlockSpec((1, LANES), lambda bb, h: (0, 0))],
        out_specs=spec,
        out_shape=jax.ShapeDtypeStruct(shp, BF16),
        compiler_params=_params("parallel", "parallel"),
        name="hgrn",
    )(*args, norm_g.astype(F32).reshape(1, LANES))


def _layer_norm_rows(z, g, b):
    mu = jnp.mean(z, axis=1, keepdims=True)
    zc = z - mu
    var = jnp.mean(zc * zc, axis=1, keepdims=True)
    return zc * lax.rsqrt(var + LN_EPS) * g + b


def _first_lane_of_max(vals, lane_f):
    mx = jnp.max(vals, axis=1, keepdims=True)
    return mx, jnp.min(jnp.where(vals == mx, lane_f, float(LANES)), axis=1, keepdims=True)


def _outproj_ln_kernel(*refs, n_parts):
    o_refs, w_refs = refs[:n_parts], refs[n_parts:2 * n_parts]
    (x_ref, g_ref, b_ref, wr_hi_ref, wr_lo_ref, rb_ref,
     y_ref, yb_ref, fields_ref, info_ref, cnt_ref) = refs[2 * n_parts:]
    mix = jnp.dot(o_refs[0][...], w_refs[0][...], preferred_element_type=F32)
    for o_ref, w_ref in zip(o_refs[1:], w_refs[1:]):
        mix += jnp.dot(o_ref[...], w_ref[...], preferred_element_type=F32)
    y = _layer_norm_rows(DEEPNORM_ALPHA * x_ref[...] + mix, g_ref[...], b_ref[...])
    y_ref[...] = y
    y_hi = y.astype(BF16)
    yb_ref[...] = y_hi
    y_lo = (y - y_hi.astype(F32)).astype(BF16)
    lg = (jnp.dot(y_hi, wr_hi_ref[...], preferred_element_type=F32)
          + jnp.dot(y_lo, wr_hi_ref[...], preferred_element_type=F32)
          + jnp.dot(y_hi, wr_lo_ref[...], preferred_element_type=F32)
          + rb_ref[...])
    lane = lax.broadcasted_iota(jnp.int32, lg.shape, 1)
    lane_f = lane.astype(F32)
    is_g = lane < N_GROUPS
    mg, grp = _first_lane_of_max(jnp.where(is_g, lg, -jnp.inf), lane_f)
    pg = 1.0 / jnp.sum(jnp.where(is_g, jnp.exp(lg - mg), 0.0), axis=1, keepdims=True)
    lo = float(N_GROUPS) + float(EXPERTS_PER_GROUP) * grp
    le = jnp.where((lane_f >= lo) & (lane_f < lo + float(EXPERTS_PER_GROUP)), lg, -jnp.inf)
    m1, i1 = _first_lane_of_max(le, lane_f)
    m2, i2 = _first_lane_of_max(jnp.where(lane_f == i1, -jnp.inf, le), lane_f)
    e21 = jnp.exp(m2 - m1)
    w0 = pg / (1.0 + e21)
    w1 = pg * e21 / (1.0 + e21)
    eid0 = i1 - float(N_GROUPS)
    eid1 = i2 - float(N_GROUPS)

    @pl.when(pl.program_id(0) == 0)
    def _():
        cnt_ref[...] = jnp.zeros(cnt_ref.shape, F32)

    hits = jnp.where(lane_f == eid0, 1.0, 0.0) + jnp.where(lane_f == eid1, 1.0, 0.0)
    cnt_ref[...] += jnp.sum(hits, axis=0, keepdims=True)
    fields = jnp.where(lane == 0, eid0, jnp.where(lane == 1, eid1, jnp.where(lane == 2, w0,
                       jnp.where(lane == 3, w1, 0.0))))
    fields_ref[...] = fields
    info_ref[...] = fields.T[:8, :]


def _outproj_ln(o_parts, x2d, w_bf16, ln_g, ln_b, wr, rb, tm=512):
    t, d = x2d.shape
    wr_hi = wr.astype(BF16)
    wr_lo = (wr - wr_hi.astype(F32)).astype(BF16)
    vec = lambda n: pl.BlockSpec((1, n), lambda i: (0, 0))
    widths = [o.shape[1] for o in o_parts]
    offs = np.cumsum([0] + widths)
    w_parts = [w_bf16[offs[k]:offs[k + 1]] for k in range(len(widths))]
    return pl.pallas_call(
        functools.partial(_outproj_ln_kernel, n_parts=len(widths)),
        grid=(t // tm,),
        in_specs=[pl.BlockSpec((tm, wd), lambda i: (i, 0)) for wd in widths]
                 + [pl.BlockSpec((wd, d), lambda i: (0, 0)) for wd in widths]
                 + [pl.BlockSpec((tm, d), lambda i: (i, 0)),
                  vec(d), vec(d),
                  pl.BlockSpec((d, LANES), lambda i: (0, 0)),
                  pl.BlockSpec((d, LANES), lambda i: (0, 0)),
                  vec(LANES)],
        out_specs=[pl.BlockSpec((tm, d), lambda i: (i, 0)),
                   pl.BlockSpec((tm, d), lambda i: (i, 0)),
                   pl.BlockSpec((tm, LANES), lambda i: (i, 0)),
                   pl.BlockSpec((8, tm), lambda i: (0, i)),
                   vec(LANES)],
        out_shape=[jax.ShapeDtypeStruct((t, d), F32),
                   jax.ShapeDtypeStruct((t, d), BF16),
                   jax.ShapeDtypeStruct((t, LANES), F32),
                   jax.ShapeDtypeStruct((8, t), F32),
                   jax.ShapeDtypeStruct((1, LANES), F32)],
        compiler_params=_params("arbitrary"),
        name="outproj_ln",
    )(*o_parts, *w_parts, x2d, ln_g.reshape(1, d), ln_b.reshape(1, d), wr_hi, wr_lo, rb)


def _expert_kernel(te_ref, nv_ref, xs_ref, w1_ref, w3_ref, w2_ref, o_ref, w1_s, w3_s, w2_s):
    i = pl.program_id(0)
    prev = te_ref[jnp.maximum(i - 1, 0)]

    @pl.when((i == 0) | (te_ref[i] != prev))
    def _cast_weights():
        w1_s[...] = w1_ref[0].astype(BF16)
        w3_s[...] = w3_ref[0].astype(BF16)
        w2_s[...] = w2_ref[0].astype(BF16)

    @pl.when(i < nv_ref[0])
    def _ffn():
        xb = xs_ref[...]
        a = jnp.dot(xb, w1_s[...], preferred_element_type=F32)
        u = jnp.dot(xb, w3_s[...], preferred_element_type=F32)
        hb = (a / (1.0 + jnp.exp(-a))) * u
        o_ref[...] = jnp.dot(hb.astype(BF16), w2_s[...], preferred_element_type=F32).astype(o_ref.dtype)

    @pl.when(i >= nv_ref[0])
    def _pad():
        o_ref[...] = jnp.zeros(o_ref.shape, o_ref.dtype)


def _expert_ffn(tile_expert, n_valid, xs, w1, w3, w2):
    p, d = xs.shape
    f = w1.shape[2]
    n_tiles = p // MOE_TILE
    grid_spec = pltpu.PrefetchScalarGridSpec(
        num_scalar_prefetch=2,
        grid=(n_tiles,),
        in_specs=[pl.BlockSpec((MOE_TILE, d), lambda i, te, nv: (i, 0)),
                  pl.BlockSpec((1, d, f), lambda i, te, nv: (te[i], 0, 0)),
                  pl.BlockSpec((1, d, f), lambda i, te, nv: (te[i], 0, 0)),
                  pl.BlockSpec((1, f, d), lambda i, te, nv: (te[i], 0, 0))],
        out_specs=pl.BlockSpec((MOE_TILE, d), lambda i, te, nv: (i, 0)),
        scratch_shapes=[pltpu.VMEM((d, f), BF16), pltpu.VMEM((d, f), BF16), pltpu.VMEM((f, d), BF16)],
    )
    return pl.pallas_call(
        _expert_kernel,
        grid_spec=grid_spec,
        out_shape=jax.ShapeDtypeStruct((p, d), BF16),
        compiler_params=_params("arbitrary"),
        name="expert_ffn",
    )(tile_expert, n_valid, xs, w1, w3, w2)


def _combine_ln_kernel(x_ref, y0_ref, y1_ref, fields_ref, g_ref, b_ref, o_ref):
    fields = fields_ref[...]
    ffn = fields[:, 2:3] * y0_ref[...].astype(F32) + fields[:, 3:4] * y1_ref[...].astype(F32)
    o_ref[...] = _layer_norm_rows(DEEPNORM_ALPHA * x_ref[...] + ffn, g_ref[...], b_ref[...])


def _combine_ln(x2d, y0, y1, fields, ln_g, ln_b, tm=512):
    t, d = x2d.shape
    row = pl.BlockSpec((tm, d), lambda i: (i, 0))
    vec = pl.BlockSpec((1, d), lambda i: (0, 0))
    return pl.pallas_call(
        _combine_ln_kernel,
        grid=(t // tm,),
        in_specs=[row, row, row, pl.BlockSpec((tm, LANES), lambda i: (i, 0)), vec, vec],
        out_specs=row,
        out_shape=jax.ShapeDtypeStruct((t, d), F32),
        compiler_params=_params("parallel"),
        name="combine_ln",
    )(x2d, y0, y1, fields, ln_g.reshape(1, d), ln_b.reshape(1, d))


def _dispatch_plan(info, counts_f):
    t = info.shape[1]
    a = t * TOPK_IN_GROUP
    n_fill = N_EXPERTS * MOE_TILE
    n_tiles = a // MOE_TILE + N_EXPERTS
    experts = jnp.arange(N_EXPERTS, dtype=jnp.int32)
    counts = counts_f[0, :N_EXPERTS].astype(jnp.int32)
    padded = (counts + MOE_TILE - 1) // MOE_TILE * MOE_TILE
    pends = jnp.cumsum(padded)
    fill_ends = jnp.cumsum(padded - counts)
    fill = jnp.arange(n_fill, dtype=jnp.int32)
    fill_key = jnp.sum((fill_ends[None, :] <= fill[:, None]).astype(jnp.int32), axis=1)
    tok = jnp.arange(t, dtype=jnp.int32)
    keys = jnp.concatenate([info[0].astype(jnp.int32), info[1].astype(jnp.int32), fill_key])
    toks = jnp.concatenate([tok, tok, fill % t])
    flat = jnp.arange(a + n_fill, dtype=jnp.int32)
    _, slot_tok, slot_flat = lax.sort((keys, toks, flat), num_keys=1, is_stable=True)
    _, slot_of = lax.sort((slot_flat, flat), num_keys=1)
    tile_start = jnp.arange(n_tiles, dtype=jnp.int32) * MOE_TILE
    tile_expert = jnp.sum((pends[None, :] <= tile_start[:, None]).astype(jnp.int32), axis=1)
    last_used = jnp.max(jnp.where(counts > 0, experts, 0))
    tile_expert = jnp.minimum(tile_expert, last_used)
    n_valid = (pends[-1] // MOE_TILE).astype(jnp.int32).reshape(1)
    return tile_expert, n_valid, slot_tok, slot_of[:t], slot_of[t:a]


def _moe_and_norm(x1, x1_bf16, fields, info, counts_f, w1, w3, w2, ln_g, ln_b):
    tile_expert, n_valid, slot_tok, dest0, dest1 = _dispatch_plan(info, counts_f)
    yb = _expert_ffn(tile_expert, n_valid, x1_bf16[slot_tok], w1, w3, w2)
    return _combine_ln(x1, yb[dest0], yb[dest1], fields, ln_g, ln_b)


def kernel(x, ab_w_in, ab_w_out, c_w_in, c_w_out, c_norm_g, hgrn_lb_logits, ln_g, ln_b,
           router_g_w, router_g_b, router_e_w, router_e_b, exp_w1, exp_w3, exp_w2):
    batch, seq, d = x.shape
    t = batch * seq
    tables = _rope_lane_tables(seq)
    bias_tab = jnp.asarray(_dilated_bias_table())
    lb_all = jnp.cumsum(jax.nn.softmax(hgrn_lb_logits.astype(F32), axis=0), axis=0)
    lb_all = lb_all - lb_all[0:1]
    n_pairs = (AB_WIDTH // 2) // LANES

    xc = x.reshape(t, d)
    for l in range(DEPTH):
        j = l // 2
        if l % 2 == 0:
            h = _proj_ab(xc, ab_w_in[j].astype(BF16), tables, seq).reshape(batch, seq, 3 * AB_WIDTH)
            o_parts = [_moba(h, n_pairs).reshape(t, AB_WIDTH // 2),
                       _dilated(h, bias_tab, n_pairs).reshape(t, AB_WIDTH // 2)]
            w_out = ab_w_out[j]
        else:
            q, lf, kk, v, g = _proj_c(xc, c_w_in[j].astype(BF16), lb_all[j])
            o_parts = [_hgrn(q, lf, kk, v, g, c_norm_g[j], batch, seq).reshape(t, d)]
            w_out = c_w_out[j]
        wr = jnp.zeros((d, LANES), F32)
        wr = wr.at[:, :N_GROUPS].set(router_g_w[l]).at[:, N_GROUPS:N_GROUPS + N_EXPERTS].set(router_e_w[l])
        rb = jnp.zeros((1, LANES), F32)
        rb = rb.at[0, :N_GROUPS].set(router_g_b[l]).at[0, N_GROUPS:N_GROUPS + N_EXPERTS].set(router_e_b[l])
        x1, x1_bf16, fields, info, counts_f = _outproj_ln(o_parts, xc, w_out.astype(BF16), ln_g[l, 0], ln_b[l, 0],
                                                          wr, rb)
        xc = _moe_and_norm(x1, x1_bf16, fields, info, counts_f, exp_w1[l], exp_w3[l], exp_w2[l],
                           ln_g[l, 1], ln_b[l, 1])
    return xc.reshape(batch, seq, d)
```

```python
import functools
import math

import numpy as np
import jax
import jax.numpy as jnp
from jax import lax
from jax.experimental import pallas as pl
from jax.experimental.pallas import tpu as pltpu

F32 = jnp.float32
BF16 = jnp.bfloat16

D_MODEL = 1024
DEPTH = 4
HEAD_DIM = 64
N_HEADS_A = 8
N_HEADS_B = 8
AB_WIDTH = (N_HEADS_A + N_HEADS_B) * HEAD_DIM
ROT_DIM = HEAD_DIM // 4
ROPE_THETA = 500000.0
MOBA_BLOCK = 256
MOBA_TOPK = 3
DILATED_BRANCHES = ((128, 1), (512, 4), (2048, 16))
HGRN_EXPAND = 128
N_HEADS_C = D_MODEL // HGRN_EXPAND
HGRN_CHUNK = 64
HGRN_SUB = 16
HGRN_UNROLL = 8
HGRN_FAST_LIMIT = 60.0
N_GROUPS = 4
EXPERTS_PER_GROUP = 8
N_EXPERTS = N_GROUPS * EXPERTS_PER_GROUP
TOPK_IN_GROUP = 2
D_EXPERT = D_MODEL // 2
DEEPNORM_ALPHA = (2.0 * DEPTH) ** 0.25
LN_EPS = 1e-5
RMS_EPS = 1e-6

LANES = 128
ATT_BLOCK = 256
MOBA_STEP = 4
MOBA_SEL_ROWS = 16
SHIFT_LANE = LANES - 2
ATT_FAST_LIMIT = 38.0
DIL_STEP = 3
NEG_BIG = -1e30
MOE_TILE = 512
MOE_SLAB = 256
VMEM_LIMIT = 56 * 1024 * 1024

_NT = (((1,), (1,)), ((), ()))
_TN = (((0,), (0,)), ((), ()))


def _params(*sem):
    return pltpu.CompilerParams(dimension_semantics=sem, vmem_limit_bytes=VMEM_LIMIT)


def _proj_ab_kernel(x_ref, w_ref, cos_ref, sa_ref, sb_ref, o_ref):
    xb = x_ref[...].astype(BF16)
    sec = AB_WIDTH // 2
    for c in range(6):
        acc = jnp.dot(xb, w_ref[:, c * sec:(c + 1) * sec], preferred_element_type=F32)
        if c % 3 == 2:
            o_ref[:, c * sec:(c + 1) * sec] = acc.astype(BF16)
            continue
        cosv = cos_ref[...]
        sa = sa_ref[...]
        sb = sb_ref[...]
        for k in range(sec // LANES):
            a = acc[:, k * LANES:(k + 1) * LANES]
            r = a * cosv + pltpu.roll(a, LANES - ROT_DIM // 2, 1) * sa + pltpu.roll(a, ROT_DIM // 2, 1) * sb
            if c % 3 == 0:
                r = r * (HEAD_DIM ** -0.5)
            o_ref[:, c * sec + k * LANES:c * sec + (k + 1) * LANES] = r.astype(BF16)


def _rope_lane_tables(seq):
    half = ROT_DIM // 2
    inv = ROPE_THETA ** (-jnp.arange(half, dtype=F32) / half)
    ang = jnp.arange(seq, dtype=F32)[:, None] * inv[None, :]
    cos, sin = jnp.cos(ang), jnp.sin(ang)
    ones = jnp.ones((seq, HEAD_DIM - ROT_DIM), F32)
    zeros = jnp.zeros((seq, HEAD_DIM - ROT_DIM), F32)
    zh = jnp.zeros((seq, half), F32)
    cos_t = jnp.concatenate([cos, cos, ones], axis=1)
    sa_t = jnp.concatenate([-sin, zh, zeros], axis=1)
    sb_t = jnp.concatenate([zh, sin, zeros], axis=1)
    rep = LANES // HEAD_DIM
    return tuple(jnp.tile(t, (1, rep)) for t in (cos_t, sa_t, sb_t))


def _proj_ab(x2d, w_bf16, tables, seq, tm=512):
    t, d = x2d.shape
    n = w_bf16.shape[1]
    nseq = seq // tm
    tab_spec = pl.BlockSpec((tm, LANES), lambda i: (i % nseq, 0))
    return pl.pallas_call(
        _proj_ab_kernel,
        grid=(t // tm,),
        in_specs=[pl.BlockSpec((tm, d), lambda i: (i, 0)),
                  pl.BlockSpec((d, n), lambda i: (0, 0)),
                  tab_spec, tab_spec, tab_spec],
        out_specs=pl.BlockSpec((tm, n), lambda i: (i, 0)),
        out_shape=jax.ShapeDtypeStruct((t, n), BF16),
        compiler_params=_params("parallel"),
        name="proj_ab",
    )(x2d, w_bf16, *tables)


def _softmax_step(parts, vaug, h, m_scr, l_scr, acc_scr):
    m_prev = m_scr[h]
    mx = functools.reduce(jnp.maximum, parts)
    m_new = jnp.maximum(m_prev, jnp.max(mx, axis=1, keepdims=True))
    alpha = jnp.exp(m_prev - m_new)
    p = jnp.concatenate([jnp.exp(part - m_new).astype(BF16) for part in parts], axis=1)
    pv = jnp.dot(p, vaug, preferred_element_type=F32)
    acc_scr[h] = alpha * acc_scr[h] + pv[:, :LANES]
    l_scr[h] = alpha * l_scr[h] + pv[:, LANES:]
    m_scr[h] = m_new


def _plain_step(parts, vaug, h, l_scr, acc_scr):
    p = jnp.concatenate([jnp.exp(part).astype(BF16) for part in parts], axis=1)
    pv = jnp.dot(p, vaug, preferred_element_type=F32)
    acc_scr[h] += pv[:, :LANES]
    l_scr[h] += pv[:, LANES:]


def _lane_parts(s):
    return [s[:, c * LANES:(c + 1) * LANES] for c in range(s.shape[1] // LANES)]


def _head_mask(lane, h):
    return (lane >= h * HEAD_DIM) & (lane < (h + 1) * HEAD_DIM)


def _head_sq_norms(x, h):
    lane = lax.broadcasted_iota(jnp.int32, x.shape, 1)
    xf = x.astype(F32)
    sq = jnp.where(_head_mask(lane, h), xf * xf, 0.0).astype(BF16)
    return jnp.dot(sq, jnp.ones((LANES, LANES), BF16), preferred_element_type=F32)


def _score_bounds(q, kn_scr):
    bounds = [jnp.sqrt(_head_sq_norms(q, h) * kn_scr[h:h + 1, :]) for h in range(2)]
    fast_ok = jnp.maximum(jnp.max(bounds[0]), jnp.max(bounds[1])) <= ATT_FAST_LIMIT
    return [jnp.where(fast_ok, -b, 0.0) for b in bounds], fast_ok


def _softmax_init(m_scr, l_scr, acc_scr):
    m_scr[...] = jnp.full(m_scr.shape, -jnp.inf, F32)
    l_scr[...] = jnp.zeros(l_scr.shape, F32)
    acc_scr[...] = jnp.zeros(acc_scr.shape, F32)


def _softmax_finish(o_ref, l_scr, acc_scr):
    lane = lax.broadcasted_iota(jnp.int32, acc_scr.shape[1:], 1)
    o0 = acc_scr[0] / l_scr[0]
    o1 = acc_scr[1] / l_scr[1]
    o_ref[0] = jnp.where(lane < HEAD_DIM, o0, o1).astype(o_ref.dtype)


def _moba_kernel(q_ref, k_ref, v_ref, o_ref, kmean_scr, kn_scr, kaug_scr, vaug_scr, qaug_scr, m_scr, l_scr, acc_scr,
                 *, seq):
    bq = ATT_BLOCK
    pad = (MOBA_STEP - 1) * bq
    i = pl.program_id(2)

    @pl.when(i == 0)
    def _per_sequence_setup():
        rows = lax.broadcasted_iota(jnp.int32, (LANES, seq), 0)
        cols = lax.broadcasted_iota(jnp.int32, (LANES, seq), 1)
        lo = rows * MOBA_BLOCK
        ind = jnp.where((cols >= lo) & (cols < lo + MOBA_BLOCK), 1.0, 0.0).astype(BF16)
        kmean_scr[...] = jnp.dot(ind, k_ref[0], preferred_element_type=F32) * (1.0 / MOBA_BLOCK)
        for h in range(2):
            kn_scr[h:h + 1, :] = jnp.max(_head_sq_norms(k_ref[0], h), axis=0, keepdims=True)
        blk_lane = lax.broadcasted_iota(jnp.int32, (bq, LANES), 1)
        kaug_scr[0:pad, 0:LANES] = jnp.zeros((pad, LANES), BF16)
        vaug_scr[0:pad, 0:LANES] = jnp.zeros((pad, LANES), BF16)
        for jb in range(-(MOBA_STEP - 1), seq // bq):
            tgt = jb if jb >= 0 else LANES - 1
            kaug_scr[pad + jb * bq:pad + (jb + 1) * bq, LANES:] = jnp.where(
                (blk_lane == tgt) | (blk_lane == SHIFT_LANE), 1.0, 0.0).astype(BF16)
        kaug_scr[pad:, 0:LANES] = k_ref[0]
        vaug_scr[pad:, 0:LANES] = v_ref[0]
        vaug_scr[:, LANES:] = jnp.ones((seq + pad, LANES), BF16)

    q = q_ref[0]
    lane = lax.broadcasted_iota(jnp.int32, (bq, LANES), 1)
    shifts, fast_ok = _score_bounds(q, kn_scr)
    km = kmean_scr[0:MOBA_SEL_ROWS, :]
    km_hi = km.astype(BF16)
    km_lo = (km - km_hi.astype(F32)).astype(BF16)
    blk = lax.broadcasted_iota(jnp.int32, (MOBA_SEL_ROWS, bq), 0)
    blk_f = blk.astype(F32)
    for h in range(2):
        qh = jnp.where(_head_mask(lane, h), q, jnp.zeros_like(q))
        g = (lax.dot_general(km_hi, qh, _NT, preferred_element_type=F32)
             + lax.dot_general(km_lo, qh, _NT, preferred_element_type=F32))
        g = jnp.where(blk < i, g, -jnp.inf)
        bias = jnp.where(blk == i, 0.0, NEG_BIG)
        for _ in range(MOBA_TOPK):
            mx = jnp.max(g, axis=0, keepdims=True)
            first = jnp.min(jnp.where(g == mx, blk_f, float(MOBA_SEL_ROWS)), axis=0, keepdims=True)
            pick = (blk_f == first) & (mx > -jnp.inf)
            bias = jnp.where(pick, 0.0, bias)
            g = jnp.where(pick, -jnp.inf, g)
        bias = jnp.concatenate([bias, jnp.full((LANES - MOBA_SEL_ROWS, bq), NEG_BIG, F32)], axis=0).T
        qaug_scr[h] = jnp.concatenate([qh, jnp.where(lane == SHIFT_LANE, shifts[h], bias).astype(BF16)], axis=1)

    _softmax_init(m_scr, l_scr, acc_scr)
    n_parts = MOBA_STEP * bq // LANES
    own_parts = bq // LANES
    qi = lax.broadcasted_iota(jnp.int32, (bq, LANES), 0)

    def step(t, first, fast):
        start = pl.multiple_of((i - MOBA_STEP * t) * bq, bq)
        kaug = kaug_scr[pl.ds(start, MOBA_STEP * bq), :]
        vaug = vaug_scr[pl.ds(start, MOBA_STEP * bq), :]
        for h in range(2):
            parts = _lane_parts(lax.dot_general(qaug_scr[h], kaug, _NT, preferred_element_type=F32))
            if first:
                for c in range(own_parts):
                    cc = n_parts - own_parts + c
                    parts[cc] = jnp.where(lane + c * LANES <= qi, parts[cc], NEG_BIG)
            if fast:
                _plain_step(parts, vaug, h, l_scr, acc_scr)
            else:
                _softmax_step(parts, vaug, h, m_scr, l_scr, acc_scr)

    def run(fast):
        step(0, True, fast)

        def later(t, carry):
            step(t, False, fast)
            return carry

        lax.fori_loop(1, (i + MOBA_STEP) // MOBA_STEP, later, 0)

    pl.when(fast_ok)(functools.partial(run, True))
    pl.when(jnp.logical_not(fast_ok))(functools.partial(run, False))
    _softmax_finish(o_ref, l_scr, acc_scr)


def _moba(h3, n_pairs):
    b, seq, _ = h3.shape
    sec_blocks = (AB_WIDTH // 2) // LANES
    nq = seq // ATT_BLOCK
    assert seq % ATT_BLOCK == 0 and nq <= MOBA_SEL_ROWS
    return pl.pallas_call(
        functools.partial(_moba_kernel, seq=seq),
        grid=(b, n_pairs, nq),
        in_specs=[pl.BlockSpec((1, ATT_BLOCK, LANES), lambda bb, p, i: (bb, i, p)),
                  pl.BlockSpec((1, seq, LANES), lambda bb, p, i: (bb, 0, sec_blocks + p)),
                  pl.BlockSpec((1, seq, LANES), lambda bb, p, i: (bb, 0, 2 * sec_blocks + p))],
        out_specs=pl.BlockSpec((1, ATT_BLOCK, LANES), lambda bb, p, i: (bb, i, p)),
        out_shape=jax.ShapeDtypeStruct((b, seq, n_pairs * LANES), BF16),
        scratch_shapes=[pltpu.VMEM((LANES, LANES), F32),
                        pltpu.VMEM((8, LANES), F32),
                        pltpu.VMEM((seq + (MOBA_STEP - 1) * ATT_BLOCK, 2 * LANES), BF16),
                        pltpu.VMEM((seq + (MOBA_STEP - 1) * ATT_BLOCK, 2 * LANES), BF16),
                        pltpu.VMEM((2, ATT_BLOCK, 2 * LANES), BF16),
                        pltpu.VMEM((2, ATT_BLOCK, LANES), F32),
                        pltpu.VMEM((2, ATT_BLOCK, LANES), F32),
                        pltpu.VMEM((2, ATT_BLOCK, LANES), F32)],
        compiler_params=_params("parallel", "parallel", "arbitrary"),
        name="moba",
    )(h3, h3, h3)


def _dilated_n_blocks():
    return max(w for w, _ in DILATED_BRANCHES) // ATT_BLOCK + 1


def _dilated_bias_table():
    n_steps = -(-_dilated_n_blocks() // DIL_STEP)
    qi = np.arange(ATT_BLOCK)[:, None]
    ki = np.arange(ATT_BLOCK)[None, :]
    tabs = []
    for t in range(n_steps):
        groups = []
        for g in range(DIL_STEP):
            d = qi - ki + (DIL_STEP * t + DIL_STEP - 1 - g) * ATT_BLOCK
            cnt = np.zeros(d.shape, np.int64)
            for window, dil in DILATED_BRANCHES:
                cnt += ((d >= 0) & (d <= window) & (d % dil == 0)).astype(np.int64)
            groups.append(np.where(cnt > 0, np.log(np.maximum(cnt, 1).astype(np.float64)), NEG_BIG))
        tabs.append(np.concatenate(groups, axis=1))
    return np.stack(tabs).astype(np.float32)


def _dilated_kernel(q_ref, k_ref, v_ref, bias_ref, o_ref, kn_scr, kaug_scr, vaug_scr, qaug_scr, m_scr, l_scr, acc_scr,
                    *, seq):
    bq = ATT_BLOCK
    pad = (DIL_STEP - 1) * bq
    i = pl.program_id(2)

    @pl.when(i == 0)
    def _per_sequence_setup():
        for h in range(2):
            kn_scr[h:h + 1, :] = jnp.max(_head_sq_norms(k_ref[0], h), axis=0, keepdims=True)
        flag_lane = lax.broadcasted_iota(jnp.int32, (pad, LANES), 1)
        all_lane = lax.broadcasted_iota(jnp.int32, (seq, LANES), 1)
        kaug_scr[0:pad, 0:LANES] = jnp.zeros((pad, LANES), BF16)
        kaug_scr[0:pad, LANES:] = jnp.where((flag_lane == 0) | (flag_lane == SHIFT_LANE), 1.0, 0.0).astype(BF16)
        kaug_scr[pad:, 0:LANES] = k_ref[0]
        kaug_scr[pad:, LANES:] = jnp.where(all_lane == SHIFT_LANE, 1.0, 0.0).astype(BF16)
        vaug_scr[0:pad, 0:LANES] = jnp.zeros((pad, LANES), BF16)
        vaug_scr[pad:, 0:LANES] = v_ref[0]
        vaug_scr[:, LANES:] = jnp.ones((seq + pad, LANES), BF16)

    q = q_ref[0]
    lane = lax.broadcasted_iota(jnp.int32, (bq, LANES), 1)
    shifts, fast_ok = _score_bounds(q, kn_scr)
    for h in range(2):
        aug = jnp.where(lane == SHIFT_LANE, shifts[h], jnp.where(lane == 0, NEG_BIG, 0.0))
        qaug_scr[h] = jnp.concatenate([jnp.where(_head_mask(lane, h), q, jnp.zeros_like(q)), aug.astype(BF16)],
                                      axis=1)
    _softmax_init(m_scr, l_scr, acc_scr)

    def step(t, fast):
        start = pl.multiple_of((i - DIL_STEP * t) * bq, bq)
        kaug = kaug_scr[pl.ds(start, DIL_STEP * bq), :]
        vaug = vaug_scr[pl.ds(start, DIL_STEP * bq), :]
        bias = bias_ref[t]
        for h in range(2):
            parts = _lane_parts(lax.dot_general(qaug_scr[h], kaug, _NT, preferred_element_type=F32) + bias)
            if fast:
                _plain_step(parts, vaug, h, l_scr, acc_scr)
            else:
                _softmax_step(parts, vaug, h, m_scr, l_scr, acc_scr)

    n_steps = (jnp.minimum(i, _dilated_n_blocks() - 1) + DIL_STEP) // DIL_STEP

    def run(fast):
        def body(t, carry):
            step(t, fast)
            return carry

        lax.fori_loop(0, n_steps, body, 0)

    pl.when(fast_ok)(functools.partial(run, True))
    pl.when(jnp.logical_not(fast_ok))(functools.partial(run, False))
    _softmax_finish(o_ref, l_scr, acc_scr)


def _dilated(h3, bias_tab, n_pairs):
    b, seq, _ = h3.shape
    sec_blocks = (AB_WIDTH // 2) // LANES
    base = 3 * sec_blocks
    nq = seq // ATT_BLOCK
    pad = (DIL_STEP - 1) * ATT_BLOCK
    return pl.pallas_call(
        functools.partial(_dilated_kernel, seq=seq),
        grid=(b, n_pairs, nq),
        in_specs=[pl.BlockSpec((1, ATT_BLOCK, LANES), lambda bb, p, i: (bb, i, base + p)),
                  pl.BlockSpec((1, seq, LANES), lambda bb, p, i: (bb, 0, base + sec_blocks + p)),
                  pl.BlockSpec((1, seq, LANES), lambda bb, p, i: (bb, 0, base + 2 * sec_blocks + p)),
                  pl.BlockSpec(bias_tab.shape, lambda bb, p, i: (0, 0, 0))],
        out_specs=pl.BlockSpec((1, ATT_BLOCK, LANES), lambda bb, p, i: (bb, i, p)),
        out_shape=jax.ShapeDtypeStruct((b, seq, n_pairs * LANES), BF16),
        scratch_shapes=[pltpu.VMEM((8, LANES), F32),
                        pltpu.VMEM((seq + pad, 2 * LANES), BF16),
                        pltpu.VMEM((seq + pad, 2 * LANES), BF16),
                        pltpu.VMEM((2, ATT_BLOCK, 2 * LANES), BF16),
                        pltpu.VMEM((2, ATT_BLOCK, LANES), F32),
                        pltpu.VMEM((2, ATT_BLOCK, LANES), F32),
                        pltpu.VMEM((2, ATT_BLOCK, LANES), F32)],
        compiler_params=_params("parallel", "parallel", "arbitrary"),
        name="dilated",
    )(h3, h3, h3, bias_tab)


def _proj_c_kernel(x_ref, w_ref, loglb_ref, log1mlb_ref, omlb_ref, q_ref, lf_ref, kk_ref, v_ref, g_ref):
    xb = x_ref[...].astype(BF16)
    d = D_MODEL

    def sec(c):
        return jnp.dot(xb, w_ref[:, c * d:(c + 1) * d], preferred_element_type=F32)

    q_ref[...] = sec(0)
    z = sec(1)
    log_sig = jnp.minimum(z, 0.0) - jnp.log1p(jnp.exp(-jnp.abs(z)))
    a = loglb_ref[...]
    c = log1mlb_ref[...] + log_sig
    lf_ref[...] = jnp.maximum(a, c) + jnp.log1p(jnp.exp(-jnp.abs(a - c)))
    kk_ref[...] = omlb_ref[...] / (1.0 + jnp.exp(z))
    v_ref[...] = sec(2).astype(v_ref.dtype)
    g_ref[...] = sec(3)


def _proj_c(x2d, w_bf16, lb, tm=256):
    t, d = x2d.shape
    n = w_bf16.shape[1]
    lb = lb.astype(F32).reshape(1, d)
    vec_spec = pl.BlockSpec((1, d), lambda i: (0, 0))
    out_spec = pl.BlockSpec((tm, d), lambda i: (i, 0))
    sds = jax.ShapeDtypeStruct((t, d), F32)
    return pl.pallas_call(
        _proj_c_kernel,
        grid=(t // tm,),
        in_specs=[pl.BlockSpec((tm, d), lambda i: (i, 0)),
                  pl.BlockSpec((d, n), lambda i: (0, 0)),
                  vec_spec, vec_spec, vec_spec],
        out_specs=[out_spec] * 5,
        out_shape=[sds, sds, sds, jax.ShapeDtypeStruct((t, d), BF16), sds],
        compiler_params=_params("parallel"),
        name="proj_c",
    )(x2d, w_bf16, jnp.log(lb), jnp.log1p(-lb), 1.0 - lb)


def _split3_bf16(x):
    h1 = x.astype(BF16)
    r1 = x - h1.astype(F32)
    h2 = r1.astype(BF16)
    h3 = (r1 - h2.astype(F32)).astype(BF16)
    return h1, h2, h3


def _hgrn_kernel(q_ref, lf_ref, kk_ref, v_ref, g_ref, ng_ref, o_ref, *, seq):
    C, SUB = HGRN_CHUNK, HGRN_SUB
    n_sub = C // SUB
    ri = lax.broadcasted_iota(jnp.int32, (C, C), 0)
    ci = lax.broadcasted_iota(jnp.int32, (C, C), 1)
    tri = jnp.where(ci <= ri, 1.0, 0.0).astype(BF16)
    row_c = lax.broadcasted_iota(jnp.int32, (C, LANES), 0)
    row_s = lax.broadcasted_iota(jnp.int32, (SUB, C), 0)
    lane_s = lax.broadcasted_iota(jnp.int32, (SUB, C), 1)
    ng = ng_ref[...]

    def scores_factored(qc, kc, b):
        refs = [jnp.zeros((1, LANES), F32)] + [b[s * SUB - 1:s * SUB, :] for s in range(1, n_sub)]
        ref_rows = jnp.concatenate([jnp.broadcast_to(r, (SUB, LANES)) for r in refs], axis=0)
        qt = (qc * jnp.exp(b - ref_rows)).astype(BF16)
        rows_a = []
        for s in range(n_sub):
            hi = (s + 1) * SUB
            kt = (kc[:hi, :] * jnp.exp(refs[s] - b[:hi, :])).astype(BF16)
            if hi < C:
                kt = jnp.concatenate([kt, jnp.zeros((C - hi, LANES), BF16)], axis=0)
            rows_a.append(lax.dot_general(qt[s * SUB:hi, :], kt, _NT, preferred_element_type=F32))
        return jnp.where(ci <= ri, jnp.concatenate(rows_a, axis=0), 0.0)

    def scores_direct(qc, kc, b):
        rows_a = []
        for sidx in range(n_sub):
            lo = sidx * SUB
            q_i = qc[lo:lo + SUB, :]
            b_i = b[lo:lo + SUB, :]
            if sidx == 0:
                a_blk = jnp.zeros((SUB, C), F32)
            else:
                ref = b[lo - 1:lo, :]
                qt = (q_i * jnp.exp(b_i - ref)).astype(BF16)
                kt = jnp.where(row_c < lo, kc * jnp.exp(jnp.minimum(ref - b, 0.0)), 0.0).astype(BF16)
                a_blk = lax.dot_general(qt, kt, _NT, preferred_element_type=F32)
            for j in range(SUB):
                r_lo = 0 if j < 8 else 8
                bj = b[lo + j:lo + j + 1, :]
                kj = kc[lo + j:lo + j + 1, :]
                pj = q_i[r_lo:, :] * (jnp.exp(jnp.minimum(b_i[r_lo:, :] - bj, 0.0)) * kj)
                col = jnp.sum(pj, axis=1, keepdims=True)
                if r_lo:
                    col = jnp.concatenate([jnp.zeros((r_lo, 1), F32), col], axis=0)
                a_blk = jnp.where((lane_s == lo + j) & (row_s >= j), col, a_blk)
            rows_a.append(a_blk)
        return jnp.concatenate(rows_a, axis=0)

    def make_chunk(scores):
        def chunk(c, st):
            r0 = pl.multiple_of(c * C, C)
            qc = q_ref[0, pl.ds(r0, C), :]
            kc = kk_ref[0, pl.ds(r0, C), :]
            vb = v_ref[0, pl.ds(r0, C), :].astype(BF16)
            l1, l2, l3 = _split3_bf16(lf_ref[0, pl.ds(r0, C), :])
            b = (jnp.dot(tri, l1, preferred_element_type=F32)
                 + jnp.dot(tri, l2, preferred_element_type=F32)
                 + jnp.dot(tri, l3, preferred_element_type=F32))
            b_last = b[C - 1:C, :]
            inter = lax.dot_general((qc * jnp.exp(b)).astype(BF16), st.astype(BF16), _NT,
                                    preferred_element_type=F32)
            o = inter + jnp.dot(scores(qc, kc, b).astype(BF16), vb, preferred_element_type=F32)
            o = o * lax.rsqrt(jnp.mean(o * o, axis=1, keepdims=True) + RMS_EPS) * ng
            gc = g_ref[0, pl.ds(r0, C), :]
            o_ref[0, pl.ds(r0, C), :] = (o * (gc / (1.0 + jnp.exp(-gc)))).astype(o_ref.dtype)
            kd = (kc * jnp.exp(b_last - b)).astype(BF16)
            return st * jnp.exp(b_last) + lax.dot_general(vb, kd, _TN, preferred_element_type=F32)
        return chunk

    sub_decay = jnp.sum(lf_ref[0].reshape(seq // SUB, SUB, LANES), axis=1)
    fast_ok = jnp.min(sub_decay) >= -HGRN_FAST_LIMIT
    st0 = jnp.zeros((LANES, LANES), F32)

    @pl.when(fast_ok)
    def _():
        lax.fori_loop(0, seq // C, make_chunk(scores_factored), st0, unroll=HGRN_UNROLL)

    @pl.when(jnp.logical_not(fast_ok))
    def _():
        lax.fori_loop(0, seq // C, make_chunk(scores_direct), st0)


def _hgrn(q, lf, kk, v, g, norm_g, batch, seq):
    shp = (batch, seq, D_MODEL)
    args = [a.reshape(shp) for a in (q, lf, kk, v, g)]
    spec = pl.BlockSpec((1, seq, LANES), lambda bb, h: (bb, 0, h))
    return pl.pallas_call(
        functools.partial(_hgrn_kernel, seq=seq),
        grid=(batch, N_HEADS_C),
        in_specs=[spec] * 5 + [pl.BlockSpec((1, LANES), lambda bb, h: (0, 0))],
        out_specs=spec,
        out_shape=jax.ShapeDtypeStruct(shp, BF16),
        compiler_params=_params("parallel", "parallel"),
        name="hgrn",
    )(*args, norm_g.astype(F32).reshape(1, LANES))


def _layer_norm_rows(z, g, b):
    mu = jnp.mean(z, axis=1, keepdims=True)
    zc = z - mu
    var = jnp.mean(zc * zc, axis=1, keepdims=True)
    return zc * lax.rsqrt(var + LN_EPS) * g + b


def _first_lane_of_max(vals, lane_f):
    mx = jnp.max(vals, axis=1, keepdims=True)
    return mx, jnp.min(jnp.where(vals == mx, lane_f, float(LANES)), axis=1, keepdims=True)


def _outproj_ln_kernel(*refs, n_parts):
    o_refs, w_refs = refs[:n_parts], refs[n_parts:2 * n_parts]
    (x_ref, g_ref, b_ref, wr_hi_ref, wr_lo_ref, rb_ref,
     y_ref, yb_ref, fields_ref, info_ref, cnt_ref) = refs[2 * n_parts:]
    mix = jnp.dot(o_refs[0][...], w_refs[0][...], preferred_element_type=F32)
    for o_ref, w_ref in zip(o_refs[1:], w_refs[1:]):
        mix += jnp.dot(o_ref[...], w_ref[...], preferred_element_type=F32)
    y = _layer_norm_rows(DEEPNORM_ALPHA * x_ref[...] + mix, g_ref[...], b_ref[...])
    y_ref[...] = y
    y_hi = y.astype(BF16)
    yb_ref[...] = y_hi
    y_lo = (y - y_hi.astype(F32)).astype(BF16)
    lg = (jnp.dot(y_hi, wr_hi_ref[...], preferred_element_type=F32)
          + jnp.dot(y_lo, wr_hi_ref[...], preferred_element_type=F32)
          + jnp.dot(y_hi, wr_lo_ref[...], preferred_element_type=F32)
          + rb_ref[...])
    lane = lax.broadcasted_iota(jnp.int32, lg.shape, 1)
    lane_f = lane.astype(F32)
    is_g = lane < N_GROUPS
    mg, grp = _first_lane_of_max(jnp.where(is_g, lg, -jnp.inf), lane_f)
    pg = 1.0 / jnp.sum(jnp.where(is_g, jnp.exp(lg - mg), 0.0), axis=1, keepdims=True)
    lo = float(N_GROUPS) + float(EXPERTS_PER_GROUP) * grp
    le = jnp.where((lane_f >= lo) & (lane_f < lo + float(EXPERTS_PER_GROUP)), lg, -jnp.inf)
    m1, i1 = _first_lane_of_max(le, lane_f)
    m2, i2 = _first_lane_of_max(jnp.where(lane_f == i1, -jnp.inf, le), lane_f)
    e21 = jnp.exp(m2 - m1)
    w0 = pg / (1.0 + e21)
    w1 = pg * e21 / (1.0 + e21)
    eid0 = i1 - float(N_GROUPS)
    eid1 = i2 - float(N_GROUPS)

    @pl.when(pl.program_id(0) == 0)
    def _():
        cnt_ref[...] = jnp.zeros(cnt_ref.shape, F32)

    hits = jnp.where(lane_f == eid0, 1.0, 0.0) + jnp.where(lane_f == eid1, 1.0, 0.0)
    cnt_ref[...] += jnp.sum(hits, axis=0, keepdims=True)
    fields = jnp.where(lane == 0, eid0, jnp.where(lane == 1, eid1, jnp.where(lane == 2, w0,
                       jnp.where(lane == 3, w1, 0.0))))
    fields_ref[...] = fields
    info_ref[...] = fields.T[:8, :]


def _outproj_ln(o_parts, x2d, w_bf16, ln_g, ln_b, wr, rb, tm=512):
    t, d = x2d.shape
    wr_hi = wr.astype(BF16)
    wr_lo = (wr - wr_hi.astype(F32)).astype(BF16)
    vec = lambda n: pl.BlockSpec((1, n), lambda i: (0, 0))
    widths = [o.shape[1] for o in o_parts]
    offs = np.cumsum([0] + widths)
    w_parts = [w_bf16[offs[k]:offs[k + 1]] for k in range(len(widths))]
    return pl.pallas_call(
        functools.partial(_outproj_ln_kernel, n_parts=len(widths)),
        grid=(t // tm,),
        in_specs=[pl.BlockSpec((tm, wd), lambda i: (i, 0)) for wd in widths]
                 + [pl.BlockSpec((wd, d), lambda i: (0, 0)) for wd in widths]
                 + [pl.BlockSpec((tm, d), lambda i: (i, 0)),
                  vec(d), vec(d),
                  pl.BlockSpec((d, LANES), lambda i: (0, 0)),
                  pl.BlockSpec((d, LANES), lambda i: (0, 0)),
                  vec(LANES)],
        out_specs=[pl.BlockSpec((tm, d), lambda i: (i, 0)),
                   pl.BlockSpec((tm, d), lambda i: (i, 0)),
                   pl.BlockSpec((tm, LANES), lambda i: (i, 0)),
                   pl.BlockSpec((8, tm), lambda i: (0, i)),
                   vec(LANES)],
        out_shape=[jax.ShapeDtypeStruct((t, d), F32),
                   jax.ShapeDtypeStruct((t, d), BF16),
                   jax.ShapeDtypeStruct((t, LANES), F32),
                   jax.ShapeDtypeStruct((8, t), F32),
                   jax.ShapeDtypeStruct((1, LANES), F32)],
        compiler_params=_params("arbitrary"),
        name="outproj_ln",
    )(*o_parts, *w_parts, x2d, ln_g.reshape(1, d), ln_b.reshape(1, d), wr_hi, wr_lo, rb)


def _expert_kernel(te_ref, nv_ref, xs_ref, w1_ref, w3_ref, w2_ref, o_ref, w1_s, w3_s, w2_s):
    i = pl.program_id(0)
    prev = te_ref[jnp.maximum(i - 1, 0)]

    @pl.when((i == 0) | (te_ref[i] != prev))
    def _cast_weights():
        w1_s[...] = w1_ref[0, 0].astype(BF16)
        w3_s[...] = w3_ref[0, 0].astype(BF16)
        w2_s[...] = w2_ref[0, 0].astype(BF16)

    @pl.when(i < nv_ref[0])
    def _ffn():
        for r in range(0, MOE_TILE, MOE_SLAB):
            xb = xs_ref[r:r + MOE_SLAB, :]
            a = jnp.dot(xb, w1_s[...], preferred_element_type=F32)
            u = jnp.dot(xb, w3_s[...], preferred_element_type=F32)
            hb = (a / (1.0 + jnp.exp(-a))) * u
            o_ref[r:r + MOE_SLAB, :] = jnp.dot(hb.astype(BF16), w2_s[...],
                                               preferred_element_type=F32).astype(o_ref.dtype)

    @pl.when(i >= nv_ref[0])
    def _pad():
        o_ref[...] = jnp.zeros(o_ref.shape, o_ref.dtype)


def _expert_ffn(tile_expert, n_valid, xs, w1, w3, w2, layer):
    p, d = xs.shape
    f = w1.shape[3]
    n_tiles = p // MOE_TILE
    grid_spec = pltpu.PrefetchScalarGridSpec(
        num_scalar_prefetch=2,
        grid=(n_tiles,),
        in_specs=[pl.BlockSpec((MOE_TILE, d), lambda i, te, nv: (i, 0)),
                  pl.BlockSpec((1, 1, d, f), lambda i, te, nv: (layer, te[i], 0, 0)),
                  pl.BlockSpec((1, 1, d, f), lambda i, te, nv: (layer, te[i], 0, 0)),
                  pl.BlockSpec((1, 1, f, d), lambda i, te, nv: (layer, te[i], 0, 0))],
        out_specs=pl.BlockSpec((MOE_TILE, d), lambda i, te, nv: (i, 0)),
        scratch_shapes=[pltpu.VMEM((d, f), BF16), pltpu.VMEM((d, f), BF16), pltpu.VMEM((f, d), BF16)],
    )
    return pl.pallas_call(
        _expert_kernel,
        grid_spec=grid_spec,
        out_shape=jax.ShapeDtypeStruct((p, d), BF16),
        compiler_params=_params("arbitrary"),
        name="expert_ffn",
    )(tile_expert, n_valid, xs, w1, w3, w2)


def _combine_ln_kernel(x_ref, y0_ref, y1_ref, fields_ref, g_ref, b_ref, o_ref):
    fields = fields_ref[...]
    ffn = fields[:, 2:3] * y0_ref[...].astype(F32) + fields[:, 3:4] * y1_ref[...].astype(F32)
    o_ref[...] = _layer_norm_rows(DEEPNORM_ALPHA * x_ref[...] + ffn, g_ref[...], b_ref[...])


def _combine_ln(x2d, y0, y1, fields, ln_g, ln_b, tm=512):
    t, d = x2d.shape
    row = pl.BlockSpec((tm, d), lambda i: (i, 0))
    vec = pl.BlockSpec((1, d), lambda i: (0, 0))
    return pl.pallas_call(
        _combine_ln_kernel,
        grid=(t // tm,),
        in_specs=[row, row, row, pl.BlockSpec((tm, LANES), lambda i: (i, 0)), vec, vec],
        out_specs=row,
        out_shape=jax.ShapeDtypeStruct((t, d), F32),
        compiler_params=_params("parallel"),
        name="combine_ln",
    )(x2d, y0, y1, fields, ln_g.reshape(1, d), ln_b.reshape(1, d))


def _dispatch_plan(info, counts_f):
    t = info.shape[1]
    a = t * TOPK_IN_GROUP
    n_fill = N_EXPERTS * MOE_TILE
    n_tiles = a // MOE_TILE + N_EXPERTS
    experts = jnp.arange(N_EXPERTS, dtype=jnp.int32)
    counts = counts_f[0, :N_EXPERTS].astype(jnp.int32)
    padded = (counts + MOE_TILE - 1) // MOE_TILE * MOE_TILE
    pends = jnp.cumsum(padded)
    fill_ends = jnp.cumsum(padded - counts)
    fill = jnp.arange(n_fill, dtype=jnp.int32)
    fill_key = jnp.sum((fill_ends[None, :] <= fill[:, None]).astype(jnp.int32), axis=1)
    tok = jnp.arange(t, dtype=jnp.int32)
    keys = jnp.concatenate([info[0].astype(jnp.int32), info[1].astype(jnp.int32), fill_key])
    toks = jnp.concatenate([tok, tok, fill % t])
    flat = jnp.arange(a + n_fill, dtype=jnp.int32)
    _, slot_tok, slot_flat = lax.sort((keys, toks, flat), num_keys=1, is_stable=True)
    _, slot_of = lax.sort((slot_flat, flat), num_keys=1)
    tile_start = jnp.arange(n_tiles, dtype=jnp.int32) * MOE_TILE
    tile_expert = jnp.sum((pends[None, :] <= tile_start[:, None]).astype(jnp.int32), axis=1)
    last_used = jnp.max(jnp.where(counts > 0, experts, 0))
    tile_expert = jnp.minimum(tile_expert, last_used)
    n_valid = (pends[-1] // MOE_TILE).astype(jnp.int32).reshape(1)
    return tile_expert, n_valid, slot_tok, slot_of[:t], slot_of[t:a]


def _moe_and_norm(x1, x1_bf16, fields, info, counts_f, w1, w3, w2, layer, ln_g, ln_b):
    tile_expert, n_valid, slot_tok, dest0, dest1 = _dispatch_plan(info, counts_f)
    yb = _expert_ffn(tile_expert, n_valid, x1_bf16[slot_tok], w1, w3, w2, layer)
    return _combine_ln(x1, yb[dest0], yb[dest1], fields, ln_g, ln_b)


def kernel(x, ab_w_in, ab_w_out, c_w_in, c_w_out, c_norm_g, hgrn_lb_logits, ln_g, ln_b,
           router_g_w, router_g_b, router_e_w, router_e_b, exp_w1, exp_w3, exp_w2):
    batch, seq, d = x.shape
    t = batch * seq
    tables = _rope_lane_tables(seq)
    bias_tab = jnp.asarray(_dilated_bias_table())
    lb_all = jnp.cumsum(jax.nn.softmax(hgrn_lb_logits.astype(F32), axis=0), axis=0)
    lb_all = lb_all - lb_all[0:1]
    n_pairs = (AB_WIDTH // 2) // LANES

    xc = x.reshape(t, d)
    for l in range(DEPTH):
        j = l // 2
        if l % 2 == 0:
            h = _proj_ab(xc, ab_w_in[j].astype(BF16), tables, seq).reshape(batch, seq, 3 * AB_WIDTH)
            o_parts = [_moba(h, n_pairs).reshape(t, AB_WIDTH // 2),
                       _dilated(h, bias_tab, n_pairs).reshape(t, AB_WIDTH // 2)]
            w_out = ab_w_out[j]
        else:
            q, lf, kk, v, g = _proj_c(xc, c_w_in[j].astype(BF16), lb_all[j])
            o_parts = [_hgrn(q, lf, kk, v, g, c_norm_g[j], batch, seq).reshape(t, d)]
            w_out = c_w_out[j]
        wr = jnp.zeros((d, LANES), F32)
        wr = wr.at[:, :N_GROUPS].set(router_g_w[l]).at[:, N_GROUPS:N_GROUPS + N_EXPERTS].set(router_e_w[l])
        rb = jnp.zeros((1, LANES), F32)
        rb = rb.at[0, :N_GROUPS].set(router_g_b[l]).at[0, N_GROUPS:N_GROUPS + N_EXPERTS].set(router_e_b[l])
        x1, x1_bf16, fields, info, counts_f = _outproj_ln(o_parts, xc, w_out.astype(BF16), ln_g[l, 0], ln_b[l, 0],
                                                          wr, rb)
        xc = _moe_and_norm(x1, x1_bf16, fields, info, counts_f, exp_w1, exp_w3, exp_w2, l,
                           ln_g[l, 1], ln_b[l, 1])
    return xc.reshape(batch, seq, d)
```

```python
import functools
import math

import numpy as np
import jax
import jax.numpy as jnp
from jax import lax
from jax.experimental import pallas as pl
from jax.experimental.pallas import tpu as pltpu

F32 = jnp.float32
BF16 = jnp.bfloat16

D_MODEL = 1024
DEPTH = 4
HEAD_DIM = 64
N_HEADS_A = 8
N_HEADS_B = 8
AB_WIDTH = (N_HEADS_A + N_HEADS_B) * HEAD_DIM
ROT_DIM = HEAD_DIM // 4
ROPE_THETA = 500000.0
MOBA_BLOCK = 256
MOBA_TOPK = 3
DILATED_BRANCHES = ((128, 1), (512, 4), (2048, 16))
HGRN_EXPAND = 128
N_HEADS_C = D_MODEL // HGRN_EXPAND
HGRN_CHUNK = 64
HGRN_SUB = 16
HGRN_UNROLL = 8
HGRN_FAST_LIMIT = 60.0
N_GROUPS = 4
EXPERTS_PER_GROUP = 8
N_EXPERTS = N_GROUPS * EXPERTS_PER_GROUP
TOPK_IN_GROUP = 2
D_EXPERT = D_MODEL // 2
DEEPNORM_ALPHA = (2.0 * DEPTH) ** 0.25
LN_EPS = 1e-5
RMS_EPS = 1e-6

LANES = 128
ATT_BLOCK = 256
MOBA_STEP = 4
MOBA_SEL_ROWS = 16
SHIFT_LANE = LANES - 2
ATT_FAST_LIMIT = 38.0
DIL_STEP = 3
NEG_BIG = -1e30
MOE_TILE = 512
MOE_SLAB = 512
VMEM_LIMIT = 56 * 1024 * 1024

_NT = (((1,), (1,)), ((), ()))
_TN = (((0,), (0,)), ((), ()))


def _params(*sem):
    return pltpu.CompilerParams(dimension_semantics=sem, vmem_limit_bytes=VMEM_LIMIT)


def _proj_ab_kernel(x_ref, w_ref, cos_ref, sa_ref, sb_ref, o_ref):
    xb = x_ref[...].astype(BF16)
    sec = AB_WIDTH // 2
    for c in range(6):
        acc = jnp.dot(xb, w_ref[:, c * sec:(c + 1) * sec], preferred_element_type=F32)
        if c % 3 == 2:
            o_ref[:, c * sec:(c + 1) * sec] = acc.astype(BF16)
            continue
        cosv = cos_ref[...]
        sa = sa_ref[...]
        sb = sb_ref[...]
        for k in range(sec // LANES):
            a = acc[:, k * LANES:(k + 1) * LANES]
            r = a * cosv + pltpu.roll(a, LANES - ROT_DIM // 2, 1) * sa + pltpu.roll(a, ROT_DIM // 2, 1) * sb
            if c % 3 == 0:
                r = r * (HEAD_DIM ** -0.5)
            o_ref[:, c * sec + k * LANES:c * sec + (k + 1) * LANES] = r.astype(BF16)


def _rope_lane_tables(seq):
    half = ROT_DIM // 2
    inv = ROPE_THETA ** (-jnp.arange(half, dtype=F32) / half)
    ang = jnp.arange(seq, dtype=F32)[:, None] * inv[None, :]
    cos, sin = jnp.cos(ang), jnp.sin(ang)
    ones = jnp.ones((seq, HEAD_DIM - ROT_DIM), F32)
    zeros = jnp.zeros((seq, HEAD_DIM - ROT_DIM), F32)
    zh = jnp.zeros((seq, half), F32)
    cos_t = jnp.concatenate([cos, cos, ones], axis=1)
    sa_t = jnp.concatenate([-sin, zh, zeros], axis=1)
    sb_t = jnp.concatenate([zh, sin, zeros], axis=1)
    rep = LANES // HEAD_DIM
    return tuple(jnp.tile(t, (1, rep)) for t in (cos_t, sa_t, sb_t))


def _proj_ab(x2d, w_bf16, tables, seq, tm=512):
    t, d = x2d.shape
    n = w_bf16.shape[1]
    nseq = seq // tm
    tab_spec = pl.BlockSpec((tm, LANES), lambda i: (i % nseq, 0))
    return pl.pallas_call(
        _proj_ab_kernel,
        grid=(t // tm,),
        in_specs=[pl.BlockSpec((tm, d), lambda i: (i, 0)),
                  pl.BlockSpec((d, n), lambda i: (0, 0)),
                  tab_spec, tab_spec, tab_spec],
        out_specs=pl.BlockSpec((tm, n), lambda i: (i, 0)),
        out_shape=jax.ShapeDtypeStruct((t, n), BF16),
        compiler_params=_params("parallel"),
        name="proj_ab",
    )(x2d, w_bf16, *tables)


def _softmax_step(parts, vaug, h, m_scr, l_scr, acc_scr):
    m_prev = m_scr[h]
    mx = functools.reduce(jnp.maximum, parts)
    m_new = jnp.maximum(m_prev, jnp.max(mx, axis=1, keepdims=True))
    alpha = jnp.exp(m_prev - m_new)
    p = jnp.concatenate([jnp.exp(part - m_new).astype(BF16) for part in parts], axis=1)
    pv = jnp.dot(p, vaug, preferred_element_type=F32)
    acc_scr[h] = alpha * acc_scr[h] + pv[:, :LANES]
    l_scr[h] = alpha * l_scr[h] + pv[:, LANES:]
    m_scr[h] = m_new


def _plain_step(parts, vaug, h, l_scr, acc_scr):
    p = jnp.concatenate([jnp.exp(part).astype(BF16) for part in parts], axis=1)
    pv = jnp.dot(p, vaug, preferred_element_type=F32)
    acc_scr[h] += pv[:, :LANES]
    l_scr[h] += pv[:, LANES:]


def _lane_parts(s):
    return [s[:, c * LANES:(c + 1) * LANES] for c in range(s.shape[1] // LANES)]


def _head_mask(lane, h):
    return (lane >= h * HEAD_DIM) & (lane < (h + 1) * HEAD_DIM)


def _head_sq_norms(x, h):
    lane = lax.broadcasted_iota(jnp.int32, x.shape, 1)
    xf = x.astype(F32)
    sq = jnp.where(_head_mask(lane, h), xf * xf, 0.0).astype(BF16)
    return jnp.dot(sq, jnp.ones((LANES, LANES), BF16), preferred_element_type=F32)


def _score_shifts(q, k, flag_ref):
    bounds = [jnp.sqrt(jnp.max(_head_sq_norms(q, h), axis=0, keepdims=True)
                       * jnp.max(_head_sq_norms(k, h), axis=0, keepdims=True)) for h in range(2)]
    fast_ok = jnp.max(jnp.maximum(bounds[0], bounds[1])) <= ATT_FAST_LIMIT
    flag_ref[0] = jnp.where(fast_ok, 1, 0).astype(jnp.int32)
    return [jnp.where(fast_ok, -b, 0.0) for b in bounds]


def _softmax_init(m_scr, l_scr, acc_scr):
    m_scr[...] = jnp.full(m_scr.shape, -jnp.inf, F32)
    l_scr[...] = jnp.zeros(l_scr.shape, F32)
    acc_scr[...] = jnp.zeros(acc_scr.shape, F32)


def _softmax_finish(o_ref, l_scr, acc_scr):
    lane = lax.broadcasted_iota(jnp.int32, acc_scr.shape[1:], 1)
    o0 = acc_scr[0] / l_scr[0]
    o1 = acc_scr[1] / l_scr[1]
    o_ref[0] = jnp.where(lane < HEAD_DIM, o0, o1).astype(o_ref.dtype)


def _moba_kernel(q_ref, k_ref, v_ref, o_ref, kaug_scr, vaug_scr, qaug_scr, flag_ref, m_scr, l_scr, acc_scr, *, seq):
    bq = ATT_BLOCK
    pad = (MOBA_STEP - 1) * bq
    i = pl.program_id(2)

    @pl.when(i == 0)
    def _per_sequence_setup():
        q = q_ref[0]
        k = k_ref[0]
        rows = lax.broadcasted_iota(jnp.int32, (MOBA_SEL_ROWS, seq), 0)
        cols = lax.broadcasted_iota(jnp.int32, (MOBA_SEL_ROWS, seq), 1)
        own = jnp.right_shift(cols, int(math.log2(MOBA_BLOCK)))
        ind = jnp.where(own == rows, 1.0, 0.0).astype(BF16)
        km = jnp.dot(ind, k, preferred_element_type=F32) * (1.0 / MOBA_BLOCK)
        km_hi = km.astype(BF16)
        km_lo = (km - km_hi.astype(F32)).astype(BF16)
        shifts = _score_shifts(q, k, flag_ref)
        blk_lane = lax.broadcasted_iota(jnp.int32, (bq, LANES), 1)
        kaug_scr[0:pad, 0:LANES] = jnp.zeros((pad, LANES), BF16)
        vaug_scr[0:pad, 0:LANES] = jnp.zeros((pad, LANES), BF16)
        for jb in range(-(MOBA_STEP - 1), seq // bq):
            tgt = jb if jb >= 0 else LANES - 1
            kaug_scr[pad + jb * bq:pad + (jb + 1) * bq, LANES:] = jnp.where(
                (blk_lane == tgt) | (blk_lane == SHIFT_LANE), 1.0, 0.0).astype(BF16)
        kaug_scr[pad:, 0:LANES] = k
        vaug_scr[pad:, 0:LANES] = v_ref[0]
        vaug_scr[:, LANES:] = jnp.ones((seq + pad, LANES), BF16)
        lane_all = lax.broadcasted_iota(jnp.int32, (seq, LANES), 1)
        rows_f = rows.astype(F32)
        for h in range(2):
            qh = jnp.where(_head_mask(lane_all, h), q, jnp.zeros_like(q))
            g = (lax.dot_general(km_hi, qh, _NT, preferred_element_type=F32)
                 + lax.dot_general(km_lo, qh, _NT, preferred_element_type=F32))
            g = jnp.where(rows < own, g, -jnp.inf)
            bias = jnp.where(rows == own, 0.0, NEG_BIG)
            for _ in range(MOBA_TOPK):
                mx = jnp.max(g, axis=0, keepdims=True)
                first = jnp.min(jnp.where(g == mx, rows_f, float(MOBA_SEL_ROWS)), axis=0, keepdims=True)
                pick = (rows_f == first) & (mx > -jnp.inf)
                bias = jnp.where(pick, 0.0, bias)
                g = jnp.where(pick, -jnp.inf, g)
            bias = jnp.concatenate([bias, jnp.full((LANES - MOBA_SEL_ROWS, seq), NEG_BIG, F32)], axis=0).T
            qaug_scr[h, :, 0:LANES] = qh
            qaug_scr[h, :, LANES:] = jnp.where(lane_all == SHIFT_LANE, shifts[h], bias).astype(BF16)

    lane = lax.broadcasted_iota(jnp.int32, (bq, LANES), 1)
    _softmax_init(m_scr, l_scr, acc_scr)
    n_parts = MOBA_STEP * bq // LANES
    own_parts = bq // LANES
    qi = lax.broadcasted_iota(jnp.int32, (bq, LANES), 0)
    q_start = pl.multiple_of(i * bq, bq)

    def step(t, first, fast):
        start = pl.multiple_of((i - MOBA_STEP * t) * bq, bq)
        kaug = kaug_scr[pl.ds(start, MOBA_STEP * bq), :]
        vaug = vaug_scr[pl.ds(start, MOBA_STEP * bq), :]
        for h in range(2):
            parts = _lane_parts(lax.dot_general(qaug_scr[h, pl.ds(q_start, bq), :], kaug, _NT,
                                                preferred_element_type=F32))
            if first:
                for c in range(own_parts):
                    cc = n_parts - own_parts + c
                    parts[cc] = jnp.where(lane + c * LANES <= qi, parts[cc], NEG_BIG)
            if fast:
                _plain_step(parts, vaug, h, l_scr, acc_scr)
            else:
                _softmax_step(parts, vaug, h, m_scr, l_scr, acc_scr)

    def run(fast):
        step(0, True, fast)

        def later(t, carry):
            step(t, False, fast)
            return carry

        lax.fori_loop(1, (i + MOBA_STEP) // MOBA_STEP, later, 0)

    pl.when(flag_ref[0] == 1)(functools.partial(run, True))
    pl.when(flag_ref[0] != 1)(functools.partial(run, False))
    _softmax_finish(o_ref, l_scr, acc_scr)


def _moba(h3, n_pairs):
    b, seq, _ = h3.shape
    sec_blocks = (AB_WIDTH // 2) // LANES
    nq = seq // ATT_BLOCK
    assert seq % ATT_BLOCK == 0 and nq <= MOBA_SEL_ROWS
    return pl.pallas_call(
        functools.partial(_moba_kernel, seq=seq),
        grid=(b, n_pairs, nq),
        in_specs=[pl.BlockSpec((1, seq, LANES), lambda bb, p, i: (bb, 0, p)),
                  pl.BlockSpec((1, seq, LANES), lambda bb, p, i: (bb, 0, sec_blocks + p)),
                  pl.BlockSpec((1, seq, LANES), lambda bb, p, i: (bb, 0, 2 * sec_blocks + p))],
        out_specs=pl.BlockSpec((1, ATT_BLOCK, LANES), lambda bb, p, i: (bb, i, p)),
        out_shape=jax.ShapeDtypeStruct((b, seq, n_pairs * LANES), BF16),
        scratch_shapes=[pltpu.VMEM((seq + (MOBA_STEP - 1) * ATT_BLOCK, 2 * LANES), BF16),
                        pltpu.VMEM((seq + (MOBA_STEP - 1) * ATT_BLOCK, 2 * LANES), BF16),
                        pltpu.VMEM((2, seq, 2 * LANES), BF16),
                        pltpu.SMEM((1,), jnp.int32),
                        pltpu.VMEM((2, ATT_BLOCK, LANES), F32),
                        pltpu.VMEM((2, ATT_BLOCK, LANES), F32),
                        pltpu.VMEM((2, ATT_BLOCK, LANES), F32)],
        compiler_params=_params("parallel", "parallel", "arbitrary"),
        name="moba",
    )(h3, h3, h3)


def _dilated_n_blocks():
    return max(w for w, _ in DILATED_BRANCHES) // ATT_BLOCK + 1


def _dilated_bias_table():
    n_steps = -(-_dilated_n_blocks() // DIL_STEP)
    qi = np.arange(ATT_BLOCK)[:, None]
    ki = np.arange(ATT_BLOCK)[None, :]
    tabs = []
    for t in range(n_steps):
        groups = []
        for g in range(DIL_STEP):
            d = qi - ki + (DIL_STEP * t + DIL_STEP - 1 - g) * ATT_BLOCK
            cnt = np.zeros(d.shape, np.int64)
            for window, dil in DILATED_BRANCHES:
                cnt += ((d >= 0) & (d <= window) & (d % dil == 0)).astype(np.int64)
            groups.append(np.where(cnt > 0, np.log(np.maximum(cnt, 1).astype(np.float64)), NEG_BIG))
        tabs.append(np.concatenate(groups, axis=1))
    return np.stack(tabs).astype(np.float32)


def _dilated_kernel(q_ref, k_ref, v_ref, bias_ref, o_ref, kaug_scr, vaug_scr, qaug_scr, flag_ref, m_scr, l_scr, acc_scr,
                    *, seq):
    bq = ATT_BLOCK
    pad = (DIL_STEP - 1) * bq
    i = pl.program_id(2)

    @pl.when(i == 0)
    def _per_sequence_setup():
        q = q_ref[0]
        k = k_ref[0]
        shifts = _score_shifts(q, k, flag_ref)
        flag_lane = lax.broadcasted_iota(jnp.int32, (pad, LANES), 1)
        lane_all = lax.broadcasted_iota(jnp.int32, (seq, LANES), 1)
        kaug_scr[0:pad, 0:LANES] = jnp.zeros((pad, LANES), BF16)
        kaug_scr[0:pad, LANES:] = jnp.where((flag_lane == 0) | (flag_lane == SHIFT_LANE), 1.0, 0.0).astype(BF16)
        kaug_scr[pad:, 0:LANES] = k
        kaug_scr[pad:, LANES:] = jnp.where(lane_all == SHIFT_LANE, 1.0, 0.0).astype(BF16)
        vaug_scr[0:pad, 0:LANES] = jnp.zeros((pad, LANES), BF16)
        vaug_scr[pad:, 0:LANES] = v_ref[0]
        vaug_scr[:, LANES:] = jnp.ones((seq + pad, LANES), BF16)
        for h in range(2):
            aug = jnp.where(lane_all == SHIFT_LANE, shifts[h], jnp.where(lane_all == 0, NEG_BIG, 0.0))
            qaug_scr[h, :, 0:LANES] = jnp.where(_head_mask(lane_all, h), q, jnp.zeros_like(q))
            qaug_scr[h, :, LANES:] = aug.astype(BF16)

    _softmax_init(m_scr, l_scr, acc_scr)
    q_start = pl.multiple_of(i * bq, bq)

    def step(t, fast):
        start = pl.multiple_of((i - DIL_STEP * t) * bq, bq)
        kaug = kaug_scr[pl.ds(start, DIL_STEP * bq), :]
        vaug = vaug_scr[pl.ds(start, DIL_STEP * bq), :]
        bias = bias_ref[t]
        for h in range(2):
            parts = _lane_parts(lax.dot_general(qaug_scr[h, pl.ds(q_start, bq), :], kaug, _NT,
                                                preferred_element_type=F32) + bias)
            if fast:
                _plain_step(parts, vaug, h, l_scr, acc_scr)
            else:
                _softmax_step(parts, vaug, h, m_scr, l_scr, acc_scr)

    n_steps = (jnp.minimum(i, _dilated_n_blocks() - 1) + DIL_STEP) // DIL_STEP

    def run(fast):
        def body(t, carry):
            step(t, fast)
            return carry

        lax.fori_loop(0, n_steps, body, 0)

    pl.when(flag_ref[0] == 1)(functools.partial(run, True))
    pl.when(flag_ref[0] != 1)(functools.partial(run, False))
    _softmax_finish(o_ref, l_scr, acc_scr)


def _dilated(h3, bias_tab, n_pairs):
    b, seq, _ = h3.shape
    sec_blocks = (AB_WIDTH // 2) // LANES
    base = 3 * sec_blocks
    nq = seq // ATT_BLOCK
    pad = (DIL_STEP - 1) * ATT_BLOCK
    return pl.pallas_call(
        functools.partial(_dilated_kernel, seq=seq),
        grid=(b, n_pairs, nq),
        in_specs=[pl.BlockSpec((1, seq, LANES), lambda bb, p, i: (bb, 0, base + p)),
                  pl.BlockSpec((1, seq, LANES), lambda bb, p, i: (bb, 0, base + sec_blocks + p)),
                  pl.BlockSpec((1, seq, LANES), lambda bb, p, i: (bb, 0, base + 2 * sec_blocks + p)),
                  pl.BlockSpec(bias_tab.shape, lambda bb, p, i: (0, 0, 0))],
        out_specs=pl.BlockSpec((1, ATT_BLOCK, LANES), lambda bb, p, i: (bb, i, p)),
        out_shape=jax.ShapeDtypeStruct((b, seq, n_pairs * LANES), BF16),
        scratch_shapes=[pltpu.VMEM((seq + pad, 2 * LANES), BF16),
                        pltpu.VMEM((seq + pad, 2 * LANES), BF16),
                        pltpu.VMEM((2, seq, 2 * LANES), BF16),
                        pltpu.SMEM((1,), jnp.int32),
                        pltpu.VMEM((2, ATT_BLOCK, LANES), F32),
                        pltpu.VMEM((2, ATT_BLOCK, LANES), F32),
                        pltpu.VMEM((2, ATT_BLOCK, LANES), F32)],
        compiler_params=_params("parallel", "parallel", "arbitrary"),
        name="dilated",
    )(h3, h3, h3, bias_tab)


def _proj_c_kernel(x_ref, w_ref, loglb_ref, log1mlb_ref, omlb_ref, q_ref, lf_ref, kk_ref, v_ref, g_ref):
    xb = x_ref[...].astype(BF16)
    d = D_MODEL

    def sec(c):
        return jnp.dot(xb, w_ref[:, c * d:(c + 1) * d], preferred_element_type=F32)

    q_ref[...] = sec(0)
    z = sec(1)
    log_sig = jnp.minimum(z, 0.0) - jnp.log1p(jnp.exp(-jnp.abs(z)))
    a = loglb_ref[...]
    c = log1mlb_ref[...] + log_sig
    lf_ref[...] = jnp.maximum(a, c) + jnp.log1p(jnp.exp(-jnp.abs(a - c)))
    kk_ref[...] = omlb_ref[...] / (1.0 + jnp.exp(z))
    v_ref[...] = sec(2).astype(v_ref.dtype)
    g_ref[...] = sec(3)


def _proj_c(x2d, w_bf16, lb, tm=512):
    t, d = x2d.shape
    n = w_bf16.shape[1]
    lb = lb.astype(F32).reshape(1, d)
    vec_spec = pl.BlockSpec((1, d), lambda i: (0, 0))
    out_spec = pl.BlockSpec((tm, d), lambda i: (i, 0))
    sds = jax.ShapeDtypeStruct((t, d), F32)
    return pl.pallas_call(
        _proj_c_kernel,
        grid=(t // tm,),
        in_specs=[pl.BlockSpec((tm, d), lambda i: (i, 0)),
                  pl.BlockSpec((d, n), lambda i: (0, 0)),
                  vec_spec, vec_spec, vec_spec],
        out_specs=[out_spec] * 5,
        out_shape=[sds, sds, sds, jax.ShapeDtypeStruct((t, d), BF16), sds],
        compiler_params=_params("parallel"),
        name="proj_c",
    )(x2d, w_bf16, jnp.log(lb), jnp.log1p(-lb), 1.0 - lb)


def _split3_bf16(x):
    h1 = x.astype(BF16)
    r1 = x - h1.astype(F32)
    h2 = r1.astype(BF16)
    h3 = (r1 - h2.astype(F32)).astype(BF16)
    return h1, h2, h3


def _hgrn_kernel(q_ref, lf_ref, kk_ref, v_ref, g_ref, ng_ref, o_ref, *, seq):
    C, SUB = HGRN_CHUNK, HGRN_SUB
    n_sub = C // SUB
    ri = lax.broadcasted_iota(jnp.int32, (C, C), 0)
    ci = lax.broadcasted_iota(jnp.int32, (C, C), 1)
    tri = jnp.where(ci <= ri, 1.0, 0.0).astype(BF16)
    row_c = lax.broadcasted_iota(jnp.int32, (C, LANES), 0)
    row_s = lax.broadcasted_iota(jnp.int32, (SUB, C), 0)
    lane_s = lax.broadcasted_iota(jnp.int32, (SUB, C), 1)
    ng = ng_ref[...]

    def scores_factored(qc, kc, b):
        refs = [jnp.zeros((1, LANES), F32)] + [b[s * SUB - 1:s * SUB, :] for s in range(1, n_sub)]
        ref_rows = jnp.concatenate([jnp.broadcast_to(r, (SUB, LANES)) for r in refs], axis=0)
        qt = (qc * jnp.exp(b - ref_rows)).astype(BF16)
        rows_a = []
        for s in range(n_sub):
            hi = (s + 1) * SUB
            kt = (kc[:hi, :] * jnp.exp(refs[s] - b[:hi, :])).astype(BF16)
            if hi < C:
                kt = jnp.concatenate([kt, jnp.zeros((C - hi, LANES), BF16)], axis=0)
            rows_a.append(lax.dot_general(qt[s * SUB:hi, :], kt, _NT, preferred_element_type=F32))
        return jnp.where(ci <= ri, jnp.concatenate(rows_a, axis=0), 0.0)

    def scores_direct(qc, kc, b):
        rows_a = []
        for sidx in range(n_sub):
            lo = sidx * SUB
            q_i = qc[lo:lo + SUB, :]
            b_i = b[lo:lo + SUB, :]
            if sidx == 0:
                a_blk = jnp.zeros((SUB, C), F32)
            else:
                ref = b[lo - 1:lo, :]
                qt = (q_i * jnp.exp(b_i - ref)).astype(BF16)
                kt = jnp.where(row_c < lo, kc * jnp.exp(jnp.minimum(ref - b, 0.0)), 0.0).astype(BF16)
                a_blk = lax.dot_general(qt, kt, _NT, preferred_element_type=F32)
            for j in range(SUB):
                r_lo = 0 if j < 8 else 8
                bj = b[lo + j:lo + j + 1, :]
                kj = kc[lo + j:lo + j + 1, :]
                pj = q_i[r_lo:, :] * (jnp.exp(jnp.minimum(b_i[r_lo:, :] - bj, 0.0)) * kj)
                col = jnp.sum(pj, axis=1, keepdims=True)
                if r_lo:
                    col = jnp.concatenate([jnp.zeros((r_lo, 1), F32), col], axis=0)
                a_blk = jnp.where((lane_s == lo + j) & (row_s >= j), col, a_blk)
            rows_a.append(a_blk)
        return jnp.concatenate(rows_a, axis=0)

    def make_chunk(scores):
        def chunk(c, st):
            r0 = pl.multiple_of(c * C, C)
            qc = q_ref[0, pl.ds(r0, C), :]
            kc = kk_ref[0, pl.ds(r0, C), :]
            vb = v_ref[0, pl.ds(r0, C), :].astype(BF16)
            l1, l2, l3 = _split3_bf16(lf_ref[0, pl.ds(r0, C), :])
            b = (jnp.dot(tri, l1, preferred_element_type=F32)
                 + jnp.dot(tri, l2, preferred_element_type=F32)
                 + jnp.dot(tri, l3, preferred_element_type=F32))
            b_last = b[C - 1:C, :]
            inter = lax.dot_general((qc * jnp.exp(b)).astype(BF16), st.astype(BF16), _NT,
                                    preferred_element_type=F32)
            o = inter + jnp.dot(scores(qc, kc, b).astype(BF16), vb, preferred_element_type=F32)
            o = o * lax.rsqrt(jnp.mean(o * o, axis=1, keepdims=True) + RMS_EPS) * ng
            gc = g_ref[0, pl.ds(r0, C), :]
            o_ref[0, pl.ds(r0, C), :] = (o * (gc / (1.0 + jnp.exp(-gc)))).astype(o_ref.dtype)
            kd = (kc * jnp.exp(b_last - b)).astype(BF16)
            return st * jnp.exp(b_last) + lax.dot_general(vb, kd, _TN, preferred_element_type=F32)
        return chunk

    sub_decay = jnp.sum(lf_ref[0].reshape(seq // SUB, SUB, LANES), axis=1)
    fast_ok = jnp.min(sub_decay) >= -HGRN_FAST_LIMIT
    st0 = jnp.zeros((LANES, LANES), F32)

    @pl.when(fast_ok)
    def _():
        lax.fori_loop(0, seq // C, make_chunk(scores_factored), st0, unroll=HGRN_UNROLL)

    @pl.when(jnp.logical_not(fast_ok))
    def _():
        lax.fori_loop(0, seq // C, make_chunk(scores_direct), st0)


def _hgrn(q, lf, kk, v, g, norm_g, batch, seq):
    shp = (batch, seq, D_MODEL)
    args = [a.reshape(shp) for a in (q, lf, kk, v, g)]
    spec = pl.BlockSpec((1, seq, LANES), lambda bb, h: (bb, 0, h))
    return pl.pallas_call(
        functools.partial(_hgrn_kernel, seq=seq),
        grid=(batch, N_HEADS_C),
        in_specs=[spec] * 5 + [pl.BlockSpec((1, LANES), lambda bb, h: (0, 0))],
        out_specs=spec,
        out_shape=jax.ShapeDtypeStruct(shp, BF16),
        compiler_params=_params("parallel", "parallel"),
        name="hgrn",
    )(*args, norm_g.astype(F32).reshape(1, LANES))


def _layer_norm_rows(z, g, b):
    mu = jnp.mean(z, axis=1, keepdims=True)
    zc = z - mu
    var = jnp.mean(zc * zc, axis=1, keepdims=True)
    return zc * lax.rsqrt(var + LN_EPS) * g + b


def _first_lane_of_max(vals, lane_f):
    mx = jnp.max(vals, axis=1, keepdims=True)
    return mx, jnp.min(jnp.where(vals == mx, lane_f, float(LANES)), axis=1, keepdims=True)


def _outproj_ln_kernel(*refs, n_parts):
    o_refs, w_refs = refs[:n_parts], refs[n_parts:2 * n_parts]
    (x_ref, g_ref, b_ref, wr_hi_ref, wr_lo_ref, rb_ref,
     y_ref, yb_ref, fields_ref, info_ref, cnt_ref) = refs[2 * n_parts:]
    mix = jnp.dot(o_refs[0][...], w_refs[0][...], preferred_element_type=F32)
    for o_ref, w_ref in zip(o_refs[1:], w_refs[1:]):
        mix += jnp.dot(o_ref[...], w_ref[...], preferred_element_type=F32)
    y = _layer_norm_rows(DEEPNORM_ALPHA * x_ref[...] + mix, g_ref[...], b_ref[...])
    y_ref[...] = y
    y_hi = y.astype(BF16)
    yb_ref[...] = y_hi
    y_lo = (y - y_hi.astype(F32)).astype(BF16)
    lg = (jnp.dot(y_hi, wr_hi_ref[...], preferred_element_type=F32)
          + jnp.dot(y_lo, wr_hi_ref[...], preferred_element_type=F32)
          + jnp.dot(y_hi, wr_lo_ref[...], preferred_element_type=F32)
          + rb_ref[...])
    lane = lax.broadcasted_iota(jnp.int32, lg.shape, 1)
    lane_f = lane.astype(F32)
    is_g = lane < N_GROUPS
    mg, grp = _first_lane_of_max(jnp.where(is_g, lg, -jnp.inf), lane_f)
    pg = 1.0 / jnp.sum(jnp.where(is_g, jnp.exp(lg - mg), 0.0), axis=1, keepdims=True)
    lo = float(N_GROUPS) + float(EXPERTS_PER_GROUP) * grp
    le = jnp.where((lane_f >= lo) & (lane_f < lo + float(EXPERTS_PER_GROUP)), lg, -jnp.inf)
    m1, i1 = _first_lane_of_max(le, lane_f)
    m2, i2 = _first_lane_of_max(jnp.where(lane_f == i1, -jnp.inf, le), lane_f)
    e21 = jnp.exp(m2 - m1)
    w0 = pg / (1.0 + e21)
    w1 = pg * e21 / (1.0 + e21)
    eid0 = i1 - float(N_GROUPS)
    eid1 = i2 - float(N_GROUPS)

    @pl.when(pl.program_id(0) == 0)
    def _():
        cnt_ref[...] = jnp.zeros(cnt_ref.shape, F32)

    hits = jnp.where(lane_f == eid0, 1.0, 0.0) + jnp.where(lane_f == eid1, 1.0, 0.0)
    cnt_ref[...] += jnp.sum(hits, axis=0, keepdims=True)
    fields = jnp.where(lane == 0, eid0, jnp.where(lane == 1, eid1, jnp.where(lane == 2, w0,
                       jnp.where(lane == 3, w1, 0.0))))
    fields_ref[...] = fields
    info_ref[...] = fields.T[:8, :]


def _outproj_ln(o_parts, x2d, w_bf16, ln_g, ln_b, wr, rb, tm=512):
    t, d = x2d.shape
    wr_hi = wr.astype(BF16)
    wr_lo = (wr - wr_hi.astype(F32)).astype(BF16)
    vec = lambda n: pl.BlockSpec((1, n), lambda i: (0, 0))
    widths = [o.shape[1] for o in o_parts]
    offs = np.cumsum([0] + widths)
    w_parts = [w_bf16[offs[k]:offs[k + 1]] for k in range(len(widths))]
    return pl.pallas_call(
        functools.partial(_outproj_ln_kernel, n_parts=len(widths)),
        grid=(t // tm,),
        in_specs=[pl.BlockSpec((tm, wd), lambda i: (i, 0)) for wd in widths]
                 + [pl.BlockSpec((wd, d), lambda i: (0, 0)) for wd in widths]
                 + [pl.BlockSpec((tm, d), lambda i: (i, 0)),
                  vec(d), vec(d),
                  pl.BlockSpec((d, LANES), lambda i: (0, 0)),
                  pl.BlockSpec((d, LANES), lambda i: (0, 0)),
                  vec(LANES)],
        out_specs=[pl.BlockSpec((tm, d), lambda i: (i, 0)),
                   pl.BlockSpec((tm, d), lambda i: (i, 0)),
                   pl.BlockSpec((tm, LANES), lambda i: (i, 0)),
                   pl.BlockSpec((8, tm), lambda i: (0, i)),
                   vec(LANES)],
        out_shape=[jax.ShapeDtypeStruct((t, d), F32),
                   jax.ShapeDtypeStruct((t, d), BF16),
                   jax.ShapeDtypeStruct((t, LANES), F32),
                   jax.ShapeDtypeStruct((8, t), F32),
                   jax.ShapeDtypeStruct((1, LANES), F32)],
        compiler_params=_params("arbitrary"),
        name="outproj_ln",
    )(*o_parts, *w_parts, x2d, ln_g.reshape(1, d), ln_b.reshape(1, d), wr_hi, wr_lo, rb)


def _expert_kernel(te_ref, nv_ref, xs_ref, w1_ref, w3_ref, w2_ref, o_ref, w1_s, w3_s, w2_s):
    i = pl.program_id(0)
    prev = te_ref[jnp.maximum(i - 1, 0)]

    @pl.when((i == 0) | (te_ref[i] != prev))
    def _cast_weights():
        w1_s[...] = w1_ref[0, 0].astype(BF16)
        w3_s[...] = w3_ref[0, 0].astype(BF16)
        w2_s[...] = w2_ref[0, 0].astype(BF16)

    @pl.when(i < nv_ref[0])
    def _ffn():
        for r in range(0, MOE_TILE, MOE_SLAB):
            xb = xs_ref[r:r + MOE_SLAB, :]
            a = jnp.dot(xb, w1_s[...], preferred_element_type=F32)
            u = jnp.dot(xb, w3_s[...], preferred_element_type=F32)
            hb = (a / (1.0 + jnp.exp(-a))) * u
            o_ref[r:r + MOE_SLAB, :] = jnp.dot(hb.astype(BF16), w2_s[...],
                                               preferred_element_type=F32).astype(o_ref.dtype)

    @pl.when(i >= nv_ref[0])
    def _pad():
        o_ref[...] = jnp.zeros(o_ref.shape, o_ref.dtype)


def _expert_ffn(tile_expert, n_valid, xs, w1, w3, w2, layer):
    p, d = xs.shape
    f = w1.shape[3]
    n_tiles = p // MOE_TILE
    grid_spec = pltpu.PrefetchScalarGridSpec(
        num_scalar_prefetch=2,
        grid=(n_tiles,),
        in_specs=[pl.BlockSpec((MOE_TILE, d), lambda i, te, nv: (i, 0)),
                  pl.BlockSpec((1, 1, d, f), lambda i, te, nv: (layer, te[i], 0, 0)),
                  pl.BlockSpec((1, 1, d, f), lambda i, te, nv: (layer, te[i], 0, 0)),
                  pl.BlockSpec((1, 1, f, d), lambda i, te, nv: (layer, te[i], 0, 0))],
        out_specs=pl.BlockSpec((MOE_TILE, d), lambda i, te, nv: (i, 0)),
        scratch_shapes=[pltpu.VMEM((d, f), BF16), pltpu.VMEM((d, f), BF16), pltpu.VMEM((f, d), BF16)],
    )
    return pl.pallas_call(
        _expert_kernel,
        grid_spec=grid_spec,
        out_shape=jax.ShapeDtypeStruct((p, d), BF16),
        compiler_params=_params("arbitrary"),
        name="expert_ffn",
    )(tile_expert, n_valid, xs, w1, w3, w2)


def _combine_ln_kernel(x_ref, y0_ref, y1_ref, fields_ref, g_ref, b_ref, o_ref):
    fields = fields_ref[...]
    ffn = fields[:, 2:3] * y0_ref[...].astype(F32) + fields[:, 3:4] * y1_ref[...].astype(F32)
    o_ref[...] = _layer_norm_rows(DEEPNORM_ALPHA * x_ref[...] + ffn, g_ref[...], b_ref[...])


def _combine_ln(x2d, y0, y1, fields, ln_g, ln_b, tm=512):
    t, d = x2d.shape
    row = pl.BlockSpec((tm, d), lambda i: (i, 0))
    vec = pl.BlockSpec((1, d), lambda i: (0, 0))
    return pl.pallas_call(
        _combine_ln_kernel,
        grid=(t // tm,),
        in_specs=[row, row, row, pl.BlockSpec((tm, LANES), lambda i: (i, 0)), vec, vec],
        out_specs=row,
        out_shape=jax.ShapeDtypeStruct((t, d), F32),
        compiler_params=_params("parallel"),
        name="combine_ln",
    )(x2d, y0, y1, fields, ln_g.reshape(1, d), ln_b.reshape(1, d))


def _dispatch_plan(info, counts_f):
    t = info.shape[1]
    a = t * TOPK_IN_GROUP
    n_fill = N_EXPERTS * MOE_TILE
    n_tiles = a // MOE_TILE + N_EXPERTS
    experts = jnp.arange(N_EXPERTS, dtype=jnp.int32)
    counts = counts_f[0, :N_EXPERTS].astype(jnp.int32)
    padded = (counts + MOE_TILE - 1) // MOE_TILE * MOE_TILE
    pends = jnp.cumsum(padded)
    fill_ends = jnp.cumsum(padded - counts)
    fill = jnp.arange(n_fill, dtype=jnp.int32)
    fill_key = jnp.sum((fill_ends[None, :] <= fill[:, None]).astype(jnp.int32), axis=1)
    tok = jnp.arange(t, dtype=jnp.int32)
    keys = jnp.concatenate([info[0].astype(jnp.int32), info[1].astype(jnp.int32), fill_key])
    toks = jnp.concatenate([tok, tok, fill % t])
    flat = jnp.arange(a + n_fill, dtype=jnp.int32)
    _, slot_tok, slot_flat = lax.sort((keys, toks, flat), num_keys=1, is_stable=True)
    _, slot_of = lax.sort((slot_flat, flat), num_keys=1)
    tile_start = jnp.arange(n_tiles, dtype=jnp.int32) * MOE_TILE
    tile_expert = jnp.sum((pends[None, :] <= tile_start[:, None]).astype(jnp.int32), axis=1)
    last_used = jnp.max(jnp.where(counts > 0, experts, 0))
    tile_expert = jnp.minimum(tile_expert, last_used)
    n_valid = (pends[-1] // MOE_TILE).astype(jnp.int32).reshape(1)
    return tile_expert, n_valid, slot_tok, slot_of[:t], slot_of[t:a]


def _moe_and_norm(x1, x1_bf16, fields, info, counts_f, w1, w3, w2, layer, ln_g, ln_b):
    tile_expert, n_valid, slot_tok, dest0, dest1 = _dispatch_plan(info, counts_f)
    yb = _expert_ffn(tile_expert, n_valid, x1_bf16[slot_tok], w1, w3, w2, layer)
    return _combine_ln(x1, yb[dest0], yb[dest1], fields, ln_g, ln_b)


def kernel(x, ab_w_in, ab_w_out, c_w_in, c_w_out, c_norm_g, hgrn_lb_logits, ln_g, ln_b,
           router_g_w, router_g_b, router_e_w, router_e_b, exp_w1, exp_w3, exp_w2):
    batch, seq, d = x.shape
    t = batch * seq
    tables = _rope_lane_tables(seq)
    bias_tab = jnp.asarray(_dilated_bias_table())
    lb_all = jnp.cumsum(jax.nn.softmax(hgrn_lb_logits.astype(F32), axis=0), axis=0)
    lb_all = lb_all - lb_all[0:1]
    n_pairs = (AB_WIDTH // 2) // LANES

    xc = x.reshape(t, d)
    for l in range(DEPTH):
        j = l // 2
        if l % 2 == 0:
            h = _proj_ab(xc, ab_w_in[j].astype(BF16), tables, seq).reshape(batch, seq, 3 * AB_WIDTH)
            o_parts = [_moba(h, n_pairs).reshape(t, AB_WIDTH // 2),
                       _dilated(h, bias_tab, n_pairs).reshape(t, AB_WIDTH // 2)]
            w_out = ab_w_out[j]
        else:
            q, lf, kk, v, g = _proj_c(xc, c_w_in[j].astype(BF16), lb_all[j])
            o_parts = [_hgrn(q, lf, kk, v, g, c_norm_g[j], batch, seq).reshape(t, d)]
            w_out = c_w_out[j]
        wr = jnp.zeros((d, LANES), F32)
        wr = wr.at[:, :N_GROUPS].set(router_g_w[l]).at[:, N_GROUPS:N_GROUPS + N_EXPERTS].set(router_e_w[l])
        rb = jnp.zeros((1, LANES), F32)
        rb = rb.at[0, :N_GROUPS].set(router_g_b[l]).at[0, N_GROUPS:N_GROUPS + N_EXPERTS].set(router_e_b[l])
        x1, x1_bf16, fields, info, counts_f = _outproj_ln(o_parts, xc, w_out.astype(BF16), ln_g[l, 0], ln_b[l, 0],
                                                          wr, rb)
        xc = _moe_and_norm(x1, x1_bf16, fields, info, counts_f, exp_w1, exp_w3, exp_w2, l,
                           ln_g[l, 1], ln_b[l, 1])
    return xc.reshape(batch, seq, d)
```

```python
import functools
import math

import numpy as np
import jax
import jax.numpy as jnp
from jax import lax
from jax.experimental import pallas as pl
from jax.experimental.pallas import tpu as pltpu

F32 = jnp.float32
BF16 = jnp.bfloat16

D_MODEL = 1024
DEPTH = 4
HEAD_DIM = 64
N_HEADS_A = 8
N_HEADS_B = 8
AB_WIDTH = (N_HEADS_A + N_HEADS_B) * HEAD_DIM
ROT_DIM = HEAD_DIM // 4
ROPE_THETA = 500000.0
MOBA_BLOCK = 256
MOBA_TOPK = 3
DILATED_BRANCHES = ((128, 1), (512, 4), (2048, 16))
HGRN_EXPAND = 128
N_HEADS_C = D_MODEL // HGRN_EXPAND
HGRN_CHUNK = 64
HGRN_SUB = 16
HGRN_UNROLL = 8
HGRN_FAST_LIMIT = 60.0
N_GROUPS = 4
EXPERTS_PER_GROUP = 8
N_EXPERTS = N_GROUPS * EXPERTS_PER_GROUP
TOPK_IN_GROUP = 2
D_EXPERT = D_MODEL // 2
DEEPNORM_ALPHA = (2.0 * DEPTH) ** 0.25
LN_EPS = 1e-5
RMS_EPS = 1e-6

LANES = 128
ATT_BLOCK = 256
MOBA_STEP = 4
MOBA_SEL_ROWS = 16
SHIFT_LANE = LANES - 2
ATT_FAST_LIMIT = 38.0
DIL_STEP = 3
NEG_BIG = -1e30
MOE_TILE = 512
MOE_SLAB = 512
VMEM_LIMIT = 56 * 1024 * 1024

_NT = (((1,), (1,)), ((), ()))
_TN = (((0,), (0,)), ((), ()))


def _params(*sem):
    return pltpu.CompilerParams(dimension_semantics=sem, vmem_limit_bytes=VMEM_LIMIT)


def _proj_ab_kernel(x_ref, w_ref, cos_ref, sa_ref, sb_ref, o_ref):
    xb = x_ref[...].astype(BF16)
    sec = AB_WIDTH // 2
    for c in range(6):
        acc = jnp.dot(xb, w_ref[:, c * sec:(c + 1) * sec], preferred_element_type=F32)
        if c % 3 == 2:
            o_ref[:, c * sec:(c + 1) * sec] = acc.astype(BF16)
            continue
        cosv = cos_ref[...]
        sa = sa_ref[...]
        sb = sb_ref[...]
        for k in range(sec // LANES):
            a = acc[:, k * LANES:(k + 1) * LANES]
            r = a * cosv + pltpu.roll(a, LANES - ROT_DIM // 2, 1) * sa + pltpu.roll(a, ROT_DIM // 2, 1) * sb
            if c % 3 == 0:
                r = r * (HEAD_DIM ** -0.5)
            o_ref[:, c * sec + k * LANES:c * sec + (k + 1) * LANES] = r.astype(BF16)


def _rope_lane_tables(seq):
    half = ROT_DIM // 2
    inv = ROPE_THETA ** (-jnp.arange(half, dtype=F32) / half)
    ang = jnp.arange(seq, dtype=F32)[:, None] * inv[None, :]
    cos, sin = jnp.cos(ang), jnp.sin(ang)
    ones = jnp.ones((seq, HEAD_DIM - ROT_DIM), F32)
    zeros = jnp.zeros((seq, HEAD_DIM - ROT_DIM), F32)
    zh = jnp.zeros((seq, half), F32)
    cos_t = jnp.concatenate([cos, cos, ones], axis=1)
    sa_t = jnp.concatenate([-sin, zh, zeros], axis=1)
    sb_t = jnp.concatenate([zh, sin, zeros], axis=1)
    rep = LANES // HEAD_DIM
    return tuple(jnp.tile(t, (1, rep)) for t in (cos_t, sa_t, sb_t))


def _proj_ab(x2d, w_bf16, tables, seq, tm=512):
    t, d = x2d.shape
    n = w_bf16.shape[1]
    nseq = seq // tm
    tab_spec = pl.BlockSpec((tm, LANES), lambda i: (i % nseq, 0))
    return pl.pallas_call(
        _proj_ab_kernel,
        grid=(t // tm,),
        in_specs=[pl.BlockSpec((tm, d), lambda i: (i, 0)),
                  pl.BlockSpec((d, n), lambda i: (0, 0)),
                  tab_spec, tab_spec, tab_spec],
        out_specs=pl.BlockSpec((tm, n), lambda i: (i, 0)),
        out_shape=jax.ShapeDtypeStruct((t, n), BF16),
        compiler_params=_params("parallel"),
        name="proj_ab",
    )(x2d, w_bf16, *tables)


def _softmax_step(parts, vaug, m_scr, l_scr, acc_scr):
    m_prev = m_scr[...]
    mx = functools.reduce(jnp.maximum, parts)
    m_new = jnp.maximum(m_prev, jnp.max(mx, axis=1, keepdims=True))
    alpha = jnp.exp(m_prev - m_new)
    p = jnp.concatenate([jnp.exp(part - m_new).astype(BF16) for part in parts], axis=1)
    pv = jnp.dot(p, vaug, preferred_element_type=F32)
    acc_scr[...] = alpha * acc_scr[...] + pv[:, :LANES]
    l_scr[...] = alpha * l_scr[...] + pv[:, LANES:]
    m_scr[...] = m_new


def _plain_step(parts, vaug, l_scr, acc_scr):
    p = jnp.concatenate([jnp.exp(part).astype(BF16) for part in parts], axis=1)
    pv = jnp.dot(p, vaug, preferred_element_type=F32)
    acc_scr[...] += pv[:, :LANES]
    l_scr[...] += pv[:, LANES:]


def _stacked_q(qaug_scr, q_start, bq):
    return jnp.concatenate([qaug_scr[0, pl.ds(q_start, bq), :], qaug_scr[1, pl.ds(q_start, bq), :]], axis=0)


def _lane_parts(s):
    return [s[:, c * LANES:(c + 1) * LANES] for c in range(s.shape[1] // LANES)]


def _head_mask(lane, h):
    return (lane >= h * HEAD_DIM) & (lane < (h + 1) * HEAD_DIM)


def _head_sq_norms(x, h):
    lane = lax.broadcasted_iota(jnp.int32, x.shape, 1)
    xf = x.astype(F32)
    sq = jnp.where(_head_mask(lane, h), xf * xf, 0.0).astype(BF16)
    return jnp.dot(sq, jnp.ones((LANES, LANES), BF16), preferred_element_type=F32)


def _score_shifts(q, k, flag_ref):
    bounds = [jnp.sqrt(jnp.max(_head_sq_norms(q, h), axis=0, keepdims=True)
                       * jnp.max(_head_sq_norms(k, h), axis=0, keepdims=True)) for h in range(2)]
    fast_ok = jnp.max(jnp.maximum(bounds[0], bounds[1])) <= ATT_FAST_LIMIT
    flag_ref[0] = jnp.where(fast_ok, 1, 0).astype(jnp.int32)
    return [jnp.where(fast_ok, -b, 0.0) for b in bounds]


def _softmax_init(m_scr, l_scr, acc_scr):
    m_scr[...] = jnp.full(m_scr.shape, -jnp.inf, F32)
    l_scr[...] = jnp.zeros(l_scr.shape, F32)
    acc_scr[...] = jnp.zeros(acc_scr.shape, F32)


def _softmax_finish(o_ref, l_scr, acc_scr):
    bq = acc_scr.shape[0] // 2
    lane = lax.broadcasted_iota(jnp.int32, (bq, LANES), 1)
    o0 = acc_scr[0:bq, :] / l_scr[0:bq, :]
    o1 = acc_scr[bq:, :] / l_scr[bq:, :]
    o_ref[0] = jnp.where(lane < HEAD_DIM, o0, o1).astype(o_ref.dtype)


def _moba_kernel(q_ref, k_ref, v_ref, o_ref, kaug_scr, vaug_scr, qaug_scr, flag_ref, m_scr, l_scr, acc_scr, *, seq):
    bq = ATT_BLOCK
    pad = (MOBA_STEP - 1) * bq
    i = pl.program_id(2)

    @pl.when(i == 0)
    def _per_sequence_setup():
        q = q_ref[0]
        k = k_ref[0]
        rows = lax.broadcasted_iota(jnp.int32, (MOBA_SEL_ROWS, seq), 0)
        cols = lax.broadcasted_iota(jnp.int32, (MOBA_SEL_ROWS, seq), 1)
        own = jnp.right_shift(cols, int(math.log2(MOBA_BLOCK)))
        ind = jnp.where(own == rows, 1.0, 0.0).astype(BF16)
        km = jnp.dot(ind, k, preferred_element_type=F32) * (1.0 / MOBA_BLOCK)
        km_hi = km.astype(BF16)
        km_lo = (km - km_hi.astype(F32)).astype(BF16)
        shifts = _score_shifts(q, k, flag_ref)
        blk_lane = lax.broadcasted_iota(jnp.int32, (bq, LANES), 1)
        kaug_scr[0:pad, 0:LANES] = jnp.zeros((pad, LANES), BF16)
        vaug_scr[0:pad, 0:LANES] = jnp.zeros((pad, LANES), BF16)
        for jb in range(-(MOBA_STEP - 1), seq // bq):
            tgt = jb if jb >= 0 else LANES - 1
            kaug_scr[pad + jb * bq:pad + (jb + 1) * bq, LANES:] = jnp.where(
                (blk_lane == tgt) | (blk_lane == SHIFT_LANE), 1.0, 0.0).astype(BF16)
        kaug_scr[pad:, 0:LANES] = k
        vaug_scr[pad:, 0:LANES] = v_ref[0]
        vaug_scr[:, LANES:] = jnp.ones((seq + pad, LANES), BF16)
        lane_all = lax.broadcasted_iota(jnp.int32, (seq, LANES), 1)
        rows_f = rows.astype(F32)
        for h in range(2):
            qh = jnp.where(_head_mask(lane_all, h), q, jnp.zeros_like(q))
            g = (lax.dot_general(km_hi, qh, _NT, preferred_element_type=F32)
                 + lax.dot_general(km_lo, qh, _NT, preferred_element_type=F32))
            g = jnp.where(rows < own, g, -jnp.inf)
            bias = jnp.where(rows == own, 0.0, NEG_BIG)
            for _ in range(MOBA_TOPK):
                mx = jnp.max(g, axis=0, keepdims=True)
                first = jnp.min(jnp.where(g == mx, rows_f, float(MOBA_SEL_ROWS)), axis=0, keepdims=True)
                pick = (rows_f == first) & (mx > -jnp.inf)
                bias = jnp.where(pick, 0.0, bias)
                g = jnp.where(pick, -jnp.inf, g)
            bias = jnp.concatenate([bias, jnp.full((LANES - MOBA_SEL_ROWS, seq), NEG_BIG, F32)], axis=0).T
            qaug_scr[h, :, 0:LANES] = qh
            qaug_scr[h, :, LANES:] = jnp.where(lane_all == SHIFT_LANE, shifts[h], bias).astype(BF16)

    lane = lax.broadcasted_iota(jnp.int32, (2 * bq, LANES), 1)
    _softmax_init(m_scr, l_scr, acc_scr)
    n_parts = MOBA_STEP * bq // LANES
    own_parts = bq // LANES
    qi = lax.broadcasted_iota(jnp.int32, (2 * bq, LANES), 0) & (bq - 1)
    q_start = pl.multiple_of(i * bq, bq)

    def step(t, first, fast):
        start = pl.multiple_of((i - MOBA_STEP * t) * bq, bq)
        kaug = kaug_scr[pl.ds(start, MOBA_STEP * bq), :]
        vaug = vaug_scr[pl.ds(start, MOBA_STEP * bq), :]
        parts = _lane_parts(lax.dot_general(_stacked_q(qaug_scr, q_start, bq), kaug, _NT,
                                            preferred_element_type=F32))
        if first:
            for c in range(own_parts):
                cc = n_parts - own_parts + c
                parts[cc] = jnp.where(lane + c * LANES <= qi, parts[cc], NEG_BIG)
        if fast:
            _plain_step(parts, vaug, l_scr, acc_scr)
        else:
            _softmax_step(parts, vaug, m_scr, l_scr, acc_scr)

    def run(fast):
        step(0, True, fast)

        def later(t, carry):
            step(t, False, fast)
            return carry

        lax.fori_loop(1, (i + MOBA_STEP) // MOBA_STEP, later, 0)

    pl.when(flag_ref[0] == 1)(functools.partial(run, True))
    pl.when(flag_ref[0] != 1)(functools.partial(run, False))
    _softmax_finish(o_ref, l_scr, acc_scr)


def _moba(h3, n_pairs):
    b, seq, _ = h3.shape
    sec_blocks = (AB_WIDTH // 2) // LANES
    nq = seq // ATT_BLOCK
    assert seq % ATT_BLOCK == 0 and nq <= MOBA_SEL_ROWS
    return pl.pallas_call(
        functools.partial(_moba_kernel, seq=seq),
        grid=(b, n_pairs, nq),
        in_specs=[pl.BlockSpec((1, seq, LANES), lambda bb, p, i: (bb, 0, p)),
                  pl.BlockSpec((1, seq, LANES), lambda bb, p, i: (bb, 0, sec_blocks + p)),
                  pl.BlockSpec((1, seq, LANES), lambda bb, p, i: (bb, 0, 2 * sec_blocks + p))],
        out_specs=pl.BlockSpec((1, ATT_BLOCK, LANES), lambda bb, p, i: (bb, i, p)),
        out_shape=jax.ShapeDtypeStruct((b, seq, n_pairs * LANES), BF16),
        scratch_shapes=[pltpu.VMEM((seq + (MOBA_STEP - 1) * ATT_BLOCK, 2 * LANES), BF16),
                        pltpu.VMEM((seq + (MOBA_STEP - 1) * ATT_BLOCK, 2 * LANES), BF16),
                        pltpu.VMEM((2, seq, 2 * LANES), BF16),
                        pltpu.SMEM((1,), jnp.int32),
                        pltpu.VMEM((2 * ATT_BLOCK, LANES), F32),
                        pltpu.VMEM((2 * ATT_BLOCK, LANES), F32),
                        pltpu.VMEM((2 * ATT_BLOCK, LANES), F32)],
        compiler_params=_params("parallel", "parallel", "arbitrary"),
        name="moba",
    )(h3, h3, h3)


def _dilated_n_blocks():
    return max(w for w, _ in DILATED_BRANCHES) // ATT_BLOCK + 1


def _dilated_bias_table():
    n_steps = -(-_dilated_n_blocks() // DIL_STEP)
    qi = np.arange(ATT_BLOCK)[:, None]
    ki = np.arange(ATT_BLOCK)[None, :]
    tabs = []
    for t in range(n_steps):
        groups = []
        for g in range(DIL_STEP):
            d = qi - ki + (DIL_STEP * t + DIL_STEP - 1 - g) * ATT_BLOCK
            cnt = np.zeros(d.shape, np.int64)
            for window, dil in DILATED_BRANCHES:
                cnt += ((d >= 0) & (d <= window) & (d % dil == 0)).astype(np.int64)
            groups.append(np.where(cnt > 0, np.log(np.maximum(cnt, 1).astype(np.float64)), NEG_BIG))
        tabs.append(np.concatenate(groups, axis=1))
    return np.stack(tabs).astype(np.float32)


def _dilated_kernel(q_ref, k_ref, v_ref, bias_ref, o_ref, kaug_scr, vaug_scr, qaug_scr, flag_ref, m_scr, l_scr, acc_scr,
                    *, seq):
    bq = ATT_BLOCK
    pad = (DIL_STEP - 1) * bq
    i = pl.program_id(2)

    @pl.when(i == 0)
    def _per_sequence_setup():
        q = q_ref[0]
        k = k_ref[0]
        shifts = _score_shifts(q, k, flag_ref)
        flag_lane = lax.broadcasted_iota(jnp.int32, (pad, LANES), 1)
        lane_all = lax.broadcasted_iota(jnp.int32, (seq, LANES), 1)
        kaug_scr[0:pad, 0:LANES] = jnp.zeros((pad, LANES), BF16)
        kaug_scr[0:pad, LANES:] = jnp.where((flag_lane == 0) | (flag_lane == SHIFT_LANE), 1.0, 0.0).astype(BF16)
        kaug_scr[pad:, 0:LANES] = k
        kaug_scr[pad:, LANES:] = jnp.where(lane_all == SHIFT_LANE, 1.0, 0.0).astype(BF16)
        vaug_scr[0:pad, 0:LANES] = jnp.zeros((pad, LANES), BF16)
        vaug_scr[pad:, 0:LANES] = v_ref[0]
        vaug_scr[:, LANES:] = jnp.ones((seq + pad, LANES), BF16)
        for h in range(2):
            aug = jnp.where(lane_all == SHIFT_LANE, shifts[h], jnp.where(lane_all == 0, NEG_BIG, 0.0))
            qaug_scr[h, :, 0:LANES] = jnp.where(_head_mask(lane_all, h), q, jnp.zeros_like(q))
            qaug_scr[h, :, LANES:] = aug.astype(BF16)

    _softmax_init(m_scr, l_scr, acc_scr)
    q_start = pl.multiple_of(i * bq, bq)

    def step(t, fast):
        start = pl.multiple_of((i - DIL_STEP * t) * bq, bq)
        kaug = kaug_scr[pl.ds(start, DIL_STEP * bq), :]
        vaug = vaug_scr[pl.ds(start, DIL_STEP * bq), :]
        bias = bias_ref[t]
        s = lax.dot_general(_stacked_q(qaug_scr, q_start, bq), kaug, _NT, preferred_element_type=F32)
        parts = _lane_parts(jnp.concatenate([s[0:bq, :] + bias, s[bq:, :] + bias], axis=0))
        if fast:
            _plain_step(parts, vaug, l_scr, acc_scr)
        else:
            _softmax_step(parts, vaug, m_scr, l_scr, acc_scr)

    n_steps = (jnp.minimum(i, _dilated_n_blocks() - 1) + DIL_STEP) // DIL_STEP

    def run(fast):
        def body(t, carry):
            step(t, fast)
            return carry

        lax.fori_loop(0, n_steps, body, 0)

    pl.when(flag_ref[0] == 1)(functools.partial(run, True))
    pl.when(flag_ref[0] != 1)(functools.partial(run, False))
    _softmax_finish(o_ref, l_scr, acc_scr)


def _dilated(h3, bias_tab, n_pairs):
    b, seq, _ = h3.shape
    sec_blocks = (AB_WIDTH // 2) // LANES
    base = 3 * sec_blocks
    nq = seq // ATT_BLOCK
    pad = (DIL_STEP - 1) * ATT_BLOCK
    return pl.pallas_call(
        functools.partial(_dilated_kernel, seq=seq),
        grid=(b, n_pairs, nq),
        in_specs=[pl.BlockSpec((1, seq, LANES), lambda bb, p, i: (bb, 0, base + p)),
                  pl.BlockSpec((1, seq, LANES), lambda bb, p, i: (bb, 0, base + sec_blocks + p)),
                  pl.BlockSpec((1, seq, LANES), lambda bb, p, i: (bb, 0, base + 2 * sec_blocks + p)),
                  pl.BlockSpec(bias_tab.shape, lambda bb, p, i: (0, 0, 0))],
        out_specs=pl.BlockSpec((1, ATT_BLOCK, LANES), lambda bb, p, i: (bb, i, p)),
        out_shape=jax.ShapeDtypeStruct((b, seq, n_pairs * LANES), BF16),
        scratch_shapes=[pltpu.VMEM((seq + pad, 2 * LANES), BF16),
                        pltpu.VMEM((seq + pad, 2 * LANES), BF16),
                        pltpu.VMEM((2, seq, 2 * LANES), BF16),
                        pltpu.SMEM((1,), jnp.int32),
                        pltpu.VMEM((2 * ATT_BLOCK, LANES), F32),
                        pltpu.VMEM((2 * ATT_BLOCK, LANES), F32),
                        pltpu.VMEM((2 * ATT_BLOCK, LANES), F32)],
        compiler_params=_params("parallel", "parallel", "arbitrary"),
        name="dilated",
    )(h3, h3, h3, bias_tab)


def _proj_c_kernel(x_ref, w_ref, loglb_ref, log1mlb_ref, omlb_ref, q_ref, lf_ref, kk_ref, v_ref, g_ref):
    xb = x_ref[...].astype(BF16)
    d = D_MODEL

    def sec(c):
        return jnp.dot(xb, w_ref[:, c * d:(c + 1) * d], preferred_element_type=F32)

    q_ref[...] = sec(0)
    z = sec(1)
    log_sig = jnp.minimum(z, 0.0) - jnp.log1p(jnp.exp(-jnp.abs(z)))
    a = loglb_ref[...]
    c = log1mlb_ref[...] + log_sig
    lf_ref[...] = jnp.maximum(a, c) + jnp.log1p(jnp.exp(-jnp.abs(a - c)))
    kk_ref[...] = omlb_ref[...] / (1.0 + jnp.exp(z))
    v_ref[...] = sec(2).astype(v_ref.dtype)
    g_ref[...] = sec(3)


def _proj_c(x2d, w_bf16, lb, tm=512):
    t, d = x2d.shape
    n = w_bf16.shape[1]
    lb = lb.astype(F32).reshape(1, d)
    vec_spec = pl.BlockSpec((1, d), lambda i: (0, 0))
    out_spec = pl.BlockSpec((tm, d), lambda i: (i, 0))
    sds = jax.ShapeDtypeStruct((t, d), F32)
    return pl.pallas_call(
        _proj_c_kernel,
        grid=(t // tm,),
        in_specs=[pl.BlockSpec((tm, d), lambda i: (i, 0)),
                  pl.BlockSpec((d, n), lambda i: (0, 0)),
                  vec_spec, vec_spec, vec_spec],
        out_specs=[out_spec] * 5,
        out_shape=[sds, sds, sds, jax.ShapeDtypeStruct((t, d), BF16), sds],
        compiler_params=_params("parallel"),
        name="proj_c",
    )(x2d, w_bf16, jnp.log(lb), jnp.log1p(-lb), 1.0 - lb)


def _split3_bf16(x):
    h1 = x.astype(BF16)
    r1 = x - h1.astype(F32)
    h2 = r1.astype(BF16)
    h3 = (r1 - h2.astype(F32)).astype(BF16)
    return h1, h2, h3


def _hgrn_kernel(q_ref, lf_ref, kk_ref, v_ref, g_ref, ng_ref, o_ref, *, seq):
    C, SUB = HGRN_CHUNK, HGRN_SUB
    n_sub = C // SUB
    ri = lax.broadcasted_iota(jnp.int32, (C, C), 0)
    ci = lax.broadcasted_iota(jnp.int32, (C, C), 1)
    tri = jnp.where(ci <= ri, 1.0, 0.0).astype(BF16)
    row_c = lax.broadcasted_iota(jnp.int32, (C, LANES), 0)
    row_s = lax.broadcasted_iota(jnp.int32, (SUB, C), 0)
    lane_s = lax.broadcasted_iota(jnp.int32, (SUB, C), 1)
    ng = ng_ref[...]

    def scores_factored(qc, kc, b):
        refs = [jnp.zeros((1, LANES), F32)] + [b[s * SUB - 1:s * SUB, :] for s in range(1, n_sub)]
        ref_rows = jnp.concatenate([jnp.broadcast_to(r, (SUB, LANES)) for r in refs], axis=0)
        qt = (qc * jnp.exp(b - ref_rows)).astype(BF16)
        rows_a = []
        for s in range(n_sub):
            hi = (s + 1) * SUB
            kt = (kc[:hi, :] * jnp.exp(refs[s] - b[:hi, :])).astype(BF16)
            if hi < C:
                kt = jnp.concatenate([kt, jnp.zeros((C - hi, LANES), BF16)], axis=0)
            rows_a.append(lax.dot_general(qt[s * SUB:hi, :], kt, _NT, preferred_element_type=F32))
        return jnp.where(ci <= ri, jnp.concatenate(rows_a, axis=0), 0.0)

    def scores_direct(qc, kc, b):
        rows_a = []
        for sidx in range(n_sub):
            lo = sidx * SUB
            q_i = qc[lo:lo + SUB, :]
            b_i = b[lo:lo + SUB, :]
            if sidx == 0:
                a_blk = jnp.zeros((SUB, C), F32)
            else:
                ref = b[lo - 1:lo, :]
                qt = (q_i * jnp.exp(b_i - ref)).astype(BF16)
                kt = jnp.where(row_c < lo, kc * jnp.exp(jnp.minimum(ref - b, 0.0)), 0.0).astype(BF16)
                a_blk = lax.dot_general(qt, kt, _NT, preferred_element_type=F32)
            for j in range(SUB):
                r_lo = 0 if j < 8 else 8
                bj = b[lo + j:lo + j + 1, :]
                kj = kc[lo + j:lo + j + 1, :]
                pj = q_i[r_lo:, :] * (jnp.exp(jnp.minimum(b_i[r_lo:, :] - bj, 0.0)) * kj)
                col = jnp.sum(pj, axis=1, keepdims=True)
                if r_lo:
                    col = jnp.concatenate([jnp.zeros((r_lo, 1), F32), col], axis=0)
                a_blk = jnp.where((lane_s == lo + j) & (row_s >= j), col, a_blk)
            rows_a.append(a_blk)
        return jnp.concatenate(rows_a, axis=0)

    def make_chunk(scores):
        def chunk(c, st):
            r0 = pl.multiple_of(c * C, C)
            qc = q_ref[0, pl.ds(r0, C), :]
            kc = kk_ref[0, pl.ds(r0, C), :]
            vb = v_ref[0, pl.ds(r0, C), :].astype(BF16)
            l1, l2, l3 = _split3_bf16(lf_ref[0, pl.ds(r0, C), :])
            b = (jnp.dot(tri, l1, preferred_element_type=F32)
                 + jnp.dot(tri, l2, preferred_element_type=F32)
                 + jnp.dot(tri, l3, preferred_element_type=F32))
            b_last = b[C - 1:C, :]
            inter = lax.dot_general((qc * jnp.exp(b)).astype(BF16), st.astype(BF16), _NT,
                                    preferred_element_type=F32)
            o = inter + jnp.dot(scores(qc, kc, b).astype(BF16), vb, preferred_element_type=F32)
            o = o * lax.rsqrt(jnp.mean(o * o, axis=1, keepdims=True) + RMS_EPS) * ng
            gc = g_ref[0, pl.ds(r0, C), :]
            o_ref[0, pl.ds(r0, C), :] = (o * (gc / (1.0 + jnp.exp(-gc)))).astype(o_ref.dtype)
            kd = (kc * jnp.exp(b_last - b)).astype(BF16)
            return st * jnp.exp(b_last) + lax.dot_general(vb, kd, _TN, preferred_element_type=F32)
        return chunk

    sub_decay = jnp.sum(lf_ref[0].reshape(seq // SUB, SUB, LANES), axis=1)
    fast_ok = jnp.min(sub_decay) >= -HGRN_FAST_LIMIT
    st0 = jnp.zeros((LANES, LANES), F32)

    @pl.when(fast_ok)
    def _():
        lax.fori_loop(0, seq // C, make_chunk(scores_factored), st0, unroll=HGRN_UNROLL)

    @pl.when(jnp.logical_not(fast_ok))
    def _():
        lax.fori_loop(0, seq // C, make_chunk(scores_direct), st0)


def _hgrn(q, lf, kk, v, g, norm_g, batch, seq):
    shp = (batch, seq, D_MODEL)
    args = [a.reshape(shp) for a in (q, lf, kk, v, g)]
    spec = pl.BlockSpec((1, seq, LANES), lambda bb, h: (bb, 0, h))
    return pl.pallas_call(
        functools.partial(_hgrn_kernel, seq=seq),
        grid=(batch, N_HEADS_C),
        in_specs=[spec] * 5 + [pl.BlockSpec((1, LANES), lambda bb, h: (0, 0))],
        out_specs=spec,
        out_shape=jax.ShapeDtypeStruct(shp, BF16),
        compiler_params=_params("parallel", "parallel"),
        name="hgrn",
    )(*args, norm_g.astype(F32).reshape(1, LANES))


def _layer_norm_rows(z, g, b):
    mu = jnp.mean(z, axis=1, keepdims=True)
    zc = z - mu
    var = jnp.mean(zc * zc, axis=1, keepdims=True)
    return zc * lax.rsqrt(var + LN_EPS) * g + b


def _first_lane_of_max(vals, lane_f):
    mx = jnp.max(vals, axis=1, keepdims=True)
    return mx, jnp.min(jnp.where(vals == mx, lane_f, float(LANES)), axis=1, keepdims=True)


def _outproj_ln_kernel(*refs, n_parts):
    o_refs, w_refs = refs[:n_parts], refs[n_parts:2 * n_parts]
    (x_ref, g_ref, b_ref, wr_hi_ref, wr_lo_ref, rb_ref,
     y_ref, yb_ref, fields_ref, info_ref, cnt_ref) = refs[2 * n_parts:]
    mix = jnp.dot(o_refs[0][...], w_refs[0][...], preferred_element_type=F32)
    for o_ref, w_ref in zip(o_refs[1:], w_refs[1:]):
        mix += jnp.dot(o_ref[...], w_ref[...], preferred_element_type=F32)
    y = _layer_norm_rows(DEEPNORM_ALPHA * x_ref[...] + mix, g_ref[...], b_ref[...])
    y_ref[...] = y
    y_hi = y.astype(BF16)
    yb_ref[...] = y_hi
    y_lo = (y - y_hi.astype(F32)).astype(BF16)
    lg = (jnp.dot(y_hi, wr_hi_ref[...], preferred_element_type=F32)
          + jnp.dot(y_lo, wr_hi_ref[...], preferred_element_type=F32)
          + jnp.dot(y_hi, wr_lo_ref[...], preferred_element_type=F32)
          + rb_ref[...])
    lane = lax.broadcasted_iota(jnp.int32, lg.shape, 1)
    lane_f = lane.astype(F32)
    is_g = lane < N_GROUPS
    mg, grp = _first_lane_of_max(jnp.where(is_g, lg, -jnp.inf), lane_f)
    pg = 1.0 / jnp.sum(jnp.where(is_g, jnp.exp(lg - mg), 0.0), axis=1, keepdims=True)
    lo = float(N_GROUPS) + float(EXPERTS_PER_GROUP) * grp
    le = jnp.where((lane_f >= lo) & (lane_f < lo + float(EXPERTS_PER_GROUP)), lg, -jnp.inf)
    m1, i1 = _first_lane_of_max(le, lane_f)
    m2, i2 = _first_lane_of_max(jnp.where(lane_f == i1, -jnp.inf, le), lane_f)
    e21 = jnp.exp(m2 - m1)
    w0 = pg / (1.0 + e21)
    w1 = pg * e21 / (1.0 + e21)
    eid0 = i1 - float(N_GROUPS)
    eid1 = i2 - float(N_GROUPS)

    @pl.when(pl.program_id(0) == 0)
    def _():
        cnt_ref[...] = jnp.zeros(cnt_ref.shape, F32)

    hits = jnp.where(lane_f == eid0, 1.0, 0.0) + jnp.where(lane_f == eid1, 1.0, 0.0)
    cnt_ref[...] += jnp.sum(hits, axis=0, keepdims=True)
    fields = jnp.where(lane == 0, eid0, jnp.where(lane == 1, eid1, jnp.where(lane == 2, w0,
                       jnp.where(lane == 3, w1, 0.0))))
    fields_ref[...] = fields
    info_ref[...] = fields.T[:8, :]


def _outproj_ln(o_parts, x2d, w_bf16, ln_g, ln_b, wr, rb, tm=512):
    t, d = x2d.shape
    wr_hi = wr.astype(BF16)
    wr_lo = (wr - wr_hi.astype(F32)).astype(BF16)
    vec = lambda n: pl.BlockSpec((1, n), lambda i: (0, 0))
    widths = [o.shape[1] for o in o_parts]
    offs = np.cumsum([0] + widths)
    w_parts = [w_bf16[offs[k]:offs[k + 1]] for k in range(len(widths))]
    return pl.pallas_call(
        functools.partial(_outproj_ln_kernel, n_parts=len(widths)),
        grid=(t // tm,),
        in_specs=[pl.BlockSpec((tm, wd), lambda i: (i, 0)) for wd in widths]
                 + [pl.BlockSpec((wd, d), lambda i: (0, 0)) for wd in widths]
                 + [pl.BlockSpec((tm, d), lambda i: (i, 0)),
                  vec(d), vec(d),
                  pl.BlockSpec((d, LANES), lambda i: (0, 0)),
                  pl.BlockSpec((d, LANES), lambda i: (0, 0)),
                  vec(LANES)],
        out_specs=[pl.BlockSpec((tm, d), lambda i: (i, 0)),
                   pl.BlockSpec((tm, d), lambda i: (i, 0)),
                   pl.BlockSpec((tm, LANES), lambda i: (i, 0)),
                   pl.BlockSpec((8, tm), lambda i: (0, i)),
                   vec(LANES)],
        out_shape=[jax.ShapeDtypeStruct((t, d), F32),
                   jax.ShapeDtypeStruct((t, d), BF16),
                   jax.ShapeDtypeStruct((t, LANES), F32),
                   jax.ShapeDtypeStruct((8, t), F32),
                   jax.ShapeDtypeStruct((1, LANES), F32)],
        compiler_params=_params("arbitrary"),
        name="outproj_ln",
    )(*o_parts, *w_parts, x2d, ln_g.reshape(1, d), ln_b.reshape(1, d), wr_hi, wr_lo, rb)


def _expert_kernel(te_ref, nv_ref, xs_ref, w1_ref, w3_ref, w2_ref, o_ref, w1_s, w3_s, w2_s):
    i = pl.program_id(0)
    prev = te_ref[jnp.maximum(i - 1, 0)]

    @pl.when((i == 0) | (te_ref[i] != prev))
    def _cast_weights():
        w1_s[...] = w1_ref[0, 0].astype(BF16)
        w3_s[...] = w3_ref[0, 0].astype(BF16)
        w2_s[...] = w2_ref[0, 0].astype(BF16)

    @pl.when(i < nv_ref[0])
    def _ffn():
        for r in range(0, MOE_TILE, MOE_SLAB):
            xb = xs_ref[r:r + MOE_SLAB, :]
            a = jnp.dot(xb, w1_s[...], preferred_element_type=F32)
            u = jnp.dot(xb, w3_s[...], preferred_element_type=F32)
            hb = (a / (1.0 + jnp.exp(-a))) * u
            o_ref[r:r + MOE_SLAB, :] = jnp.dot(hb.astype(BF16), w2_s[...],
                                               preferred_element_type=F32).astype(o_ref.dtype)

    @pl.when(i >= nv_ref[0])
    def _pad():
        o_ref[...] = jnp.zeros(o_ref.shape, o_ref.dtype)


def _expert_ffn(tile_expert, n_valid, xs, w1, w3, w2, layer):
    p, d = xs.shape
    f = w1.shape[3]
    n_tiles = p // MOE_TILE
    grid_spec = pltpu.PrefetchScalarGridSpec(
        num_scalar_prefetch=2,
        grid=(n_tiles,),
        in_specs=[pl.BlockSpec((MOE_TILE, d), lambda i, te, nv: (i, 0)),
                  pl.BlockSpec((1, 1, d, f), lambda i, te, nv: (layer, te[i], 0, 0)),
                  pl.BlockSpec((1, 1, d, f), lambda i, te, nv: (layer, te[i], 0, 0)),
                  pl.BlockSpec((1, 1, f, d), lambda i, te, nv: (layer, te[i], 0, 0))],
        out_specs=pl.BlockSpec((MOE_TILE, d), lambda i, te, nv: (i, 0)),
        scratch_shapes=[pltpu.VMEM((d, f), BF16), pltpu.VMEM((d, f), BF16), pltpu.VMEM((f, d), BF16)],
    )
    return pl.pallas_call(
        _expert_kernel,
        grid_spec=grid_spec,
        out_shape=jax.ShapeDtypeStruct((p, d), BF16),
        compiler_params=_params("arbitrary"),
        name="expert_ffn",
    )(tile_expert, n_valid, xs, w1, w3, w2)


def _combine_ln_kernel(x_ref, y0_ref, y1_ref, fields_ref, g_ref, b_ref, o_ref):
    fields = fields_ref[...]
    ffn = fields[:, 2:3] * y0_ref[...].astype(F32) + fields[:, 3:4] * y1_ref[...].astype(F32)
    o_ref[...] = _layer_norm_rows(DEEPNORM_ALPHA * x_ref[...] + ffn, g_ref[...], b_ref[...])


def _combine_ln(x2d, y0, y1, fields, ln_g, ln_b, tm=512):
    t, d = x2d.shape
    row = pl.BlockSpec((tm, d), lambda i: (i, 0))
    vec = pl.BlockSpec((1, d), lambda i: (0, 0))
    return pl.pallas_call(
        _combine_ln_kernel,
        grid=(t // tm,),
        in_specs=[row, row, row, pl.BlockSpec((tm, LANES), lambda i: (i, 0)), vec, vec],
        out_specs=row,
        out_shape=jax.ShapeDtypeStruct((t, d), F32),
        compiler_params=_params("parallel"),
        name="combine_ln",
    )(x2d, y0, y1, fields, ln_g.reshape(1, d), ln_b.reshape(1, d))


def _dispatch_plan(info, counts_f):
    t = info.shape[1]
    a = t * TOPK_IN_GROUP
    n_fill = N_EXPERTS * MOE_TILE
    n_tiles = a // MOE_TILE + N_EXPERTS
    experts = jnp.arange(N_EXPERTS, dtype=jnp.int32)
    counts = counts_f[0, :N_EXPERTS].astype(jnp.int32)
    padded = (counts + MOE_TILE - 1) // MOE_TILE * MOE_TILE
    pends = jnp.cumsum(padded)
    fill_ends = jnp.cumsum(padded - counts)
    fill = jnp.arange(n_fill, dtype=jnp.int32)
    fill_key = jnp.sum((fill_ends[None, :] <= fill[:, None]).astype(jnp.int32), axis=1)
    tok = jnp.arange(t, dtype=jnp.int32)
    keys = jnp.concatenate([info[0].astype(jnp.int32), info[1].astype(jnp.int32), fill_key])
    toks = jnp.concatenate([tok, tok, fill % t])
    flat = jnp.arange(a + n_fill, dtype=jnp.int32)
    _, slot_tok, slot_flat = lax.sort((keys, toks, flat), num_keys=1, is_stable=True)
    _, slot_of = lax.sort((slot_flat, flat), num_keys=1)
    tile_start = jnp.arange(n_tiles, dtype=jnp.int32) * MOE_TILE
    tile_expert = jnp.sum((pends[None, :] <= tile_start[:, None]).astype(jnp.int32), axis=1)
    last_used = jnp.max(jnp.where(counts > 0, experts, 0))
    tile_expert = jnp.minimum(tile_expert, last_used)
    n_valid = (pends[-1] // MOE_TILE).astype(jnp.int32).reshape(1)
    return tile_expert, n_valid, slot_tok, slot_of[:t], slot_of[t:a]


def _moe_and_norm(x1, x1_bf16, fields, info, counts_f, w1, w3, w2, layer, ln_g, ln_b):
    tile_expert, n_valid, slot_tok, dest0, dest1 = _dispatch_plan(info, counts_f)
    yb = _expert_ffn(tile_expert, n_valid, x1_bf16[slot_tok], w1, w3, w2, layer)
    return _combine_ln(x1, yb[dest0], yb[dest1], fields, ln_g, ln_b)


def kernel(x, ab_w_in, ab_w_out, c_w_in, c_w_out, c_norm_g, hgrn_lb_logits, ln_g, ln_b,
           router_g_w, router_g_b, router_e_w, router_e_b, exp_w1, exp_w3, exp_w2):
    batch, seq, d = x.shape
    t = batch * seq
    tables = _rope_lane_tables(seq)
    bias_tab = jnp.asarray(_dilated_bias_table())
    lb_all = jnp.cumsum(jax.nn.softmax(hgrn_lb_logits.astype(F32), axis=0), axis=0)
    lb_all = lb_all - lb_all[0:1]
    n_pairs = (AB_WIDTH // 2) // LANES

    xc = x.reshape(t, d)
    for l in range(DEPTH):
        j = l // 2
        if l % 2 == 0:
            h = _proj_ab(xc, ab_w_in[j].astype(BF16), tables, seq).reshape(batch, seq, 3 * AB_WIDTH)
            o_parts = [_moba(h, n_pairs).reshape(t, AB_WIDTH // 2),
                       _dilated(h, bias_tab, n_pairs).reshape(t, AB_WIDTH // 2)]
            w_out = ab_w_out[j]
        else:
            q, lf, kk, v, g = _proj_c(xc, c_w_in[j].astype(BF16), lb_all[j])
            o_parts = [_hgrn(q, lf, kk, v, g, c_norm_g[j], batch, seq).reshape(t, d)]
            w_out = c_w_out[j]
        wr = jnp.zeros((d, LANES), F32)
        wr = wr.at[:, :N_GROUPS].set(router_g_w[l]).at[:, N_GROUPS:N_GROUPS + N_EXPERTS].set(router_e_w[l])
        rb = jnp.zeros((1, LANES), F32)
        rb = rb.at[0, :N_GROUPS].set(router_g_b[l]).at[0, N_GROUPS:N_GROUPS + N_EXPERTS].set(router_e_b[l])
        x1, x1_bf16, fields, info, counts_f = _outproj_ln(o_parts, xc, w_out.astype(BF16), ln_g[l, 0], ln_b[l, 0],
                                                          wr, rb)
        xc = _moe_and_norm(x1, x1_bf16, fields, info, counts_f, exp_w1, exp_w3, exp_w2, l,
                           ln_g[l, 1], ln_b[l, 1])
    return xc.reshape(batch, seq, d)
```

```python
import functools
import math

import numpy as np
import jax
import jax.numpy as jnp
from jax import lax
from jax.experimental import pallas as pl
from jax.experimental.pallas import tpu as pltpu

F32 = jnp.float32
BF16 = jnp.bfloat16

D_MODEL = 1024
DEPTH = 4
HEAD_DIM = 64
N_HEADS_A = 8
N_HEADS_B = 8
AB_WIDTH = (N_HEADS_A + N_HEADS_B) * HEAD_DIM
ROT_DIM = HEAD_DIM // 4
ROPE_THETA = 500000.0
MOBA_BLOCK = 256
MOBA_TOPK = 3
DILATED_BRANCHES = ((128, 1), (512, 4), (2048, 16))
HGRN_EXPAND = 128
N_HEADS_C = D_MODEL // HGRN_EXPAND
HGRN_CHUNK = 64
HGRN_SUB = 16
HGRN_BLOCK = 8
HGRN_FAST_LIMIT = 60.0
N_GROUPS = 4
EXPERTS_PER_GROUP = 8
N_EXPERTS = N_GROUPS * EXPERTS_PER_GROUP
TOPK_IN_GROUP = 2
D_EXPERT = D_MODEL // 2
DEEPNORM_ALPHA = (2.0 * DEPTH) ** 0.25
LN_EPS = 1e-5
RMS_EPS = 1e-6

LANES = 128
ATT_BLOCK = 256
MOBA_STEP = 4
MOBA_SEL_ROWS = 16
SHIFT_LANE = LANES - 2
ATT_FAST_LIMIT = 38.0
DIL_STEP = 3
NEG_BIG = -1e30
MOE_TILE = 512
MOE_SLAB = 512
VMEM_LIMIT = 56 * 1024 * 1024

_NT = (((1,), (1,)), ((), ()))
_TN = (((0,), (0,)), ((), ()))


def _params(*sem):
    return pltpu.CompilerParams(dimension_semantics=sem, vmem_limit_bytes=VMEM_LIMIT)


def _proj_ab_kernel(x_ref, w_ref, cos_ref, sa_ref, sb_ref, o_ref):
    xb = x_ref[...].astype(BF16)
    sec = AB_WIDTH // 2
    for c in range(6):
        acc = jnp.dot(xb, w_ref[:, c * sec:(c + 1) * sec], preferred_element_type=F32)
        if c % 3 == 2:
            o_ref[:, c * sec:(c + 1) * sec] = acc.astype(BF16)
            continue
        cosv = cos_ref[...]
        sa = sa_ref[...]
        sb = sb_ref[...]
        for k in range(sec // LANES):
            a = acc[:, k * LANES:(k + 1) * LANES]
            r = a * cosv + pltpu.roll(a, LANES - ROT_DIM // 2, 1) * sa + pltpu.roll(a, ROT_DIM // 2, 1) * sb
            if c % 3 == 0:
                r = r * (HEAD_DIM ** -0.5)
            o_ref[:, c * sec + k * LANES:c * sec + (k + 1) * LANES] = r.astype(BF16)


def _rope_lane_tables(seq):
    half = ROT_DIM // 2
    inv = ROPE_THETA ** (-jnp.arange(half, dtype=F32) / half)
    ang = jnp.arange(seq, dtype=F32)[:, None] * inv[None, :]
    cos, sin = jnp.cos(ang), jnp.sin(ang)
    ones = jnp.ones((seq, HEAD_DIM - ROT_DIM), F32)
    zeros = jnp.zeros((seq, HEAD_DIM - ROT_DIM), F32)
    zh = jnp.zeros((seq, half), F32)
    cos_t = jnp.concatenate([cos, cos, ones], axis=1)
    sa_t = jnp.concatenate([-sin, zh, zeros], axis=1)
    sb_t = jnp.concatenate([zh, sin, zeros], axis=1)
    rep = LANES // HEAD_DIM
    return tuple(jnp.tile(t, (1, rep)) for t in (cos_t, sa_t, sb_t))


def _proj_ab(x2d, w_bf16, tables, seq, tm=512):
    t, d = x2d.shape
    n = w_bf16.shape[1]
    nseq = seq // tm
    tab_spec = pl.BlockSpec((tm, LANES), lambda i: (i % nseq, 0))
    return pl.pallas_call(
        _proj_ab_kernel,
        grid=(t // tm,),
        in_specs=[pl.BlockSpec((tm, d), lambda i: (i, 0)),
                  pl.BlockSpec((d, n), lambda i: (0, 0)),
                  tab_spec, tab_spec, tab_spec],
        out_specs=pl.BlockSpec((tm, n), lambda i: (i, 0)),
        out_shape=jax.ShapeDtypeStruct((t, n), BF16),
        compiler_params=_params("parallel"),
        name="proj_ab",
    )(x2d, w_bf16, *tables)


def _softmax_step(parts, vaug, m_scr, l_scr, acc_scr):
    m_prev = m_scr[...]
    mx = functools.reduce(jnp.maximum, parts)
    m_new = jnp.maximum(m_prev, jnp.max(mx, axis=1, keepdims=True))
    alpha = jnp.exp(m_prev - m_new)
    p = jnp.concatenate([jnp.exp(part - m_new).astype(BF16) for part in parts], axis=1)
    pv = jnp.dot(p, vaug, preferred_element_type=F32)
    acc_scr[...] = alpha * acc_scr[...] + pv[:, :LANES]
    l_scr[...] = alpha * l_scr[...] + pv[:, LANES:]
    m_scr[...] = m_new


def _plain_step(parts, vaug, l_scr, acc_scr):
    p = jnp.concatenate([jnp.exp(part).astype(BF16) for part in parts], axis=1)
    pv = jnp.dot(p, vaug, preferred_element_type=F32)
    acc_scr[...] += pv[:, :LANES]
    l_scr[...] += pv[:, LANES:]


def _stacked_q(qaug_scr, q_start, bq):
    return jnp.concatenate([qaug_scr[0, pl.ds(q_start, bq), :], qaug_scr[1, pl.ds(q_start, bq), :]], axis=0)


def _lane_parts(s):
    return [s[:, c * LANES:(c + 1) * LANES] for c in range(s.shape[1] // LANES)]


def _head_mask(lane, h):
    return (lane >= h * HEAD_DIM) & (lane < (h + 1) * HEAD_DIM)


def _head_sq_norms(x, h):
    lane = lax.broadcasted_iota(jnp.int32, x.shape, 1)
    xf = x.astype(F32)
    sq = jnp.where(_head_mask(lane, h), xf * xf, 0.0).astype(BF16)
    return jnp.dot(sq, jnp.ones((LANES, LANES), BF16), preferred_element_type=F32)


def _score_shifts(q, k, flag_ref):
    bounds = [jnp.sqrt(jnp.max(_head_sq_norms(q, h), axis=0, keepdims=True)
                       * jnp.max(_head_sq_norms(k, h), axis=0, keepdims=True)) for h in range(2)]
    fast_ok = jnp.max(jnp.maximum(bounds[0], bounds[1])) <= ATT_FAST_LIMIT
    flag_ref[0] = jnp.where(fast_ok, 1, 0).astype(jnp.int32)
    return [jnp.where(fast_ok, -b, 0.0) for b in bounds]


def _softmax_init(m_scr, l_scr, acc_scr):
    m_scr[...] = jnp.full(m_scr.shape, -jnp.inf, F32)
    l_scr[...] = jnp.zeros(l_scr.shape, F32)
    acc_scr[...] = jnp.zeros(acc_scr.shape, F32)


def _softmax_finish(o_ref, l_scr, acc_scr):
    bq = acc_scr.shape[0] // 2
    lane = lax.broadcasted_iota(jnp.int32, (bq, LANES), 1)
    o0 = acc_scr[0:bq, :] / l_scr[0:bq, :]
    o1 = acc_scr[bq:, :] / l_scr[bq:, :]
    o_ref[0] = jnp.where(lane < HEAD_DIM, o0, o1).astype(o_ref.dtype)


def _moba_kernel(q_ref, k_ref, v_ref, o_ref, kaug_scr, vaug_scr, qaug_scr, flag_ref, m_scr, l_scr, acc_scr, *, seq):
    bq = ATT_BLOCK
    pad = (MOBA_STEP - 1) * bq
    i = pl.program_id(2)

    @pl.when(i == 0)
    def _per_sequence_setup():
        q = q_ref[0]
        k = k_ref[0]
        rows = lax.broadcasted_iota(jnp.int32, (MOBA_SEL_ROWS, seq), 0)
        cols = lax.broadcasted_iota(jnp.int32, (MOBA_SEL_ROWS, seq), 1)
        own = jnp.right_shift(cols, int(math.log2(MOBA_BLOCK)))
        ind = jnp.where(own == rows, 1.0, 0.0).astype(BF16)
        km = jnp.dot(ind, k, preferred_element_type=F32) * (1.0 / MOBA_BLOCK)
        km_hi = km.astype(BF16)
        km_lo = (km - km_hi.astype(F32)).astype(BF16)
        shifts = _score_shifts(q, k, flag_ref)
        blk_lane = lax.broadcasted_iota(jnp.int32, (bq, LANES), 1)
        kaug_scr[0:pad, 0:LANES] = jnp.zeros((pad, LANES), BF16)
        vaug_scr[0:pad, 0:LANES] = jnp.zeros((pad, LANES), BF16)
        for jb in range(-(MOBA_STEP - 1), seq // bq):
            tgt = jb if jb >= 0 else LANES - 1
            kaug_scr[pad + jb * bq:pad + (jb + 1) * bq, LANES:] = jnp.where(
                (blk_lane == tgt) | (blk_lane == SHIFT_LANE), 1.0, 0.0).astype(BF16)
        kaug_scr[pad:, 0:LANES] = k
        vaug_scr[pad:, 0:LANES] = v_ref[0]
        vaug_scr[:, LANES:] = jnp.ones((seq + pad, LANES), BF16)
        lane_all = lax.broadcasted_iota(jnp.int32, (seq, LANES), 1)
        rows_f = rows.astype(F32)
        for h in range(2):
            qh = jnp.where(_head_mask(lane_all, h), q, jnp.zeros_like(q))
            g = (lax.dot_general(km_hi, qh, _NT, preferred_element_type=F32)
                 + lax.dot_general(km_lo, qh, _NT, preferred_element_type=F32))
            g = jnp.where(rows < own, g, -jnp.inf)
            bias = jnp.where(rows == own, 0.0, NEG_BIG)
            for _ in range(MOBA_TOPK):
                mx = jnp.max(g, axis=0, keepdims=True)
                first = jnp.min(jnp.where(g == mx, rows_f, float(MOBA_SEL_ROWS)), axis=0, keepdims=True)
                pick = (rows_f == first) & (mx > -jnp.inf)
                bias = jnp.where(pick, 0.0, bias)
                g = jnp.where(pick, -jnp.inf, g)
            bias = jnp.concatenate([bias, jnp.full((LANES - MOBA_SEL_ROWS, seq), NEG_BIG, F32)], axis=0).T
            qaug_scr[h, :, 0:LANES] = qh
            qaug_scr[h, :, LANES:] = jnp.where(lane_all == SHIFT_LANE, shifts[h], bias).astype(BF16)

    lane = lax.broadcasted_iota(jnp.int32, (2 * bq, LANES), 1)
    _softmax_init(m_scr, l_scr, acc_scr)
    n_parts = MOBA_STEP * bq // LANES
    own_parts = bq // LANES
    qi = lax.broadcasted_iota(jnp.int32, (2 * bq, LANES), 0) & (bq - 1)
    q_start = pl.multiple_of(i * bq, bq)

    def step(t, first, fast):
        start = pl.multiple_of((i - MOBA_STEP * t) * bq, bq)
        kaug = kaug_scr[pl.ds(start, MOBA_STEP * bq), :]
        vaug = vaug_scr[pl.ds(start, MOBA_STEP * bq), :]
        parts = _lane_parts(lax.dot_general(_stacked_q(qaug_scr, q_start, bq), kaug, _NT,
                                            preferred_element_type=F32))
        if first:
            for c in range(own_parts):
                cc = n_parts - own_parts + c
                parts[cc] = jnp.where(lane + c * LANES <= qi, parts[cc], NEG_BIG)
        if fast:
            _plain_step(parts, vaug, l_scr, acc_scr)
        else:
            _softmax_step(parts, vaug, m_scr, l_scr, acc_scr)

    def run(fast):
        step(0, True, fast)

        def later(t, carry):
            step(t, False, fast)
            return carry

        lax.fori_loop(1, (i + MOBA_STEP) // MOBA_STEP, later, 0)

    pl.when(flag_ref[0] == 1)(functools.partial(run, True))
    pl.when(flag_ref[0] != 1)(functools.partial(run, False))
    _softmax_finish(o_ref, l_scr, acc_scr)


def _moba(h3, n_pairs):
    b, seq, _ = h3.shape
    sec_blocks = (AB_WIDTH // 2) // LANES
    nq = seq // ATT_BLOCK
    assert seq % ATT_BLOCK == 0 and nq <= MOBA_SEL_ROWS
    return pl.pallas_call(
        functools.partial(_moba_kernel, seq=seq),
        grid=(b, n_pairs, nq),
        in_specs=[pl.BlockSpec((1, seq, LANES), lambda bb, p, i: (bb, 0, p)),
                  pl.BlockSpec((1, seq, LANES), lambda bb, p, i: (bb, 0, sec_blocks + p)),
                  pl.BlockSpec((1, seq, LANES), lambda bb, p, i: (bb, 0, 2 * sec_blocks + p))],
        out_specs=pl.BlockSpec((1, ATT_BLOCK, LANES), lambda bb, p, i: (bb, i, p)),
        out_shape=jax.ShapeDtypeStruct((b, seq, n_pairs * LANES), BF16),
        scratch_shapes=[pltpu.VMEM((seq + (MOBA_STEP - 1) * ATT_BLOCK, 2 * LANES), BF16),
                        pltpu.VMEM((seq + (MOBA_STEP - 1) * ATT_BLOCK, 2 * LANES), BF16),
                        pltpu.VMEM((2, seq, 2 * LANES), BF16),
                        pltpu.SMEM((1,), jnp.int32),
                        pltpu.VMEM((2 * ATT_BLOCK, LANES), F32),
                        pltpu.VMEM((2 * ATT_BLOCK, LANES), F32),
                        pltpu.VMEM((2 * ATT_BLOCK, LANES), F32)],
        compiler_params=_params("parallel", "parallel", "arbitrary"),
        name="moba",
    )(h3, h3, h3)


def _dilated_n_blocks():
    return max(w for w, _ in DILATED_BRANCHES) // ATT_BLOCK + 1


def _dilated_bias_table():
    n_steps = -(-_dilated_n_blocks() // DIL_STEP)
    qi = np.arange(ATT_BLOCK)[:, None]
    ki = np.arange(ATT_BLOCK)[None, :]
    tabs = []
    for t in range(n_steps):
        groups = []
        for g in range(DIL_STEP):
            d = qi - ki + (DIL_STEP * t + DIL_STEP - 1 - g) * ATT_BLOCK
            cnt = np.zeros(d.shape, np.int64)
            for window, dil in DILATED_BRANCHES:
                cnt += ((d >= 0) & (d <= window) & (d % dil == 0)).astype(np.int64)
            groups.append(np.where(cnt > 0, np.log(np.maximum(cnt, 1).astype(np.float64)), NEG_BIG))
        tabs.append(np.concatenate(groups, axis=1))
    return np.stack(tabs).astype(np.float32)


def _dilated_kernel(q_ref, k_ref, v_ref, bias_ref, o_ref, kaug_scr, vaug_scr, qaug_scr, flag_ref, m_scr, l_scr, acc_scr,
                    *, seq):
    bq = ATT_BLOCK
    pad = (DIL_STEP - 1) * bq
    i = pl.program_id(2)

    @pl.when(i == 0)
    def _per_sequence_setup():
        q = q_ref[0]
        k = k_ref[0]
        shifts = _score_shifts(q, k, flag_ref)
        flag_lane = lax.broadcasted_iota(jnp.int32, (pad, LANES), 1)
        lane_all = lax.broadcasted_iota(jnp.int32, (seq, LANES), 1)
        kaug_scr[0:pad, 0:LANES] = jnp.zeros((pad, LANES), BF16)
        kaug_scr[0:pad, LANES:] = jnp.where((flag_lane == 0) | (flag_lane == SHIFT_LANE), 1.0, 0.0).astype(BF16)
        kaug_scr[pad:, 0:LANES] = k
        kaug_scr[pad:, LANES:] = jnp.where(lane_all == SHIFT_LANE, 1.0, 0.0).astype(BF16)
        vaug_scr[0:pad, 0:LANES] = jnp.zeros((pad, LANES), BF16)
        vaug_scr[pad:, 0:LANES] = v_ref[0]
        vaug_scr[:, LANES:] = jnp.ones((seq + pad, LANES), BF16)
        for h in range(2):
            aug = jnp.where(lane_all == SHIFT_LANE, shifts[h], jnp.where(lane_all == 0, NEG_BIG, 0.0))
            qaug_scr[h, :, 0:LANES] = jnp.where(_head_mask(lane_all, h), q, jnp.zeros_like(q))
            qaug_scr[h, :, LANES:] = aug.astype(BF16)

    _softmax_init(m_scr, l_scr, acc_scr)
    q_start = pl.multiple_of(i * bq, bq)

    def step(t, fast):
        start = pl.multiple_of((i - DIL_STEP * t) * bq, bq)
        kaug = kaug_scr[pl.ds(start, DIL_STEP * bq), :]
        vaug = vaug_scr[pl.ds(start, DIL_STEP * bq), :]
        bias = bias_ref[t]
        s = lax.dot_general(_stacked_q(qaug_scr, q_start, bq), kaug, _NT, preferred_element_type=F32)
        parts = _lane_parts(jnp.concatenate([s[0:bq, :] + bias, s[bq:, :] + bias], axis=0))
        if fast:
            _plain_step(parts, vaug, l_scr, acc_scr)
        else:
            _softmax_step(parts, vaug, m_scr, l_scr, acc_scr)

    n_steps = (jnp.minimum(i, _dilated_n_blocks() - 1) + DIL_STEP) // DIL_STEP

    def run(fast):
        def body(t, carry):
            step(t, fast)
            return carry

        lax.fori_loop(0, n_steps, body, 0)

    pl.when(flag_ref[0] == 1)(functools.partial(run, True))
    pl.when(flag_ref[0] != 1)(functools.partial(run, False))
    _softmax_finish(o_ref, l_scr, acc_scr)


def _dilated(h3, bias_tab, n_pairs):
    b, seq, _ = h3.shape
    sec_blocks = (AB_WIDTH // 2) // LANES
    base = 3 * sec_blocks
    nq = seq // ATT_BLOCK
    pad = (DIL_STEP - 1) * ATT_BLOCK
    return pl.pallas_call(
        functools.partial(_dilated_kernel, seq=seq),
        grid=(b, n_pairs, nq),
        in_specs=[pl.BlockSpec((1, seq, LANES), lambda bb, p, i: (bb, 0, base + p)),
                  pl.BlockSpec((1, seq, LANES), lambda bb, p, i: (bb, 0, base + sec_blocks + p)),
                  pl.BlockSpec((1, seq, LANES), lambda bb, p, i: (bb, 0, base + 2 * sec_blocks + p)),
                  pl.BlockSpec(bias_tab.shape, lambda bb, p, i: (0, 0, 0))],
        out_specs=pl.BlockSpec((1, ATT_BLOCK, LANES), lambda bb, p, i: (bb, i, p)),
        out_shape=jax.ShapeDtypeStruct((b, seq, n_pairs * LANES), BF16),
        scratch_shapes=[pltpu.VMEM((seq + pad, 2 * LANES), BF16),
                        pltpu.VMEM((seq + pad, 2 * LANES), BF16),
                        pltpu.VMEM((2, seq, 2 * LANES), BF16),
                        pltpu.SMEM((1,), jnp.int32),
                        pltpu.VMEM((2 * ATT_BLOCK, LANES), F32),
                        pltpu.VMEM((2 * ATT_BLOCK, LANES), F32),
                        pltpu.VMEM((2 * ATT_BLOCK, LANES), F32)],
        compiler_params=_params("parallel", "parallel", "arbitrary"),
        name="dilated",
    )(h3, h3, h3, bias_tab)


def _proj_c_kernel(x_ref, w_ref, loglb_ref, log1mlb_ref, omlb_ref, q_ref, lf_ref, kk_ref, v_ref, g_ref):
    xb = x_ref[...].astype(BF16)
    d = D_MODEL

    def sec(c):
        return jnp.dot(xb, w_ref[:, c * d:(c + 1) * d], preferred_element_type=F32)

    q_ref[...] = sec(0)
    z = sec(1)
    log_sig = jnp.minimum(z, 0.0) - jnp.log1p(jnp.exp(-jnp.abs(z)))
    a = loglb_ref[...]
    c = log1mlb_ref[...] + log_sig
    lf_ref[...] = jnp.maximum(a, c) + jnp.log1p(jnp.exp(-jnp.abs(a - c)))
    kk_ref[...] = omlb_ref[...] / (1.0 + jnp.exp(z))
    v_ref[...] = sec(2).astype(v_ref.dtype)
    g_ref[...] = sec(3)


def _proj_c(x2d, w_bf16, lb, tm=512):
    t, d = x2d.shape
    n = w_bf16.shape[1]
    lb = lb.astype(F32).reshape(1, d)
    vec_spec = pl.BlockSpec((1, d), lambda i: (0, 0))
    out_spec = pl.BlockSpec((tm, d), lambda i: (i, 0))
    sds = jax.ShapeDtypeStruct((t, d), F32)
    return pl.pallas_call(
        _proj_c_kernel,
        grid=(t // tm,),
        in_specs=[pl.BlockSpec((tm, d), lambda i: (i, 0)),
                  pl.BlockSpec((d, n), lambda i: (0, 0)),
                  vec_spec, vec_spec, vec_spec],
        out_specs=[out_spec] * 5,
        out_shape=[sds, sds, sds, jax.ShapeDtypeStruct((t, d), BF16), sds],
        compiler_params=_params("parallel"),
        name="proj_c",
    )(x2d, w_bf16, jnp.log(lb), jnp.log1p(-lb), 1.0 - lb)


def _split3_bf16(x):
    h1 = x.astype(BF16)
    r1 = x - h1.astype(F32)
    h2 = r1.astype(BF16)
    h3 = (r1 - h2.astype(F32)).astype(BF16)
    return h1, h2, h3


def _hgrn_tables():
    C, SUB, nblk = HGRN_CHUNK, HGRN_SUB, HGRN_BLOCK
    i = np.arange(C * nblk)[:, None]
    j = np.arange(C * nblk)[None, :]
    tri = ((i // C == j // C) & (j <= i)).astype(np.float32)
    r = np.arange(SUB * nblk)[:, None]
    masks = np.stack([((j // C == r // SUB) & (j % C <= s * SUB + r % SUB)).astype(np.float32)
                      for s in range(C // SUB)])
    return jnp.asarray(tri, BF16), jnp.asarray(masks, F32)


def _hgrn_kernel(q_ref, lf_ref, kk_ref, v_ref, g_ref, ng_ref, tri_ref, mask_ref, o_ref, *, seq):
    C, SUB = HGRN_CHUNK, HGRN_SUB
    n_sub = C // SUB
    ri = lax.broadcasted_iota(jnp.int32, (C, C), 0)
    ci = lax.broadcasted_iota(jnp.int32, (C, C), 1)
    tri = jnp.where(ci <= ri, 1.0, 0.0).astype(BF16)
    row_c = lax.broadcasted_iota(jnp.int32, (C, LANES), 0)
    row_s = lax.broadcasted_iota(jnp.int32, (SUB, C), 0)
    lane_s = lax.broadcasted_iota(jnp.int32, (SUB, C), 1)
    ng = ng_ref[...]

    nblk = HGRN_BLOCK
    R = C * nblk
    pos = lax.broadcasted_iota(jnp.int32, (R, LANES), 0) & (C - 1)

    def block_factored(blk, st):
        r0 = pl.multiple_of(blk * R, R)
        q = q_ref[0, pl.ds(r0, R), :]
        k = kk_ref[0, pl.ds(r0, R), :]
        vb = v_ref[0, pl.ds(r0, R), :].astype(BF16)
        lf = lf_ref[0, pl.ds(r0, R), :]
        l1 = lf.astype(BF16)
        l2 = (lf - l1.astype(F32)).astype(BF16)
        b = (jnp.dot(tri_ref[...], l1, preferred_element_type=F32)
             + jnp.dot(tri_ref[...], l2, preferred_element_type=F32))
        refs = [jnp.zeros((R, LANES), F32)]
        for s in range(1, n_sub):
            refs.append(jnp.concatenate(
                [jnp.broadcast_to(b[c * C + s * SUB - 1:c * C + s * SUB, :], (C, LANES)) for c in range(nblk)], axis=0))
        own_ref = refs[0]
        for s in range(1, n_sub):
            own_ref = jnp.where(pos >= s * SUB, refs[s], own_ref)
        qt = (q * jnp.exp(b - own_ref)).astype(BF16)
        qe = (q * jnp.exp(b)).astype(BF16)
        intra = []
        for s in range(n_sub):
            qs = jnp.concatenate([qt[c * C + s * SUB:c * C + (s + 1) * SUB, :] for c in range(nblk)], axis=0)
            kt = jnp.where(pos < (s + 1) * SUB,
                           k * jnp.exp(jnp.minimum(refs[s] - b, HGRN_FAST_LIMIT)), 0.0).astype(BF16)
            a = lax.dot_general(qs, kt, _NT, preferred_element_type=F32) * mask_ref[s]
            intra.append(jnp.dot(a.astype(BF16), vb, preferred_element_type=F32))
        outs = []
        for c in range(nblk):
            lo = c * C
            inter = lax.dot_general(qe[lo:lo + C, :], st.astype(BF16), _NT, preferred_element_type=F32)
            outs.append(inter + jnp.concatenate([intra[s][c * SUB:(c + 1) * SUB, :] for s in range(n_sub)], axis=0))
            b_last = b[lo + C - 1:lo + C, :]
            kd = (k[lo:lo + C, :] * jnp.exp(b_last - b[lo:lo + C, :])).astype(BF16)
            st = st * jnp.exp(b_last) + lax.dot_general(vb[lo:lo + C, :], kd, _TN, preferred_element_type=F32)
        o = jnp.concatenate(outs, axis=0)
        o = o * lax.rsqrt(jnp.mean(o * o, axis=1, keepdims=True) + RMS_EPS) * ng
        gc = g_ref[0, pl.ds(r0, R), :]
        o_ref[0, pl.ds(r0, R), :] = (o * (gc / (1.0 + jnp.exp(-gc)))).astype(o_ref.dtype)
        return st

    def scores_direct(qc, kc, b):
        rows_a = []
        for sidx in range(n_sub):
            lo = sidx * SUB
            q_i = qc[lo:lo + SUB, :]
            b_i = b[lo:lo + SUB, :]
            if sidx == 0:
                a_blk = jnp.zeros((SUB, C), F32)
            else:
                ref = b[lo - 1:lo, :]
                qt = (q_i * jnp.exp(b_i - ref)).astype(BF16)
                kt = jnp.where(row_c < lo, kc * jnp.exp(jnp.minimum(ref - b, 0.0)), 0.0).astype(BF16)
                a_blk = lax.dot_general(qt, kt, _NT, preferred_element_type=F32)
            for j in range(SUB):
                r_lo = 0 if j < 8 else 8
                bj = b[lo + j:lo + j + 1, :]
                kj = kc[lo + j:lo + j + 1, :]
                pj = q_i[r_lo:, :] * (jnp.exp(jnp.minimum(b_i[r_lo:, :] - bj, 0.0)) * kj)
                col = jnp.sum(pj, axis=1, keepdims=True)
                if r_lo:
                    col = jnp.concatenate([jnp.zeros((r_lo, 1), F32), col], axis=0)
                a_blk = jnp.where((lane_s == lo + j) & (row_s >= j), col, a_blk)
            rows_a.append(a_blk)
        return jnp.concatenate(rows_a, axis=0)

    def make_chunk(scores):
        def chunk(c, st):
            r0 = pl.multiple_of(c * C, C)
            qc = q_ref[0, pl.ds(r0, C), :]
            kc = kk_ref[0, pl.ds(r0, C), :]
            vb = v_ref[0, pl.ds(r0, C), :].astype(BF16)
            l1, l2, l3 = _split3_bf16(lf_ref[0, pl.ds(r0, C), :])
            b = (jnp.dot(tri, l1, preferred_element_type=F32)
                 + jnp.dot(tri, l2, preferred_element_type=F32)
                 + jnp.dot(tri, l3, preferred_element_type=F32))
            b_last = b[C - 1:C, :]
            inter = lax.dot_general((qc * jnp.exp(b)).astype(BF16), st.astype(BF16), _NT,
                                    preferred_element_type=F32)
            o = inter + jnp.dot(scores(qc, kc, b).astype(BF16), vb, preferred_element_type=F32)
            o = o * lax.rsqrt(jnp.mean(o * o, axis=1, keepdims=True) + RMS_EPS) * ng
            gc = g_ref[0, pl.ds(r0, C), :]
            o_ref[0, pl.ds(r0, C), :] = (o * (gc / (1.0 + jnp.exp(-gc)))).astype(o_ref.dtype)
            kd = (kc * jnp.exp(b_last - b)).astype(BF16)
            return st * jnp.exp(b_last) + lax.dot_general(vb, kd, _TN, preferred_element_type=F32)
        return chunk

    sub_decay = jnp.sum(lf_ref[0].reshape(seq // SUB, SUB, LANES), axis=1)
    fast_ok = jnp.min(sub_decay) >= -HGRN_FAST_LIMIT
    st0 = jnp.zeros((LANES, LANES), F32)

    @pl.when(fast_ok)
    def _():
        lax.fori_loop(0, seq // R, block_factored, st0)

    @pl.when(jnp.logical_not(fast_ok))
    def _():
        lax.fori_loop(0, seq // C, make_chunk(scores_direct), st0)


def _hgrn(q, lf, kk, v, g, norm_g, batch, seq):
    assert seq % (HGRN_CHUNK * HGRN_BLOCK) == 0
    shp = (batch, seq, D_MODEL)
    args = [a.reshape(shp) for a in (q, lf, kk, v, g)]
    tri, masks = _hgrn_tables()
    spec = pl.BlockSpec((1, seq, LANES), lambda bb, h: (bb, 0, h))
    return pl.pallas_call(
        functools.partial(_hgrn_kernel, seq=seq),
        grid=(batch, N_HEADS_C),
        in_specs=[spec] * 5 + [pl.BlockSpec((1, LANES), lambda bb, h: (0, 0)),
                               pl.BlockSpec(tri.shape, lambda bb, h: (0, 0)),
                               pl.BlockSpec(masks.shape, lambda bb, h: (0, 0, 0))],
        out_specs=spec,
        out_shape=jax.ShapeDtypeStruct(shp, BF16),
        compiler_params=_params("parallel", "parallel"),
        name="hgrn",
    )(*args, norm_g.astype(F32).reshape(1, LANES), tri, masks)


def _layer_norm_rows(z, g, b):
    mu = jnp.mean(z, axis=1, keepdims=True)
    zc = z - mu
    var = jnp.mean(zc * zc, axis=1, keepdims=True)
    return zc * lax.rsqrt(var + LN_EPS) * g + b


def _first_lane_of_max(vals, lane_f):
    mx = jnp.max(vals, axis=1, keepdims=True)
    return mx, jnp.min(jnp.where(vals == mx, lane_f, float(LANES)), axis=1, keepdims=True)


def _outproj_ln_kernel(*refs, n_parts):
    o_refs, w_refs = refs[:n_parts], refs[n_parts:2 * n_parts]
    (x_ref, g_ref, b_ref, wr_hi_ref, wr_lo_ref, rb_ref,
     y_ref, yb_ref, fields_ref, info_ref, cnt_ref) = refs[2 * n_parts:]
    mix = jnp.dot(o_refs[0][...], w_refs[0][...], preferred_element_type=F32)
    for o_ref, w_ref in zip(o_refs[1:], w_refs[1:]):
        mix += jnp.dot(o_ref[...], w_ref[...], preferred_element_type=F32)
    y = _layer_norm_rows(DEEPNORM_ALPHA * x_ref[...] + mix, g_ref[...], b_ref[...])
    y_ref[...] = y
    y_hi = y.astype(BF16)
    yb_ref[...] = y_hi
    y_lo = (y - y_hi.astype(F32)).astype(BF16)
    lg = (jnp.dot(y_hi, wr_hi_ref[...], preferred_element_type=F32)
          + jnp.dot(y_lo, wr_hi_ref[...], preferred_element_type=F32)
          + jnp.dot(y_hi, wr_lo_ref[...], preferred_element_type=F32)
          + rb_ref[...])
    lane = lax.broadcasted_iota(jnp.int32, lg.shape, 1)
    lane_f = lane.astype(F32)
    is_g = lane < N_GROUPS
    mg, grp = _first_lane_of_max(jnp.where(is_g, lg, -jnp.inf), lane_f)
    pg = 1.0 / jnp.sum(jnp.where(is_g, jnp.exp(lg - mg), 0.0), axis=1, keepdims=True)
    lo = float(N_GROUPS) + float(EXPERTS_PER_GROUP) * grp
    le = jnp.where((lane_f >= lo) & (lane_f < lo + float(EXPERTS_PER_GROUP)), lg, -jnp.inf)
    m1, i1 = _first_lane_of_max(le, lane_f)
    m2, i2 = _first_lane_of_max(jnp.where(lane_f == i1, -jnp.inf, le), lane_f)
    e21 = jnp.exp(m2 - m1)
    w0 = pg / (1.0 + e21)
    w1 = pg * e21 / (1.0 + e21)
    eid0 = i1 - float(N_GROUPS)
    eid1 = i2 - float(N_GROUPS)

    @pl.when(pl.program_id(0) == 0)
    def _():
        cnt_ref[...] = jnp.zeros(cnt_ref.shape, F32)

    hits = jnp.where(lane_f == eid0, 1.0, 0.0) + jnp.where(lane_f == eid1, 1.0, 0.0)
    cnt_ref[...] += jnp.sum(hits, axis=0, keepdims=True)
    fields = jnp.where(lane == 0, eid0, jnp.where(lane == 1, eid1, jnp.where(lane == 2, w0,
                       jnp.where(lane == 3, w1, 0.0))))
    fields_ref[...] = fields
    info_ref[...] = fields.T[:8, :]


def _outproj_ln(o_parts, x2d, w_bf16, ln_g, ln_b, wr, rb, tm=512):
    t, d = x2d.shape
    wr_hi = wr.astype(BF16)
    wr_lo = (wr - wr_hi.astype(F32)).astype(BF16)
    vec = lambda n: pl.BlockSpec((1, n), lambda i: (0, 0))
    widths = [o.shape[1] for o in o_parts]
    offs = np.cumsum([0] + widths)
    w_parts = [w_bf16[offs[k]:offs[k + 1]] for k in range(len(widths))]
    return pl.pallas_call(
        functools.partial(_outproj_ln_kernel, n_parts=len(widths)),
        grid=(t // tm,),
        in_specs=[pl.BlockSpec((tm, wd), lambda i: (i, 0)) for wd in widths]
                 + [pl.BlockSpec((wd, d), lambda i: (0, 0)) for wd in widths]
                 + [pl.BlockSpec((tm, d), lambda i: (i, 0)),
                  vec(d), vec(d),
                  pl.BlockSpec((d, LANES), lambda i: (0, 0)),
                  pl.BlockSpec((d, LANES), lambda i: (0, 0)),
                  vec(LANES)],
        out_specs=[pl.BlockSpec((tm, d), lambda i: (i, 0)),
                   pl.BlockSpec((tm, d), lambda i: (i, 0)),
                   pl.BlockSpec((tm, LANES), lambda i: (i, 0)),
                   pl.BlockSpec((8, tm), lambda i: (0, i)),
                   vec(LANES)],
        out_shape=[jax.ShapeDtypeStruct((t, d), F32),
                   jax.ShapeDtypeStruct((t, d), BF16),
                   jax.ShapeDtypeStruct((t, LANES), F32),
                   jax.ShapeDtypeStruct((8, t), F32),
                   jax.ShapeDtypeStruct((1, LANES), F32)],
        compiler_params=_params("arbitrary"),
        name="outproj_ln",
    )(*o_parts, *w_parts, x2d, ln_g.reshape(1, d), ln_b.reshape(1, d), wr_hi, wr_lo, rb)


def _expert_kernel(te_ref, nv_ref, xs_ref, w1_ref, w3_ref, w2_ref, o_ref, w1_s, w3_s, w2_s):
    i = pl.program_id(0)
    prev = te_ref[jnp.maximum(i - 1, 0)]

    @pl.when((i == 0) | (te_ref[i] != prev))
    def _cast_weights():
        w1_s[...] = w1_ref[0, 0].astype(BF16)
        w3_s[...] = w3_ref[0, 0].astype(BF16)
        w2_s[...] = w2_ref[0, 0].astype(BF16)

    @pl.when(i < nv_ref[0])
    def _ffn():
        for r in range(0, MOE_TILE, MOE_SLAB):
            xb = xs_ref[r:r + MOE_SLAB, :]
            a = jnp.dot(xb, w1_s[...], preferred_element_type=F32)
            u = jnp.dot(xb, w3_s[...], preferred_element_type=F32)
            hb = (a / (1.0 + jnp.exp(-a))) * u
            o_ref[r:r + MOE_SLAB, :] = jnp.dot(hb.astype(BF16), w2_s[...],
                                               preferred_element_type=F32).astype(o_ref.dtype)

    @pl.when(i >= nv_ref[0])
    def _pad():
        o_ref[...] = jnp.zeros(o_ref.shape, o_ref.dtype)


def _expert_ffn(tile_expert, n_valid, xs, w1, w3, w2, layer):
    p, d = xs.shape
    f = w1.shape[3]
    n_tiles = p // MOE_TILE
    grid_spec = pltpu.PrefetchScalarGridSpec(
        num_scalar_prefetch=2,
        grid=(n_tiles,),
        in_specs=[pl.BlockSpec((MOE_TILE, d), lambda i, te, nv: (i, 0)),
                  pl.BlockSpec((1, 1, d, f), lambda i, te, nv: (layer, te[i], 0, 0)),
                  pl.BlockSpec((1, 1, d, f), lambda i, te, nv: (layer, te[i], 0, 0)),
                  pl.BlockSpec((1, 1, f, d), lambda i, te, nv: (layer, te[i], 0, 0))],
        out_specs=pl.BlockSpec((MOE_TILE, d), lambda i, te, nv: (i, 0)),
        scratch_shapes=[pltpu.VMEM((d, f), BF16), pltpu.VMEM((d, f), BF16), pltpu.VMEM((f, d), BF16)],
    )
    return pl.pallas_call(
        _expert_kernel,
        grid_spec=grid_spec,
        out_shape=jax.ShapeDtypeStruct((p, d), BF16),
        compiler_params=_params("arbitrary"),
        name="expert_ffn",
    )(tile_expert, n_valid, xs, w1, w3, w2)


def _combine_ln_kernel(x_ref, y0_ref, y1_ref, fields_ref, g_ref, b_ref, o_ref):
    fields = fields_ref[...]
    ffn = fields[:, 2:3] * y0_ref[...].astype(F32) + fields[:, 3:4] * y1_ref[...].astype(F32)
    o_ref[...] = _layer_norm_rows(DEEPNORM_ALPHA * x_ref[...] + ffn, g_ref[...], b_ref[...])


def _combine_ln(x2d, y0, y1, fields, ln_g, ln_b, tm=512):
    t, d = x2d.shape
    row = pl.BlockSpec((tm, d), lambda i: (i, 0))
    vec = pl.BlockSpec((1, d), lambda i: (0, 0))
    return pl.pallas_call(
        _combine_ln_kernel,
        grid=(t // tm,),
        in_specs=[row, row, row, pl.BlockSpec((tm, LANES), lambda i: (i, 0)), vec, vec],
        out_specs=row,
        out_shape=jax.ShapeDtypeStruct((t, d), F32),
        compiler_params=_params("parallel"),
        name="combine_ln",
    )(x2d, y0, y1, fields, ln_g.reshape(1, d), ln_b.reshape(1, d))


def _dispatch_plan(info, counts_f):
    t = info.shape[1]
    a = t * TOPK_IN_GROUP
    n_fill = N_EXPERTS * MOE_TILE
    n_tiles = a // MOE_TILE + N_EXPERTS
    experts = jnp.arange(N_EXPERTS, dtype=jnp.int32)
    counts = counts_f[0, :N_EXPERTS].astype(jnp.int32)
    padded = (counts + MOE_TILE - 1) // MOE_TILE * MOE_TILE
    pends = jnp.cumsum(padded)
    fill_ends = jnp.cumsum(padded - counts)
    fill = jnp.arange(n_fill, dtype=jnp.int32)
    fill_key = jnp.sum((fill_ends[None, :] <= fill[:, None]).astype(jnp.int32), axis=1)
    tok = jnp.arange(t, dtype=jnp.int32)
    keys = jnp.concatenate([info[0].astype(jnp.int32), info[1].astype(jnp.int32), fill_key])
    toks = jnp.concatenate([tok, tok, fill % t])
    flat = jnp.arange(a + n_fill, dtype=jnp.int32)
    _, slot_tok, slot_flat = lax.sort((keys, toks, flat), num_keys=1, is_stable=True)
    _, slot_of = lax.sort((slot_flat, flat), num_keys=1)
    tile_start = jnp.arange(n_tiles, dtype=jnp.int32) * MOE_TILE
    tile_expert = jnp.sum((pends[None, :] <= tile_start[:, None]).astype(jnp.int32), axis=1)
    last_used = jnp.max(jnp.where(counts > 0, experts, 0))
    tile_expert = jnp.minimum(tile_expert, last_used)
    n_valid = (pends[-1] // MOE_TILE).astype(jnp.int32).reshape(1)
    return tile_expert, n_valid, slot_tok, slot_of[:t], slot_of[t:a]


def _moe_and_norm(x1, x1_bf16, fields, info, counts_f, w1, w3, w2, layer, ln_g, ln_b):
    tile_expert, n_valid, slot_tok, dest0, dest1 = _dispatch_plan(info, counts_f)
    yb = _expert_ffn(tile_expert, n_valid, x1_bf16[slot_tok], w1, w3, w2, layer)
    return _combine_ln(x1, yb[dest0], yb[dest1], fields, ln_g, ln_b)


def kernel(x, ab_w_in, ab_w_out, c_w_in, c_w_out, c_norm_g, hgrn_lb_logits, ln_g, ln_b,
           router_g_w, router_g_b, router_e_w, router_e_b, exp_w1, exp_w3, exp_w2):
    batch, seq, d = x.shape
    t = batch * seq
    tables = _rope_lane_tables(seq)
    bias_tab = jnp.asarray(_dilated_bias_table())
    lb_all = jnp.cumsum(jax.nn.softmax(hgrn_lb_logits.astype(F32), axis=0), axis=0)
    lb_all = lb_all - lb_all[0:1]
    n_pairs = (AB_WIDTH // 2) // LANES

    xc = x.reshape(t, d)
    for l in range(DEPTH):
        j = l // 2
        if l % 2 == 0:
            h = _proj_ab(xc, ab_w_in[j].astype(BF16), tables, seq).reshape(batch, seq, 3 * AB_WIDTH)
            o_parts = [_moba(h, n_pairs).reshape(t, AB_WIDTH // 2),
                       _dilated(h, bias_tab, n_pairs).reshape(t, AB_WIDTH // 2)]
            w_out = ab_w_out[j]
        else:
            q, lf, kk, v, g = _proj_c(xc, c_w_in[j].astype(BF16), lb_all[j])
            o_parts = [_hgrn(q, lf, kk, v, g, c_norm_g[j], batch, seq).reshape(t, d)]
            w_out = c_w_out[j]
        wr = jnp.zeros((d, LANES), F32)
        wr = wr.at[:, :N_GROUPS].set(router_g_w[l]).at[:, N_GROUPS:N_GROUPS + N_EXPERTS].set(router_e_w[l])
        rb = jnp.zeros((1, LANES), F32)
        rb = rb.at[0, :N_GROUPS].set(router_g_b[l]).at[0, N_GROUPS:N_GROUPS + N_EXPERTS].set(router_e_b[l])
        x1, x1_bf16, fields, info, counts_f = _outproj_ln(o_parts, xc, w_out.astype(BF16), ln_g[l, 0], ln_b[l, 0],
                                                          wr, rb)
        xc = _moe_and_norm(x1, x1_bf16, fields, info, counts_f, exp_w1, exp_w3, exp_w2, l,
                           ln_g[l, 1], ln_b[l, 1])
    return xc.reshape(batch, seq, d)
```

```python
import functools
import math

import numpy as np
import jax
import jax.numpy as jnp
from jax import lax
from jax.experimental import pallas as pl
from jax.experimental.pallas import tpu as pltpu

F32 = jnp.float32
BF16 = jnp.bfloat16

D_MODEL = 1024
DEPTH = 4
HEAD_DIM = 64
N_HEADS_A = 8
N_HEADS_B = 8
AB_WIDTH = (N_HEADS_A + N_HEADS_B) * HEAD_DIM
ROT_DIM = HEAD_DIM // 4
ROPE_THETA = 500000.0
MOBA_BLOCK = 256
MOBA_TOPK = 3
DILATED_BRANCHES = ((128, 1), (512, 4), (2048, 16))
HGRN_EXPAND = 128
N_HEADS_C = D_MODEL // HGRN_EXPAND
HGRN_CHUNK = 64
HGRN_SUB = 16
HGRN_UNROLL = 8
HGRN_FAST_LIMIT = 60.0
N_GROUPS = 4
EXPERTS_PER_GROUP = 8
N_EXPERTS = N_GROUPS * EXPERTS_PER_GROUP
TOPK_IN_GROUP = 2
D_EXPERT = D_MODEL // 2
DEEPNORM_ALPHA = (2.0 * DEPTH) ** 0.25
LN_EPS = 1e-5
RMS_EPS = 1e-6

LANES = 128
ATT_BLOCK = 256
MOBA_STEP = 4
MOBA_SEL_ROWS = 16
SHIFT_LANE = LANES - 2
ATT_FAST_LIMIT = 38.0
DIL_STEP = 3
NEG_BIG = -1e30
MOE_TILE = 512
MOE_SLAB = 512
VMEM_LIMIT = 56 * 1024 * 1024

_NT = (((1,), (1,)), ((), ()))
_TN = (((0,), (0,)), ((), ()))


def _params(*sem):
    return pltpu.CompilerParams(dimension_semantics=sem, vmem_limit_bytes=VMEM_LIMIT)


def _proj_ab_kernel(x_ref, w_ref, cos_ref, sa_ref, sb_ref, o_ref):
    xb = x_ref[...].astype(BF16)
    sec = AB_WIDTH // 2
    for c in range(6):
        acc = jnp.dot(xb, w_ref[:, c * sec:(c + 1) * sec], preferred_element_type=F32)
        if c % 3 == 2:
            o_ref[:, c * sec:(c + 1) * sec] = acc.astype(BF16)
            continue
        cosv = cos_ref[...]
        sa = sa_ref[...]
        sb = sb_ref[...]
        for k in range(sec // LANES):
            a = acc[:, k * LANES:(k + 1) * LANES]
            r = a * cosv + pltpu.roll(a, LANES - ROT_DIM // 2, 1) * sa + pltpu.roll(a, ROT_DIM // 2, 1) * sb
            if c % 3 == 0:
                r = r * (HEAD_DIM ** -0.5)
            o_ref[:, c * sec + k * LANES:c * sec + (k + 1) * LANES] = r.astype(BF16)


def _rope_lane_tables(seq):
    half = ROT_DIM // 2
    inv = ROPE_THETA ** (-jnp.arange(half, dtype=F32) / half)
    ang = jnp.arange(seq, dtype=F32)[:, None] * inv[None, :]
    cos, sin = jnp.cos(ang), jnp.sin(ang)
    ones = jnp.ones((seq, HEAD_DIM - ROT_DIM), F32)
    zeros = jnp.zeros((seq, HEAD_DIM - ROT_DIM), F32)
    zh = jnp.zeros((seq, half), F32)
    cos_t = jnp.concatenate([cos, cos, ones], axis=1)
    sa_t = jnp.concatenate([-sin, zh, zeros], axis=1)
    sb_t = jnp.concatenate([zh, sin, zeros], axis=1)
    rep = LANES // HEAD_DIM
    return tuple(jnp.tile(t, (1, rep)) for t in (cos_t, sa_t, sb_t))


def _proj_ab(x2d, w_bf16, tables, seq, tm=512):
    t, d = x2d.shape
    n = w_bf16.shape[1]
    nseq = seq // tm
    tab_spec = pl.BlockSpec((tm, LANES), lambda i: (i % nseq, 0))
    return pl.pallas_call(
        _proj_ab_kernel,
        grid=(t // tm,),
        in_specs=[pl.BlockSpec((tm, d), lambda i: (i, 0)),
                  pl.BlockSpec((d, n), lambda i: (0, 0)),
                  tab_spec, tab_spec, tab_spec],
        out_specs=pl.BlockSpec((tm, n), lambda i: (i, 0)),
        out_shape=jax.ShapeDtypeStruct((t, n), BF16),
        compiler_params=_params("parallel"),
        name="proj_ab",
    )(x2d, w_bf16, *tables)


def _softmax_step(parts, vaug, m_scr, l_scr, acc_scr):
    m_prev = m_scr[...]
    mx = functools.reduce(jnp.maximum, parts)
    m_new = jnp.maximum(m_prev, jnp.max(mx, axis=1, keepdims=True))
    alpha = jnp.exp(m_prev - m_new)
    p = jnp.concatenate([jnp.exp(part - m_new).astype(BF16) for part in parts], axis=1)
    pv = jnp.dot(p, vaug, preferred_element_type=F32)
    acc_scr[...] = alpha * acc_scr[...] + pv[:, :LANES]
    l_scr[...] = alpha * l_scr[...] + pv[:, LANES:]
    m_scr[...] = m_new


def _plain_step(parts, vaug, l_scr, acc_scr):
    p = jnp.concatenate([jnp.exp(part).astype(BF16) for part in parts], axis=1)
    pv = jnp.dot(p, vaug, preferred_element_type=F32)
    acc_scr[...] += pv[:, :LANES]
    l_scr[...] += pv[:, LANES:]


def _stacked_q(qaug_scr, q_start, bq):
    return jnp.concatenate([qaug_scr[0, pl.ds(q_start, bq), :], qaug_scr[1, pl.ds(q_start, bq), :]], axis=0)


def _lane_parts(s):
    return [s[:, c * LANES:(c + 1) * LANES] for c in range(s.shape[1] // LANES)]


def _head_mask(lane, h):
    return (lane >= h * HEAD_DIM) & (lane < (h + 1) * HEAD_DIM)


def _head_sq_norms(x, h):
    lane = lax.broadcasted_iota(jnp.int32, x.shape, 1)
    xf = x.astype(F32)
    sq = jnp.where(_head_mask(lane, h), xf * xf, 0.0).astype(BF16)
    return jnp.dot(sq, jnp.ones((LANES, LANES), BF16), preferred_element_type=F32)


def _score_shifts(q, k, flag_ref):
    bounds = [jnp.sqrt(jnp.max(_head_sq_norms(q, h), axis=0, keepdims=True)
                       * jnp.max(_head_sq_norms(k, h), axis=0, keepdims=True)) for h in range(2)]
    fast_ok = jnp.max(jnp.maximum(bounds[0], bounds[1])) <= ATT_FAST_LIMIT
    flag_ref[0] = jnp.where(fast_ok, 1, 0).astype(jnp.int32)
    return [jnp.where(fast_ok, -b, 0.0) for b in bounds]


def _softmax_init(m_scr, l_scr, acc_scr):
    m_scr[...] = jnp.full(m_scr.shape, -jnp.inf, F32)
    l_scr[...] = jnp.zeros(l_scr.shape, F32)
    acc_scr[...] = jnp.zeros(acc_scr.shape, F32)


def _softmax_finish(o_ref, l_scr, acc_scr):
    bq = acc_scr.shape[0] // 2
    lane = lax.broadcasted_iota(jnp.int32, (bq, LANES), 1)
    o0 = acc_scr[0:bq, :] / l_scr[0:bq, :]
    o1 = acc_scr[bq:, :] / l_scr[bq:, :]
    o_ref[0] = jnp.where(lane < HEAD_DIM, o0, o1).astype(o_ref.dtype)


def _moba_kernel(q_ref, k_ref, v_ref, o_ref, kaug_scr, vaug_scr, qaug_scr, flag_ref, m_scr, l_scr, acc_scr, *, seq):
    bq = ATT_BLOCK
    pad = (MOBA_STEP - 1) * bq
    i = pl.program_id(2)

    @pl.when(i == 0)
    def _per_sequence_setup():
        q = q_ref[0]
        k = k_ref[0]
        rows = lax.broadcasted_iota(jnp.int32, (MOBA_SEL_ROWS, seq), 0)
        cols = lax.broadcasted_iota(jnp.int32, (MOBA_SEL_ROWS, seq), 1)
        own = jnp.right_shift(cols, int(math.log2(MOBA_BLOCK)))
        ind = jnp.where(own == rows, 1.0, 0.0).astype(BF16)
        km = jnp.dot(ind, k, preferred_element_type=F32) * (1.0 / MOBA_BLOCK)
        km_hi = km.astype(BF16)
        km_lo = (km - km_hi.astype(F32)).astype(BF16)
        shifts = _score_shifts(q, k, flag_ref)
        blk_lane = lax.broadcasted_iota(jnp.int32, (bq, LANES), 1)
        kaug_scr[0:pad, 0:LANES] = jnp.zeros((pad, LANES), BF16)
        vaug_scr[0:pad, 0:LANES] = jnp.zeros((pad, LANES), BF16)
        for jb in range(-(MOBA_STEP - 1), seq // bq):
            tgt = jb if jb >= 0 else LANES - 1
            kaug_scr[pad + jb * bq:pad + (jb + 1) * bq, LANES:] = jnp.where(
                (blk_lane == tgt) | (blk_lane == SHIFT_LANE), 1.0, 0.0).astype(BF16)
        kaug_scr[pad:, 0:LANES] = k
        vaug_scr[pad:, 0:LANES] = v_ref[0]
        vaug_scr[:, LANES:] = jnp.ones((seq + pad, LANES), BF16)
        lane_all = lax.broadcasted_iota(jnp.int32, (seq, LANES), 1)
        rows_f = rows.astype(F32)
        for h in range(2):
            qh = jnp.where(_head_mask(lane_all, h), q, jnp.zeros_like(q))
            g = (lax.dot_general(km_hi, qh, _NT, preferred_element_type=F32)
                 + lax.dot_general(km_lo, qh, _NT, preferred_element_type=F32))
            g = jnp.where(rows < own, g, -jnp.inf)
            bias = jnp.where(rows == own, 0.0, NEG_BIG)
            for _ in range(MOBA_TOPK):
                mx = jnp.max(g, axis=0, keepdims=True)
                first = jnp.min(jnp.where(g == mx, rows_f, float(MOBA_SEL_ROWS)), axis=0, keepdims=True)
                pick = (rows_f == first) & (mx > -jnp.inf)
                bias = jnp.where(pick, 0.0, bias)
                g = jnp.where(pick, -jnp.inf, g)
            bias = jnp.concatenate([bias, jnp.full((LANES - MOBA_SEL_ROWS, seq), NEG_BIG, F32)], axis=0).T
            qaug_scr[h, :, 0:LANES] = qh
            qaug_scr[h, :, LANES:] = jnp.where(lane_all == SHIFT_LANE, shifts[h], bias).astype(BF16)

    lane = lax.broadcasted_iota(jnp.int32, (2 * bq, LANES), 1)
    _softmax_init(m_scr, l_scr, acc_scr)
    n_parts = MOBA_STEP * bq // LANES
    own_parts = bq // LANES
    qi = lax.broadcasted_iota(jnp.int32, (2 * bq, LANES), 0) & (bq - 1)
    q_start = pl.multiple_of(i * bq, bq)

    def step(t, first, fast):
        start = pl.multiple_of((i - MOBA_STEP * t) * bq, bq)
        kaug = kaug_scr[pl.ds(start, MOBA_STEP * bq), :]
        vaug = vaug_scr[pl.ds(start, MOBA_STEP * bq), :]
        parts = _lane_parts(lax.dot_general(_stacked_q(qaug_scr, q_start, bq), kaug, _NT,
                                            preferred_element_type=F32))
        if first:
            for c in range(own_parts):
                cc = n_parts - own_parts + c
                parts[cc] = jnp.where(lane + c * LANES <= qi, parts[cc], NEG_BIG)
        if fast:
            _plain_step(parts, vaug, l_scr, acc_scr)
        else:
            _softmax_step(parts, vaug, m_scr, l_scr, acc_scr)

    def run(fast):
        step(0, True, fast)

        def later(t, carry):
            step(t, False, fast)
            return carry

        lax.fori_loop(1, (i + MOBA_STEP) // MOBA_STEP, later, 0)

    pl.when(flag_ref[0] == 1)(functools.partial(run, True))
    pl.when(flag_ref[0] != 1)(functools.partial(run, False))
    _softmax_finish(o_ref, l_scr, acc_scr)


def _moba(h3, n_pairs):
    b, seq, _ = h3.shape
    sec_blocks = (AB_WIDTH // 2) // LANES
    nq = seq // ATT_BLOCK
    assert seq % ATT_BLOCK == 0 and nq <= MOBA_SEL_ROWS
    return pl.pallas_call(
        functools.partial(_moba_kernel, seq=seq),
        grid=(b, n_pairs, nq),
        in_specs=[pl.BlockSpec((1, seq, LANES), lambda bb, p, i: (bb, 0, p)),
                  pl.BlockSpec((1, seq, LANES), lambda bb, p, i: (bb, 0, sec_blocks + p)),
                  pl.BlockSpec((1, seq, LANES), lambda bb, p, i: (bb, 0, 2 * sec_blocks + p))],
        out_specs=pl.BlockSpec((1, ATT_BLOCK, LANES), lambda bb, p, i: (bb, i, p)),
        out_shape=jax.ShapeDtypeStruct((b, seq, n_pairs * LANES), BF16),
        scratch_shapes=[pltpu.VMEM((seq + (MOBA_STEP - 1) * ATT_BLOCK, 2 * LANES), BF16),
                        pltpu.VMEM((seq + (MOBA_STEP - 1) * ATT_BLOCK, 2 * LANES), BF16),
                        pltpu.VMEM((2, seq, 2 * LANES), BF16),
                        pltpu.SMEM((1,), jnp.int32),
                        pltpu.VMEM((2 * ATT_BLOCK, LANES), F32),
                        pltpu.VMEM((2 * ATT_BLOCK, LANES), F32),
                        pltpu.VMEM((2 * ATT_BLOCK, LANES), F32)],
        compiler_params=_params("parallel", "parallel", "arbitrary"),
        name="moba",
    )(h3, h3, h3)


def _dilated_n_blocks():
    return max(w for w, _ in DILATED_BRANCHES) // ATT_BLOCK + 1


def _dilated_bias_table():
    n_steps = -(-_dilated_n_blocks() // DIL_STEP)
    qi = np.arange(ATT_BLOCK)[:, None]
    ki = np.arange(ATT_BLOCK)[None, :]
    tabs = []
    for t in range(n_steps):
        groups = []
        for g in range(DIL_STEP):
            d = qi - ki + (DIL_STEP * t + DIL_STEP - 1 - g) * ATT_BLOCK
            cnt = np.zeros(d.shape, np.int64)
            for window, dil in DILATED_BRANCHES:
                cnt += ((d >= 0) & (d <= window) & (d % dil == 0)).astype(np.int64)
            groups.append(np.where(cnt > 0, np.log(np.maximum(cnt, 1).astype(np.float64)), NEG_BIG))
        tabs.append(np.concatenate(groups, axis=1))
    return np.stack(tabs).astype(np.float32)


def _dilated_kernel(q_ref, k_ref, v_ref, bias_ref, o_ref, kaug_scr, vaug_scr, qaug_scr, flag_ref, m_scr, l_scr, acc_scr,
                    *, seq):
    bq = ATT_BLOCK
    pad = (DIL_STEP - 1) * bq
    i = pl.program_id(2)

    @pl.when(i == 0)
    def _per_sequence_setup():
        q = q_ref[0]
        k = k_ref[0]
        shifts = _score_shifts(q, k, flag_ref)
        flag_lane = lax.broadcasted_iota(jnp.int32, (pad, LANES), 1)
        lane_all = lax.broadcasted_iota(jnp.int32, (seq, LANES), 1)
        kaug_scr[0:pad, 0:LANES] = jnp.zeros((pad, LANES), BF16)
        kaug_scr[0:pad, LANES:] = jnp.where((flag_lane == 0) | (flag_lane == SHIFT_LANE), 1.0, 0.0).astype(BF16)
        kaug_scr[pad:, 0:LANES] = k
        kaug_scr[pad:, LANES:] = jnp.where(lane_all == SHIFT_LANE, 1.0, 0.0).astype(BF16)
        vaug_scr[0:pad, 0:LANES] = jnp.zeros((pad, LANES), BF16)
        vaug_scr[pad:, 0:LANES] = v_ref[0]
        vaug_scr[:, LANES:] = jnp.ones((seq + pad, LANES), BF16)
        for h in range(2):
            aug = jnp.where(lane_all == SHIFT_LANE, shifts[h], jnp.where(lane_all == 0, NEG_BIG, 0.0))
            qaug_scr[h, :, 0:LANES] = jnp.where(_head_mask(lane_all, h), q, jnp.zeros_like(q))
            qaug_scr[h, :, LANES:] = aug.astype(BF16)

    _softmax_init(m_scr, l_scr, acc_scr)
    q_start = pl.multiple_of(i * bq, bq)

    def step(t, fast):
        start = pl.multiple_of((i - DIL_STEP * t) * bq, bq)
        kaug = kaug_scr[pl.ds(start, DIL_STEP * bq), :]
        vaug = vaug_scr[pl.ds(start, DIL_STEP * bq), :]
        bias = bias_ref[t]
        s = lax.dot_general(_stacked_q(qaug_scr, q_start, bq), kaug, _NT, preferred_element_type=F32)
        parts = _lane_parts(jnp.concatenate([s[0:bq, :] + bias, s[bq:, :] + bias], axis=0))
        if fast:
            _plain_step(parts, vaug, l_scr, acc_scr)
        else:
            _softmax_step(parts, vaug, m_scr, l_scr, acc_scr)

    n_steps = (jnp.minimum(i, _dilated_n_blocks() - 1) + DIL_STEP) // DIL_STEP

    def run(fast):
        def body(t, carry):
            step(t, fast)
            return carry

        lax.fori_loop(0, n_steps, body, 0)

    pl.when(flag_ref[0] == 1)(functools.partial(run, True))
    pl.when(flag_ref[0] != 1)(functools.partial(run, False))
    _softmax_finish(o_ref, l_scr, acc_scr)


def _dilated(h3, bias_tab, n_pairs):
    b, seq, _ = h3.shape
    sec_blocks = (AB_WIDTH // 2) // LANES
    base = 3 * sec_blocks
    nq = seq // ATT_BLOCK
    pad = (DIL_STEP - 1) * ATT_BLOCK
    return pl.pallas_call(
        functools.partial(_dilated_kernel, seq=seq),
        grid=(b, n_pairs, nq),
        in_specs=[pl.BlockSpec((1, seq, LANES), lambda bb, p, i: (bb, 0, base + p)),
                  pl.BlockSpec((1, seq, LANES), lambda bb, p, i: (bb, 0, base + sec_blocks + p)),
                  pl.BlockSpec((1, seq, LANES), lambda bb, p, i: (bb, 0, base + 2 * sec_blocks + p)),
                  pl.BlockSpec(bias_tab.shape, lambda bb, p, i: (0, 0, 0))],
        out_specs=pl.BlockSpec((1, ATT_BLOCK, LANES), lambda bb, p, i: (bb, i, p)),
        out_shape=jax.ShapeDtypeStruct((b, seq, n_pairs * LANES), BF16),
        scratch_shapes=[pltpu.VMEM((seq + pad, 2 * LANES), BF16),
                        pltpu.VMEM((seq + pad, 2 * LANES), BF16),
                        pltpu.VMEM((2, seq, 2 * LANES), BF16),
                        pltpu.SMEM((1,), jnp.int32),
                        pltpu.VMEM((2 * ATT_BLOCK, LANES), F32),
                        pltpu.VMEM((2 * ATT_BLOCK, LANES), F32),
                        pltpu.VMEM((2 * ATT_BLOCK, LANES), F32)],
        compiler_params=_params("parallel", "parallel", "arbitrary"),
        name="dilated",
    )(h3, h3, h3, bias_tab)


def _proj_c_kernel(x_ref, w_ref, loglb_ref, log1mlb_ref, omlb_ref, q_ref, lf_ref, kk_ref, v_ref, g_ref):
    xb = x_ref[...].astype(BF16)
    d = D_MODEL

    def sec(c):
        return jnp.dot(xb, w_ref[:, c * d:(c + 1) * d], preferred_element_type=F32)

    q_ref[...] = sec(0)
    z = sec(1)
    log_sig = jnp.minimum(z, 0.0) - jnp.log1p(jnp.exp(-jnp.abs(z)))
    a = loglb_ref[...]
    c = log1mlb_ref[...] + log_sig
    lf_ref[...] = jnp.maximum(a, c) + jnp.log1p(jnp.exp(-jnp.abs(a - c)))
    kk_ref[...] = omlb_ref[...] / (1.0 + jnp.exp(z))
    v_ref[...] = sec(2).astype(v_ref.dtype)
    g_ref[...] = sec(3)


def _proj_c(x2d, w_bf16, lb, tm=512):
    t, d = x2d.shape
    n = w_bf16.shape[1]
    lb = lb.astype(F32).reshape(1, d)
    vec_spec = pl.BlockSpec((1, d), lambda i: (0, 0))
    out_spec = pl.BlockSpec((tm, d), lambda i: (i, 0))
    sds = jax.ShapeDtypeStruct((t, d), F32)
    return pl.pallas_call(
        _proj_c_kernel,
        grid=(t // tm,),
        in_specs=[pl.BlockSpec((tm, d), lambda i: (i, 0)),
                  pl.BlockSpec((d, n), lambda i: (0, 0)),
                  vec_spec, vec_spec, vec_spec],
        out_specs=[out_spec] * 5,
        out_shape=[sds, sds, sds, jax.ShapeDtypeStruct((t, d), BF16), sds],
        compiler_params=_params("parallel"),
        name="proj_c",
    )(x2d, w_bf16, jnp.log(lb), jnp.log1p(-lb), 1.0 - lb)


def _split3_bf16(x):
    h1 = x.astype(BF16)
    r1 = x - h1.astype(F32)
    h2 = r1.astype(BF16)
    h3 = (r1 - h2.astype(F32)).astype(BF16)
    return h1, h2, h3


def _hgrn_kernel(q_ref, lf_ref, kk_ref, v_ref, g_ref, ng_ref, o_ref, *, seq):
    C, SUB = HGRN_CHUNK, HGRN_SUB
    n_sub = C // SUB
    ri = lax.broadcasted_iota(jnp.int32, (C, C), 0)
    ci = lax.broadcasted_iota(jnp.int32, (C, C), 1)
    tri = jnp.where(ci <= ri, 1.0, 0.0).astype(BF16)
    row_c = lax.broadcasted_iota(jnp.int32, (C, LANES), 0)
    row_s = lax.broadcasted_iota(jnp.int32, (SUB, C), 0)
    lane_s = lax.broadcasted_iota(jnp.int32, (SUB, C), 1)
    ng = ng_ref[...]

    def prefix_shifts(x):
        for sh in (1, 2, 4):
            x = x + jnp.where(row_c >= sh, pltpu.roll(x, sh, 0), 0.0)
        sh = 8
        while sh < C:
            x = x + jnp.concatenate([jnp.zeros((sh, LANES), F32), x[:C - sh, :]], axis=0)
            sh *= 2
        return x

    def prefix_matmul(x):
        l1, l2, l3 = _split3_bf16(x)
        return (jnp.dot(tri, l1, preferred_element_type=F32) + jnp.dot(tri, l2, preferred_element_type=F32)
                + jnp.dot(tri, l3, preferred_element_type=F32))

    def scores_factored(qc, kc, b):
        refs = [jnp.zeros((1, LANES), F32)] + [b[s * SUB - 1:s * SUB, :] for s in range(1, n_sub)]
        ref_rows = jnp.concatenate([jnp.broadcast_to(r, (SUB, LANES)) for r in refs], axis=0)
        qt = (qc * jnp.exp(b - ref_rows)).astype(BF16)
        rows_a = []
        for s in range(n_sub):
            hi = (s + 1) * SUB
            kt = (kc[:hi, :] * jnp.exp(refs[s] - b[:hi, :])).astype(BF16)
            a = lax.dot_general(qt[s * SUB:hi, :], kt, _NT, preferred_element_type=F32)
            if hi < C:
                a = jnp.concatenate([a, jnp.zeros((SUB, C - hi), F32)], axis=1)
            rows_a.append(a)
        return jnp.where(ci <= ri, jnp.concatenate(rows_a, axis=0), 0.0)

    def scores_direct(qc, kc, b):
        rows_a = []
        for sidx in range(n_sub):
            lo = sidx * SUB
            q_i = qc[lo:lo + SUB, :]
            b_i = b[lo:lo + SUB, :]
            if sidx == 0:
                a_blk = jnp.zeros((SUB, C), F32)
            else:
                ref = b[lo - 1:lo, :]
                qt = (q_i * jnp.exp(b_i - ref)).astype(BF16)
                kt = jnp.where(row_c < lo, kc * jnp.exp(jnp.minimum(ref - b, 0.0)), 0.0).astype(BF16)
                a_blk = lax.dot_general(qt, kt, _NT, preferred_element_type=F32)
            for j in range(SUB):
                r_lo = 0 if j < 8 else 8
                bj = b[lo + j:lo + j + 1, :]
                kj = kc[lo + j:lo + j + 1, :]
                pj = q_i[r_lo:, :] * (jnp.exp(jnp.minimum(b_i[r_lo:, :] - bj, 0.0)) * kj)
                col = jnp.sum(pj, axis=1, keepdims=True)
                if r_lo:
                    col = jnp.concatenate([jnp.zeros((r_lo, 1), F32), col], axis=0)
                a_blk = jnp.where((lane_s == lo + j) & (row_s >= j), col, a_blk)
            rows_a.append(a_blk)
        return jnp.concatenate(rows_a, axis=0)

    def make_chunk(scores, prefix):
        def chunk(c, st):
            r0 = pl.multiple_of(c * C, C)
            qc = q_ref[0, pl.ds(r0, C), :]
            kc = kk_ref[0, pl.ds(r0, C), :]
            vb = v_ref[0, pl.ds(r0, C), :].astype(BF16)
            b = prefix(lf_ref[0, pl.ds(r0, C), :])
            b_last = b[C - 1:C, :]
            inter = lax.dot_general((qc * jnp.exp(b)).astype(BF16), st.astype(BF16), _NT,
                                    preferred_element_type=F32)
            o = inter + jnp.dot(scores(qc, kc, b).astype(BF16), vb, preferred_element_type=F32)
            o = o * lax.rsqrt(jnp.mean(o * o, axis=1, keepdims=True) + RMS_EPS) * ng
            gc = g_ref[0, pl.ds(r0, C), :]
            o_ref[0, pl.ds(r0, C), :] = (o * (gc / (1.0 + jnp.exp(-gc)))).astype(o_ref.dtype)
            kd = (kc * jnp.exp(b_last - b)).astype(BF16)
            return st * jnp.exp(b_last) + lax.dot_general(vb, kd, _TN, preferred_element_type=F32)
        return chunk

    sub_decay = jnp.sum(lf_ref[0].reshape(seq // SUB, SUB, LANES), axis=1)
    fast_ok = jnp.min(sub_decay) >= -HGRN_FAST_LIMIT
    st0 = jnp.zeros((LANES, LANES), F32)

    @pl.when(fast_ok)
    def _():
        lax.fori_loop(0, seq // C, make_chunk(scores_factored, prefix_shifts), st0, unroll=HGRN_UNROLL)

    @pl.when(jnp.logical_not(fast_ok))
    def _():
        lax.fori_loop(0, seq // C, make_chunk(scores_direct, prefix_matmul), st0)


def _hgrn(q, lf, kk, v, g, norm_g, batch, seq):
    shp = (batch, seq, D_MODEL)
    args = [a.reshape(shp) for a in (q, lf, kk, v, g)]
    spec = pl.BlockSpec((1, seq, LANES), lambda bb, h: (bb, 0, h))
    return pl.pallas_call(
        functools.partial(_hgrn_kernel, seq=seq),
        grid=(batch, N_HEADS_C),
        in_specs=[spec] * 5 + [pl.BlockSpec((1, LANES), lambda bb, h: (0, 0))],
        out_specs=spec,
        out_shape=jax.ShapeDtypeStruct(shp, BF16),
        compiler_params=_params("parallel", "parallel"),
        name="hgrn",
    )(*args, norm_g.astype(F32).reshape(1, LANES))


def _layer_norm_rows(z, g, b):
    mu = jnp.mean(z, axis=1, keepdims=True)
    zc = z - mu
    var = jnp.mean(zc * zc, axis=1, keepdims=True)
    return zc * lax.rsqrt(var + LN_EPS) * g + b


def _first_lane_of_max(vals, lane_f):
    mx = jnp.max(vals, axis=1, keepdims=True)
    return mx, jnp.min(jnp.where(vals == mx, lane_f, float(LANES)), axis=1, keepdims=True)


def _outproj_ln_kernel(*refs, n_parts):
    o_refs, w_refs = refs[:n_parts], refs[n_parts:2 * n_parts]
    (x_ref, g_ref, b_ref, wr_hi_ref, wr_lo_ref, rb_ref,
     y_ref, yb_ref, fields_ref, info_ref, cnt_ref) = refs[2 * n_parts:]
    mix = jnp.dot(o_refs[0][...], w_refs[0][...], preferred_element_type=F32)
    for o_ref, w_ref in zip(o_refs[1:], w_refs[1:]):
        mix += jnp.dot(o_ref[...], w_ref[...], preferred_element_type=F32)
    y = _layer_norm_rows(DEEPNORM_ALPHA * x_ref[...] + mix, g_ref[...], b_ref[...])
    y_ref[...] = y
    y_hi = y.astype(BF16)
    yb_ref[...] = y_hi
    y_lo = (y - y_hi.astype(F32)).astype(BF16)
    lg = (jnp.dot(y_hi, wr_hi_ref[...], preferred_element_type=F32)
          + jnp.dot(y_lo, wr_hi_ref[...], preferred_element_type=F32)
          + jnp.dot(y_hi, wr_lo_ref[...], preferred_element_type=F32)
          + rb_ref[...])
    lane = lax.broadcasted_iota(jnp.int32, lg.shape, 1)
    lane_f = lane.astype(F32)
    is_g = lane < N_GROUPS
    mg, grp = _first_lane_of_max(jnp.where(is_g, lg, -jnp.inf), lane_f)
    pg = 1.0 / jnp.sum(jnp.where(is_g, jnp.exp(lg - mg), 0.0), axis=1, keepdims=True)
    lo = float(N_GROUPS) + float(EXPERTS_PER_GROUP) * grp
    le = jnp.where((lane_f >= lo) & (lane_f < lo + float(EXPERTS_PER_GROUP)), lg, -jnp.inf)
    m1, i1 = _first_lane_of_max(le, lane_f)
    m2, i2 = _first_lane_of_max(jnp.where(lane_f == i1, -jnp.inf, le), lane_f)
    e21 = jnp.exp(m2 - m1)
    w0 = pg / (1.0 + e21)
    w1 = pg * e21 / (1.0 + e21)
    eid0 = i1 - float(N_GROUPS)
    eid1 = i2 - float(N_GROUPS)

    @pl.when(pl.program_id(0) == 0)
    def _():
        cnt_ref[...] = jnp.zeros(cnt_ref.shape, F32)

    hits = jnp.where(lane_f == eid0, 1.0, 0.0) + jnp.where(lane_f == eid1, 1.0, 0.0)
    cnt_ref[...] += jnp.sum(hits, axis=0, keepdims=True)
    fields = jnp.where(lane == 0, eid0, jnp.where(lane == 1, eid1, jnp.where(lane == 2, w0,
                       jnp.where(lane == 3, w1, 0.0))))
    fields_ref[...] = fields
    info_ref[...] = fields.T[:8, :]


def _outproj_ln(o_parts, x2d, w_bf16, ln_g, ln_b, wr, rb, tm=512):
    t, d = x2d.shape
    wr_hi = wr.astype(BF16)
    wr_lo = (wr - wr_hi.astype(F32)).astype(BF16)
    vec = lambda n: pl.BlockSpec((1, n), lambda i: (0, 0))
    widths = [o.shape[1] for o in o_parts]
    offs = np.cumsum([0] + widths)
    w_parts = [w_bf16[offs[k]:offs[k + 1]] for k in range(len(widths))]
    return pl.pallas_call(
        functools.partial(_outproj_ln_kernel, n_parts=len(widths)),
        grid=(t // tm,),
        in_specs=[pl.BlockSpec((tm, wd), lambda i: (i, 0)) for wd in widths]
                 + [pl.BlockSpec((wd, d), lambda i: (0, 0)) for wd in widths]
                 + [pl.BlockSpec((tm, d), lambda i: (i, 0)),
                  vec(d), vec(d),
                  pl.BlockSpec((d, LANES), lambda i: (0, 0)),
                  pl.BlockSpec((d, LANES), lambda i: (0, 0)),
                  vec(LANES)],
        out_specs=[pl.BlockSpec((tm, d), lambda i: (i, 0)),
                   pl.BlockSpec((tm, d), lambda i: (i, 0)),
                   pl.BlockSpec((tm, LANES), lambda i: (i, 0)),
                   pl.BlockSpec((8, tm), lambda i: (0, i)),
                   vec(LANES)],
        out_shape=[jax.ShapeDtypeStruct((t, d), F32),
                   jax.ShapeDtypeStruct((t, d), BF16),
                   jax.ShapeDtypeStruct((t, LANES), F32),
                   jax.ShapeDtypeStruct((8, t), F32),
                   jax.ShapeDtypeStruct((1, LANES), F32)],
        compiler_params=_params("arbitrary"),
        name="outproj_ln",
    )(*o_parts, *w_parts, x2d, ln_g.reshape(1, d), ln_b.reshape(1, d), wr_hi, wr_lo, rb)


def _expert_kernel(te_ref, nv_ref, xs_ref, w1_ref, w3_ref, w2_ref, o_ref, w1_s, w3_s, w2_s):
    i = pl.program_id(0)
    prev = te_ref[jnp.maximum(i - 1, 0)]

    @pl.when((i == 0) | (te_ref[i] != prev))
    def _cast_weights():
        w1_s[...] = w1_ref[0, 0].astype(BF16)
        w3_s[...] = w3_ref[0, 0].astype(BF16)
        w2_s[...] = w2_ref[0, 0].astype(BF16)

    @pl.when(i < nv_ref[0])
    def _ffn():
        for r in range(0, MOE_TILE, MOE_SLAB):
            xb = xs_ref[r:r + MOE_SLAB, :]
            a = jnp.dot(xb, w1_s[...], preferred_element_type=F32)
            u = jnp.dot(xb, w3_s[...], preferred_element_type=F32)
            hb = (a / (1.0 + jnp.exp(-a))) * u
            o_ref[r:r + MOE_SLAB, :] = jnp.dot(hb.astype(BF16), w2_s[...],
                                               preferred_element_type=F32).astype(o_ref.dtype)

    @pl.when(i >= nv_ref[0])
    def _pad():
        o_ref[...] = jnp.zeros(o_ref.shape, o_ref.dtype)


def _expert_ffn(tile_expert, n_valid, xs, w1, w3, w2, layer):
    p, d = xs.shape
    f = w1.shape[3]
    n_tiles = p // MOE_TILE
    grid_spec = pltpu.PrefetchScalarGridSpec(
        num_scalar_prefetch=2,
        grid=(n_tiles,),
        in_specs=[pl.BlockSpec((MOE_TILE, d), lambda i, te, nv: (i, 0)),
                  pl.BlockSpec((1, 1, d, f), lambda i, te, nv: (layer, te[i], 0, 0)),
                  pl.BlockSpec((1, 1, d, f), lambda i, te, nv: (layer, te[i], 0, 0)),
                  pl.BlockSpec((1, 1, f, d), lambda i, te, nv: (layer, te[i], 0, 0))],
        out_specs=pl.BlockSpec((MOE_TILE, d), lambda i, te, nv: (i, 0)),
        scratch_shapes=[pltpu.VMEM((d, f), BF16), pltpu.VMEM((d, f), BF16), pltpu.VMEM((f, d), BF16)],
    )
    return pl.pallas_call(
        _expert_kernel,
        grid_spec=grid_spec,
        out_shape=jax.ShapeDtypeStruct((p, d), BF16),
        compiler_params=_params("arbitrary"),
        name="expert_ffn",
    )(tile_expert, n_valid, xs, w1, w3, w2)


def _combine_ln_kernel(x_ref, y0_ref, y1_ref, fields_ref, g_ref, b_ref, o_ref):
    fields = fields_ref[...]
    ffn = fields[:, 2:3] * y0_ref[...].astype(F32) + fields[:, 3:4] * y1_ref[...].astype(F32)
    o_ref[...] = _layer_norm_rows(DEEPNORM_ALPHA * x_ref[...] + ffn, g_ref[...], b_ref[...])


def _combine_ln(x2d, y0, y1, fields, ln_g, ln_b, tm=512):
    t, d = x2d.shape
    row = pl.BlockSpec((tm, d), lambda i: (i, 0))
    vec = pl.BlockSpec((1, d), lambda i: (0, 0))
    return pl.pallas_call(
        _combine_ln_kernel,
        grid=(t // tm,),
        in_specs=[row, row, row, pl.BlockSpec((tm, LANES), lambda i: (i, 0)), vec, vec],
        out_specs=row,
        out_shape=jax.ShapeDtypeStruct((t, d), F32),
        compiler_params=_params("parallel"),
        name="combine_ln",
    )(x2d, y0, y1, fields, ln_g.reshape(1, d), ln_b.reshape(1, d))


def _dispatch_plan(info, counts_f):
    t = info.shape[1]
    a = t * TOPK_IN_GROUP
    n_fill = N_EXPERTS * MOE_TILE
    n_tiles = a // MOE_TILE + N_EXPERTS
    experts = jnp.arange(N_EXPERTS, dtype=jnp.int32)
    counts = counts_f[0, :N_EXPERTS].astype(jnp.int32)
    padded = (counts + MOE_TILE - 1) // MOE_TILE * MOE_TILE
    pends = jnp.cumsum(padded)
    fill_ends = jnp.cumsum(padded - counts)
    fill = jnp.arange(n_fill, dtype=jnp.int32)
    fill_key = jnp.sum((fill_ends[None, :] <= fill[:, None]).astype(jnp.int32), axis=1)
    tok = jnp.arange(t, dtype=jnp.int32)
    keys = jnp.concatenate([info[0].astype(jnp.int32), info[1].astype(jnp.int32), fill_key])
    toks = jnp.concatenate([tok, tok, fill % t])
    flat = jnp.arange(a + n_fill, dtype=jnp.int32)
    _, slot_tok, slot_flat = lax.sort((keys, toks, flat), num_keys=1, is_stable=True)
    _, slot_of = lax.sort((slot_flat, flat), num_keys=1)
    tile_start = jnp.arange(n_tiles, dtype=jnp.int32) * MOE_TILE
    tile_expert = jnp.sum((pends[None, :] <= tile_start[:, None]).astype(jnp.int32), axis=1)
    last_used = jnp.max(jnp.where(counts > 0, experts, 0))
    tile_expert = jnp.minimum(tile_expert, last_used)
    n_valid = (pends[-1] // MOE_TILE).astype(jnp.int32).reshape(1)
    return tile_expert, n_valid, slot_tok, slot_of[:t], slot_of[t:a]


def _moe_and_norm(x1, x1_bf16, fields, info, counts_f, w1, w3, w2, layer, ln_g, ln_b):
    tile_expert, n_valid, slot_tok, dest0, dest1 = _dispatch_plan(info, counts_f)
    yb = _expert_ffn(tile_expert, n_valid, x1_bf16[slot_tok], w1, w3, w2, layer)
    return _combine_ln(x1, yb[dest0], yb[dest1], fields, ln_g, ln_b)


def kernel(x, ab_w_in, ab_w_out, c_w_in, c_w_out, c_norm_g, hgrn_lb_logits, ln_g, ln_b,
           router_g_w, router_g_b, router_e_w, router_e_b, exp_w1, exp_w3, exp_w2):
    batch, seq, d = x.shape
    t = batch * seq
    tables = _rope_lane_tables(seq)
    bias_tab = jnp.asarray(_dilated_bias_table())
    lb_all = jnp.cumsum(jax.nn.softmax(hgrn_lb_logits.astype(F32), axis=0), axis=0)
    lb_all = lb_all - lb_all[0:1]
    n_pairs = (AB_WIDTH // 2) // LANES

    xc = x.reshape(t, d)
    for l in range(DEPTH):
        j = l // 2
        if l % 2 == 0:
            h = _proj_ab(xc, ab_w_in[j].astype(BF16), tables, seq).reshape(batch, seq, 3 * AB_WIDTH)
            o_parts = [_moba(h, n_pairs).reshape(t, AB_WIDTH // 2),
                       _dilated(h, bias_tab, n_pairs).reshape(t, AB_WIDTH // 2)]
            w_out = ab_w_out[j]
        else:
            q, lf, kk, v, g = _proj_c(xc, c_w_in[j].astype(BF16), lb_all[j])
            o_parts = [_hgrn(q, lf, kk, v, g, c_norm_g[j], batch, seq).reshape(t, d)]
            w_out = c_w_out[j]
        wr = jnp.zeros((d, LANES), F32)
        wr = wr.at[:, :N_GROUPS].set(router_g_w[l]).at[:, N_GROUPS:N_GROUPS + N_EXPERTS].set(router_e_w[l])
        rb = jnp.zeros((1, LANES), F32)
        rb = rb.at[0, :N_GROUPS].set(router_g_b[l]).at[0, N_GROUPS:N_GROUPS + N_EXPERTS].set(router_e_b[l])
        x1, x1_bf16, fields, info, counts_f = _outproj_ln(o_parts, xc, w_out.astype(BF16), ln_g[l, 0], ln_b[l, 0],
                                                          wr, rb)
        xc = _moe_and_norm(x1, x1_bf16, fields, info, counts_f, exp_w1, exp_w3, exp_w2, l,
                           ln_g[l, 1], ln_b[l, 1])
    return xc.reshape(batch, seq, d)
```

```python
import functools
import math

import numpy as np
import jax
import jax.numpy as jnp
from jax import lax
from jax.experimental import pallas as pl
from jax.experimental.pallas import tpu as pltpu

F32 = jnp.float32
BF16 = jnp.bfloat16

D_MODEL = 1024
DEPTH = 4
HEAD_DIM = 64
N_HEADS_A = 8
N_HEADS_B = 8
AB_WIDTH = (N_HEADS_A + N_HEADS_B) * HEAD_DIM
ROT_DIM = HEAD_DIM // 4
ROPE_THETA = 500000.0
MOBA_BLOCK = 256
MOBA_TOPK = 3
DILATED_BRANCHES = ((128, 1), (512, 4), (2048, 16))
HGRN_EXPAND = 128
N_HEADS_C = D_MODEL // HGRN_EXPAND
HGRN_CHUNK = 64
HGRN_SUB = 16
HGRN_UNROLL = 8
HGRN_FAST_LIMIT = 60.0
N_GROUPS = 4
EXPERTS_PER_GROUP = 8
N_EXPERTS = N_GROUPS * EXPERTS_PER_GROUP
TOPK_IN_GROUP = 2
D_EXPERT = D_MODEL // 2
DEEPNORM_ALPHA = (2.0 * DEPTH) ** 0.25
LN_EPS = 1e-5
RMS_EPS = 1e-6

LANES = 128
ATT_BLOCK = 256
MOBA_STEP = 4
MOBA_SEL_ROWS = 16
MOBA_QROWS = 512
SHIFT_LANE = LANES - 2
ATT_FAST_LIMIT = 38.0
DIL_STEP = 3
NEG_BIG = -1e30
MOE_TILE = 512
MOE_SLAB = 512
VMEM_LIMIT = 56 * 1024 * 1024

_NT = (((1,), (1,)), ((), ()))
_TN = (((0,), (0,)), ((), ()))


def _params(*sem):
    return pltpu.CompilerParams(dimension_semantics=sem, vmem_limit_bytes=VMEM_LIMIT)


def _proj_ab_kernel(x_ref, w_ref, cos_ref, sa_ref, sb_ref, o_ref):
    xb = x_ref[...].astype(BF16)
    sec = AB_WIDTH // 2
    for c in range(6):
        acc = jnp.dot(xb, w_ref[:, c * sec:(c + 1) * sec], preferred_element_type=F32)
        if c % 3 == 2:
            o_ref[:, c * sec:(c + 1) * sec] = acc.astype(BF16)
            continue
        cosv = cos_ref[...]
        sa = sa_ref[...]
        sb = sb_ref[...]
        for k in range(sec // LANES):
            a = acc[:, k * LANES:(k + 1) * LANES]
            r = a * cosv + pltpu.roll(a, LANES - ROT_DIM // 2, 1) * sa + pltpu.roll(a, ROT_DIM // 2, 1) * sb
            if c % 3 == 0:
                r = r * (HEAD_DIM ** -0.5)
            o_ref[:, c * sec + k * LANES:c * sec + (k + 1) * LANES] = r.astype(BF16)


def _rope_lane_tables(seq):
    half = ROT_DIM // 2
    inv = ROPE_THETA ** (-jnp.arange(half, dtype=F32) / half)
    ang = jnp.arange(seq, dtype=F32)[:, None] * inv[None, :]
    cos, sin = jnp.cos(ang), jnp.sin(ang)
    ones = jnp.ones((seq, HEAD_DIM - ROT_DIM), F32)
    zeros = jnp.zeros((seq, HEAD_DIM - ROT_DIM), F32)
    zh = jnp.zeros((seq, half), F32)
    cos_t = jnp.concatenate([cos, cos, ones], axis=1)
    sa_t = jnp.concatenate([-sin, zh, zeros], axis=1)
    sb_t = jnp.concatenate([zh, sin, zeros], axis=1)
    rep = LANES // HEAD_DIM
    return tuple(jnp.tile(t, (1, rep)) for t in (cos_t, sa_t, sb_t))


def _proj_ab(x2d, w_bf16, tables, seq, tm=512):
    t, d = x2d.shape
    n = w_bf16.shape[1]
    nseq = seq // tm
    tab_spec = pl.BlockSpec((tm, LANES), lambda i: (i % nseq, 0))
    return pl.pallas_call(
        _proj_ab_kernel,
        grid=(t // tm,),
        in_specs=[pl.BlockSpec((tm, d), lambda i: (i, 0)),
                  pl.BlockSpec((d, n), lambda i: (0, 0)),
                  tab_spec, tab_spec, tab_spec],
        out_specs=pl.BlockSpec((tm, n), lambda i: (i, 0)),
        out_shape=jax.ShapeDtypeStruct((t, n), BF16),
        compiler_params=_params("parallel"),
        name="proj_ab",
    )(x2d, w_bf16, *tables)


def _softmax_step(parts, vaug, m_scr, l_scr, acc_scr):
    m_prev = m_scr[...]
    mx = functools.reduce(jnp.maximum, parts)
    m_new = jnp.maximum(m_prev, jnp.max(mx, axis=1, keepdims=True))
    alpha = jnp.exp(m_prev - m_new)
    p = jnp.concatenate([jnp.exp(part - m_new).astype(BF16) for part in parts], axis=1)
    pv = jnp.dot(p, vaug, preferred_element_type=F32)
    acc_scr[...] = alpha * acc_scr[...] + pv[:, :LANES]
    l_scr[...] = alpha * l_scr[...] + pv[:, LANES:]
    m_scr[...] = m_new


def _plain_step(parts, vaug, l_scr, acc_scr):
    p = jnp.concatenate([jnp.exp(part).astype(BF16) for part in parts], axis=1)
    pv = jnp.dot(p, vaug, preferred_element_type=F32)
    acc_scr[...] += pv[:, :LANES]
    l_scr[...] += pv[:, LANES:]


def _stacked_q(qaug_scr, q_start, bq):
    return jnp.concatenate([qaug_scr[0, pl.ds(q_start, bq), :], qaug_scr[1, pl.ds(q_start, bq), :]], axis=0)


def _lane_parts(s):
    return [s[:, c * LANES:(c + 1) * LANES] for c in range(s.shape[1] // LANES)]


def _head_mask(lane, h):
    return (lane >= h * HEAD_DIM) & (lane < (h + 1) * HEAD_DIM)


def _head_sq_norms(x, h):
    lane = lax.broadcasted_iota(jnp.int32, x.shape, 1)
    xf = x.astype(F32)
    sq = jnp.where(_head_mask(lane, h), xf * xf, 0.0).astype(BF16)
    return jnp.dot(sq, jnp.ones((LANES, LANES), BF16), preferred_element_type=F32)


def _score_shifts(q, k, flag_ref):
    bounds = [jnp.sqrt(jnp.max(_head_sq_norms(q, h), axis=0, keepdims=True)
                       * jnp.max(_head_sq_norms(k, h), axis=0, keepdims=True)) for h in range(2)]
    fast_ok = jnp.max(jnp.maximum(bounds[0], bounds[1])) <= ATT_FAST_LIMIT
    flag_ref[0] = jnp.where(fast_ok, 1, 0).astype(jnp.int32)
    return [jnp.where(fast_ok, -b, 0.0) for b in bounds]


def _softmax_init(m_scr, l_scr, acc_scr):
    m_scr[...] = jnp.full(m_scr.shape, -jnp.inf, F32)
    l_scr[...] = jnp.zeros(l_scr.shape, F32)
    acc_scr[...] = jnp.zeros(acc_scr.shape, F32)


def _softmax_finish(o_ref, l_scr, acc_scr):
    bq = acc_scr.shape[0] // 2
    lane = lax.broadcasted_iota(jnp.int32, (bq, LANES), 1)
    o0 = acc_scr[0:bq, :] / l_scr[0:bq, :]
    o1 = acc_scr[bq:, :] / l_scr[bq:, :]
    o_ref[0] = jnp.where(lane < HEAD_DIM, o0, o1).astype(o_ref.dtype)


def _moba_kernel(q_ref, k_ref, v_ref, o_ref, kaug_scr, vaug_scr, qaug_scr, flag_ref, m_scr, l_scr, acc_scr, *, seq):
    bq = ATT_BLOCK
    pad = (MOBA_STEP - 1) * bq
    i = pl.program_id(2)

    @pl.when(i == 0)
    def _per_sequence_setup():
        q = q_ref[0]
        k = k_ref[0]
        rows = lax.broadcasted_iota(jnp.int32, (MOBA_SEL_ROWS, seq), 0)
        cols = lax.broadcasted_iota(jnp.int32, (MOBA_SEL_ROWS, seq), 1)
        own = jnp.right_shift(cols, int(math.log2(MOBA_BLOCK)))
        ind = jnp.where(own == rows, 1.0, 0.0).astype(BF16)
        km = jnp.dot(ind, k, preferred_element_type=F32) * (1.0 / MOBA_BLOCK)
        km_hi = km.astype(BF16)
        km_lo = (km - km_hi.astype(F32)).astype(BF16)
        shifts = _score_shifts(q, k, flag_ref)
        blk_lane = lax.broadcasted_iota(jnp.int32, (bq, LANES), 1)
        kaug_scr[0:pad, 0:LANES] = jnp.zeros((pad, LANES), BF16)
        vaug_scr[0:pad, 0:LANES] = jnp.zeros((pad, LANES), BF16)
        for jb in range(-(MOBA_STEP - 1), seq // bq):
            tgt = jb if jb >= 0 else LANES - 1
            kaug_scr[pad + jb * bq:pad + (jb + 1) * bq, LANES:] = jnp.where(
                (blk_lane == tgt) | (blk_lane == SHIFT_LANE), 1.0, 0.0).astype(BF16)
        kaug_scr[pad:, 0:LANES] = k
        vaug_scr[pad:, 0:LANES] = v_ref[0]
        vaug_scr[:, LANES:] = jnp.ones((seq + pad, LANES), BF16)
        lane_all = lax.broadcasted_iota(jnp.int32, (seq, LANES), 1)
        rows_f = rows.astype(F32)
        for h in range(2):
            qh = jnp.where(_head_mask(lane_all, h), q, jnp.zeros_like(q))
            g = (lax.dot_general(km_hi, qh, _NT, preferred_element_type=F32)
                 + lax.dot_general(km_lo, qh, _NT, preferred_element_type=F32))
            g = jnp.where(rows < own, g, -jnp.inf)
            bias = jnp.where(rows == own, 0.0, NEG_BIG)
            for _ in range(MOBA_TOPK):
                mx = jnp.max(g, axis=0, keepdims=True)
                first = jnp.min(jnp.where(g == mx, rows_f, float(MOBA_SEL_ROWS)), axis=0, keepdims=True)
                pick = (rows_f == first) & (mx > -jnp.inf)
                bias = jnp.where(pick, 0.0, bias)
                g = jnp.where(pick, -jnp.inf, g)
            bias = jnp.concatenate([bias, jnp.full((LANES - MOBA_SEL_ROWS, seq), NEG_BIG, F32)], axis=0).T
            qaug_scr[h, :, 0:LANES] = qh
            qaug_scr[h, :, LANES:] = jnp.where(lane_all == SHIFT_LANE, shifts[h], bias).astype(BF16)

    qrows = MOBA_QROWS
    qpb = qrows // bq
    last = (i + 1) * qpb - 1
    lane = lax.broadcasted_iota(jnp.int32, (2 * qrows, LANES), 1)
    row = lax.broadcasted_iota(jnp.int32, (2 * qrows, LANES), 0)
    qi = row & (bq - 1)
    qblk = jnp.right_shift(row, int(math.log2(bq))) & (qpb - 1)
    _softmax_init(m_scr, l_scr, acc_scr)
    parts_per_blk = bq // LANES
    q_start = pl.multiple_of(i * qrows, qrows)

    def step(t, first, fast):
        start = pl.multiple_of((last - MOBA_STEP * t) * bq, bq)
        kaug = kaug_scr[pl.ds(start, MOBA_STEP * bq), :]
        vaug = vaug_scr[pl.ds(start, MOBA_STEP * bq), :]
        parts = _lane_parts(lax.dot_general(_stacked_q(qaug_scr, q_start, qrows), kaug, _NT,
                                            preferred_element_type=F32))
        if first:
            for own in range(qpb):
                for c in range(parts_per_blk):
                    cc = (MOBA_STEP - qpb + own) * parts_per_blk + c
                    parts[cc] = jnp.where((qblk == own) & (lane + c * LANES > qi), NEG_BIG, parts[cc])
        if fast:
            _plain_step(parts, vaug, l_scr, acc_scr)
        else:
            _softmax_step(parts, vaug, m_scr, l_scr, acc_scr)

    def run(fast):
        step(0, True, fast)

        def later(t, carry):
            step(t, False, fast)
            return carry

        lax.fori_loop(1, (last + MOBA_STEP) // MOBA_STEP, later, 0)

    pl.when(flag_ref[0] == 1)(functools.partial(run, True))
    pl.when(flag_ref[0] != 1)(functools.partial(run, False))
    _softmax_finish(o_ref, l_scr, acc_scr)


def _moba(h3, n_pairs):
    b, seq, _ = h3.shape
    sec_blocks = (AB_WIDTH // 2) // LANES
    assert seq % MOBA_QROWS == 0 and seq // ATT_BLOCK <= MOBA_SEL_ROWS and MOBA_QROWS // ATT_BLOCK <= MOBA_STEP
    return pl.pallas_call(
        functools.partial(_moba_kernel, seq=seq),
        grid=(b, n_pairs, seq // MOBA_QROWS),
        in_specs=[pl.BlockSpec((1, seq, LANES), lambda bb, p, i: (bb, 0, p)),
                  pl.BlockSpec((1, seq, LANES), lambda bb, p, i: (bb, 0, sec_blocks + p)),
                  pl.BlockSpec((1, seq, LANES), lambda bb, p, i: (bb, 0, 2 * sec_blocks + p))],
        out_specs=pl.BlockSpec((1, MOBA_QROWS, LANES), lambda bb, p, i: (bb, i, p)),
        out_shape=jax.ShapeDtypeStruct((b, seq, n_pairs * LANES), BF16),
        scratch_shapes=[pltpu.VMEM((seq + (MOBA_STEP - 1) * ATT_BLOCK, 2 * LANES), BF16),
                        pltpu.VMEM((seq + (MOBA_STEP - 1) * ATT_BLOCK, 2 * LANES), BF16),
                        pltpu.VMEM((2, seq, 2 * LANES), BF16),
                        pltpu.SMEM((1,), jnp.int32),
                        pltpu.VMEM((2 * MOBA_QROWS, LANES), F32),
                        pltpu.VMEM((2 * MOBA_QROWS, LANES), F32),
                        pltpu.VMEM((2 * MOBA_QROWS, LANES), F32)],
        compiler_params=_params("parallel", "parallel", "arbitrary"),
        name="moba",
    )(h3, h3, h3)


def _dilated_n_blocks():
    return max(w for w, _ in DILATED_BRANCHES) // ATT_BLOCK + 1


def _dilated_bias_table():
    n_steps = -(-_dilated_n_blocks() // DIL_STEP)
    qi = np.arange(ATT_BLOCK)[:, None]
    ki = np.arange(ATT_BLOCK)[None, :]
    tabs = []
    for t in range(n_steps):
        groups = []
        for g in range(DIL_STEP):
            d = qi - ki + (DIL_STEP * t + DIL_STEP - 1 - g) * ATT_BLOCK
            cnt = np.zeros(d.shape, np.int64)
            for window, dil in DILATED_BRANCHES:
                cnt += ((d >= 0) & (d <= window) & (d % dil == 0)).astype(np.int64)
            groups.append(np.where(cnt > 0, np.log(np.maximum(cnt, 1).astype(np.float64)), NEG_BIG))
        tabs.append(np.concatenate(groups, axis=1))
    return np.stack(tabs).astype(np.float32)


def _dilated_kernel(q_ref, k_ref, v_ref, bias_ref, o_ref, kaug_scr, vaug_scr, qaug_scr, flag_ref, m_scr, l_scr, acc_scr,
                    *, seq):
    bq = ATT_BLOCK
    pad = (DIL_STEP - 1) * bq
    i = pl.program_id(2)

    @pl.when(i == 0)
    def _per_sequence_setup():
        q = q_ref[0]
        k = k_ref[0]
        shifts = _score_shifts(q, k, flag_ref)
        flag_lane = lax.broadcasted_iota(jnp.int32, (pad, LANES), 1)
        lane_all = lax.broadcasted_iota(jnp.int32, (seq, LANES), 1)
        kaug_scr[0:pad, 0:LANES] = jnp.zeros((pad, LANES), BF16)
        kaug_scr[0:pad, LANES:] = jnp.where((flag_lane == 0) | (flag_lane == SHIFT_LANE), 1.0, 0.0).astype(BF16)
        kaug_scr[pad:, 0:LANES] = k
        kaug_scr[pad:, LANES:] = jnp.where(lane_all == SHIFT_LANE, 1.0, 0.0).astype(BF16)
        vaug_scr[0:pad, 0:LANES] = jnp.zeros((pad, LANES), BF16)
        vaug_scr[pad:, 0:LANES] = v_ref[0]
        vaug_scr[:, LANES:] = jnp.ones((seq + pad, LANES), BF16)
        for h in range(2):
            aug = jnp.where(lane_all == SHIFT_LANE, shifts[h], jnp.where(lane_all == 0, NEG_BIG, 0.0))
            qaug_scr[h, :, 0:LANES] = jnp.where(_head_mask(lane_all, h), q, jnp.zeros_like(q))
            qaug_scr[h, :, LANES:] = aug.astype(BF16)

    _softmax_init(m_scr, l_scr, acc_scr)
    q_start = pl.multiple_of(i * bq, bq)

    def step(t, fast):
        start = pl.multiple_of((i - DIL_STEP * t) * bq, bq)
        kaug = kaug_scr[pl.ds(start, DIL_STEP * bq), :]
        vaug = vaug_scr[pl.ds(start, DIL_STEP * bq), :]
        bias = bias_ref[t]
        s = lax.dot_general(_stacked_q(qaug_scr, q_start, bq), kaug, _NT, preferred_element_type=F32)
        parts = _lane_parts(jnp.concatenate([s[0:bq, :] + bias, s[bq:, :] + bias], axis=0))
        if fast:
            _plain_step(parts, vaug, l_scr, acc_scr)
        else:
            _softmax_step(parts, vaug, m_scr, l_scr, acc_scr)

    n_steps = (jnp.minimum(i, _dilated_n_blocks() - 1) + DIL_STEP) // DIL_STEP

    def run(fast):
        def body(t, carry):
            step(t, fast)
            return carry

        lax.fori_loop(0, n_steps, body, 0)

    pl.when(flag_ref[0] == 1)(functools.partial(run, True))
    pl.when(flag_ref[0] != 1)(functools.partial(run, False))
    _softmax_finish(o_ref, l_scr, acc_scr)


def _dilated(h3, bias_tab, n_pairs):
    b, seq, _ = h3.shape
    sec_blocks = (AB_WIDTH // 2) // LANES
    base = 3 * sec_blocks
    nq = seq // ATT_BLOCK
    pad = (DIL_STEP - 1) * ATT_BLOCK
    return pl.pallas_call(
        functools.partial(_dilated_kernel, seq=seq),
        grid=(b, n_pairs, nq),
        in_specs=[pl.BlockSpec((1, seq, LANES), lambda bb, p, i: (bb, 0, base + p)),
                  pl.BlockSpec((1, seq, LANES), lambda bb, p, i: (bb, 0, base + sec_blocks + p)),
                  pl.BlockSpec((1, seq, LANES), lambda bb, p, i: (bb, 0, base + 2 * sec_blocks + p)),
                  pl.BlockSpec(bias_tab.shape, lambda bb, p, i: (0, 0, 0))],
        out_specs=pl.BlockSpec((1, ATT_BLOCK, LANES), lambda bb, p, i: (bb, i, p)),
        out_shape=jax.ShapeDtypeStruct((b, seq, n_pairs * LANES), BF16),
        scratch_shapes=[pltpu.VMEM((seq + pad, 2 * LANES), BF16),
                        pltpu.VMEM((seq + pad, 2 * LANES), BF16),
                        pltpu.VMEM((2, seq, 2 * LANES), BF16),
                        pltpu.SMEM((1,), jnp.int32),
                        pltpu.VMEM((2 * ATT_BLOCK, LANES), F32),
                        pltpu.VMEM((2 * ATT_BLOCK, LANES), F32),
                        pltpu.VMEM((2 * ATT_BLOCK, LANES), F32)],
        compiler_params=_params("parallel", "parallel", "arbitrary"),
        name="dilated",
    )(h3, h3, h3, bias_tab)


def _proj_c_kernel(x_ref, w_ref, loglb_ref, log1mlb_ref, omlb_ref, q_ref, lf_ref, kk_ref, v_ref, g_ref):
    xb = x_ref[...].astype(BF16)
    d = D_MODEL

    def sec(c):
        return jnp.dot(xb, w_ref[:, c * d:(c + 1) * d], preferred_element_type=F32)

    q_ref[...] = sec(0)
    z = sec(1)
    log_sig = jnp.minimum(z, 0.0) - jnp.log1p(jnp.exp(-jnp.abs(z)))
    a = loglb_ref[...]
    c = log1mlb_ref[...] + log_sig
    lf_ref[...] = jnp.maximum(a, c) + jnp.log1p(jnp.exp(-jnp.abs(a - c)))
    kk_ref[...] = omlb_ref[...] / (1.0 + jnp.exp(z))
    v_ref[...] = sec(2).astype(v_ref.dtype)
    g_ref[...] = sec(3)


def _proj_c(x2d, w_bf16, lb, tm=512):
    t, d = x2d.shape
    n = w_bf16.shape[1]
    lb = lb.astype(F32).reshape(1, d)
    vec_spec = pl.BlockSpec((1, d), lambda i: (0, 0))
    out_spec = pl.BlockSpec((tm, d), lambda i: (i, 0))
    sds = jax.ShapeDtypeStruct((t, d), F32)
    return pl.pallas_call(
        _proj_c_kernel,
        grid=(t // tm,),
        in_specs=[pl.BlockSpec((tm, d), lambda i: (i, 0)),
                  pl.BlockSpec((d, n), lambda i: (0, 0)),
                  vec_spec, vec_spec, vec_spec],
        out_specs=[out_spec] * 5,
        out_shape=[sds, sds, sds, jax.ShapeDtypeStruct((t, d), BF16), sds],
        compiler_params=_params("parallel"),
        name="proj_c",
    )(x2d, w_bf16, jnp.log(lb), jnp.log1p(-lb), 1.0 - lb)


def _split3_bf16(x):
    h1 = x.astype(BF16)
    r1 = x - h1.astype(F32)
    h2 = r1.astype(BF16)
    h3 = (r1 - h2.astype(F32)).astype(BF16)
    return h1, h2, h3


def _hgrn_kernel(q_ref, lf_ref, kk_ref, v_ref, g_ref, ng_ref, o_ref, *, seq):
    C, SUB = HGRN_CHUNK, HGRN_SUB
    n_sub = C // SUB
    ri = lax.broadcasted_iota(jnp.int32, (C, C), 0)
    ci = lax.broadcasted_iota(jnp.int32, (C, C), 1)
    tri = jnp.where(ci <= ri, 1.0, 0.0).astype(BF16)
    row_c = lax.broadcasted_iota(jnp.int32, (C, LANES), 0)
    row_s = lax.broadcasted_iota(jnp.int32, (SUB, C), 0)
    lane_s = lax.broadcasted_iota(jnp.int32, (SUB, C), 1)
    ng = ng_ref[...]

    def prefix_shifts(x):
        for sh in (1, 2, 4):
            x = x + jnp.where(row_c >= sh, pltpu.roll(x, sh, 0), 0.0)
        sh = 8
        while sh < C:
            x = x + jnp.concatenate([jnp.zeros((sh, LANES), F32), x[:C - sh, :]], axis=0)
            sh *= 2
        return x

    def prefix_matmul(x):
        l1, l2, l3 = _split3_bf16(x)
        return (jnp.dot(tri, l1, preferred_element_type=F32) + jnp.dot(tri, l2, preferred_element_type=F32)
                + jnp.dot(tri, l3, preferred_element_type=F32))

    def scores_factored(qc, kc, b):
        refs = [jnp.zeros((1, LANES), F32)] + [b[s * SUB - 1:s * SUB, :] for s in range(1, n_sub)]
        ref_rows = jnp.concatenate([jnp.broadcast_to(r, (SUB, LANES)) for r in refs], axis=0)
        qt = (qc * jnp.exp(b - ref_rows)).astype(BF16)
        rows_a = []
        for s in range(n_sub):
            hi = (s + 1) * SUB
            kt = (kc[:hi, :] * jnp.exp(refs[s] - b[:hi, :])).astype(BF16)
            a = lax.dot_general(qt[s * SUB:hi, :], kt, _NT, preferred_element_type=F32)
            if hi < C:
                a = jnp.concatenate([a, jnp.zeros((SUB, C - hi), F32)], axis=1)
            rows_a.append(a)
        return jnp.where(ci <= ri, jnp.concatenate(rows_a, axis=0), 0.0)

    def scores_direct(qc, kc, b):
        rows_a = []
        for sidx in range(n_sub):
            lo = sidx * SUB
            q_i = qc[lo:lo + SUB, :]
            b_i = b[lo:lo + SUB, :]
            if sidx == 0:
                a_blk = jnp.zeros((SUB, C), F32)
            else:
                ref = b[lo - 1:lo, :]
                qt = (q_i * jnp.exp(b_i - ref)).astype(BF16)
                kt = jnp.where(row_c < lo, kc * jnp.exp(jnp.minimum(ref - b, 0.0)), 0.0).astype(BF16)
                a_blk = lax.dot_general(qt, kt, _NT, preferred_element_type=F32)
            for j in range(SUB):
                r_lo = 0 if j < 8 else 8
                bj = b[lo + j:lo + j + 1, :]
                kj = kc[lo + j:lo + j + 1, :]
                pj = q_i[r_lo:, :] * (jnp.exp(jnp.minimum(b_i[r_lo:, :] - bj, 0.0)) * kj)
                col = jnp.sum(pj, axis=1, keepdims=True)
                if r_lo:
                    col = jnp.concatenate([jnp.zeros((r_lo, 1), F32), col], axis=0)
                a_blk = jnp.where((lane_s == lo + j) & (row_s >= j), col, a_blk)
            rows_a.append(a_blk)
        return jnp.concatenate(rows_a, axis=0)

    def make_chunk(scores, prefix):
        def chunk(c, st):
            r0 = pl.multiple_of(c * C, C)
            qc = q_ref[0, pl.ds(r0, C), :]
            kc = kk_ref[0, pl.ds(r0, C), :]
            vb = v_ref[0, pl.ds(r0, C), :].astype(BF16)
            b = prefix(lf_ref[0, pl.ds(r0, C), :])
            b_last = b[C - 1:C, :]
            inter = lax.dot_general((qc * jnp.exp(b)).astype(BF16), st.astype(BF16), _NT,
                                    preferred_element_type=F32)
            o = inter + jnp.dot(scores(qc, kc, b).astype(BF16), vb, preferred_element_type=F32)
            o = o * lax.rsqrt(jnp.mean(o * o, axis=1, keepdims=True) + RMS_EPS) * ng
            gc = g_ref[0, pl.ds(r0, C), :]
            o_ref[0, pl.ds(r0, C), :] = (o * (gc / (1.0 + jnp.exp(-gc)))).astype(o_ref.dtype)
            kd = (kc * jnp.exp(b_last - b)).astype(BF16)
            return st * jnp.exp(b_last) + lax.dot_general(vb, kd, _TN, preferred_element_type=F32)
        return chunk

    sub_decay = jnp.sum(lf_ref[0].reshape(seq // SUB, SUB, LANES), axis=1)
    fast_ok = jnp.min(sub_decay) >= -HGRN_FAST_LIMIT
    st0 = jnp.zeros((LANES, LANES), F32)

    @pl.when(fast_ok)
    def _():
        lax.fori_loop(0, seq // C, make_chunk(scores_factored, prefix_shifts), st0, unroll=HGRN_UNROLL)

    @pl.when(jnp.logical_not(fast_ok))
    def _():
        lax.fori_loop(0, seq // C, make_chunk(scores_direct, prefix_matmul), st0)


def _hgrn(q, lf, kk, v, g, norm_g, batch, seq):
    shp = (batch, seq, D_MODEL)
    args = [a.reshape(shp) for a in (q, lf, kk, v, g)]
    spec = pl.BlockSpec((1, seq, LANES), lambda bb, h: (bb, 0, h))
    return pl.pallas_call(
        functools.partial(_hgrn_kernel, seq=seq),
        grid=(batch, N_HEADS_C),
        in_specs=[spec] * 5 + [pl.BlockSpec((1, LANES), lambda bb, h: (0, 0))],
        out_specs=spec,
        out_shape=jax.ShapeDtypeStruct(shp, BF16),
        compiler_params=_params("parallel", "parallel"),
        name="hgrn",
    )(*args, norm_g.astype(F32).reshape(1, LANES))


def _layer_norm_rows(z, g, b):
    mu = jnp.mean(z, axis=1, keepdims=True)
    zc = z - mu
    var = jnp.mean(zc * zc, axis=1, keepdims=True)
    return zc * lax.rsqrt(var + LN_EPS) * g + b


def _first_lane_of_max(vals, lane_f):
    mx = jnp.max(vals, axis=1, keepdims=True)
    return mx, jnp.min(jnp.where(vals == mx, lane_f, float(LANES)), axis=1, keepdims=True)


def _outproj_ln_kernel(*refs, n_parts):
    o_refs, w_refs = refs[:n_parts], refs[n_parts:2 * n_parts]
    (x_ref, g_ref, b_ref, wr_hi_ref, wr_lo_ref, rb_ref,
     y_ref, yb_ref, fields_ref, info_ref, cnt_ref) = refs[2 * n_parts:]
    mix = jnp.dot(o_refs[0][...], w_refs[0][...], preferred_element_type=F32)
    for o_ref, w_ref in zip(o_refs[1:], w_refs[1:]):
        mix += jnp.dot(o_ref[...], w_ref[...], preferred_element_type=F32)
    y = _layer_norm_rows(DEEPNORM_ALPHA * x_ref[...] + mix, g_ref[...], b_ref[...])
    y_ref[...] = y
    y_hi = y.astype(BF16)
    yb_ref[...] = y_hi
    y_lo = (y - y_hi.astype(F32)).astype(BF16)
    lg = (jnp.dot(y_hi, wr_hi_ref[...], preferred_element_type=F32)
          + jnp.dot(y_lo, wr_hi_ref[...], preferred_element_type=F32)
          + jnp.dot(y_hi, wr_lo_ref[...], preferred_element_type=F32)
          + rb_ref[...])
    lane = lax.broadcasted_iota(jnp.int32, lg.shape, 1)
    lane_f = lane.astype(F32)
    is_g = lane < N_GROUPS
    mg, grp = _first_lane_of_max(jnp.where(is_g, lg, -jnp.inf), lane_f)
    pg = 1.0 / jnp.sum(jnp.where(is_g, jnp.exp(lg - mg), 0.0), axis=1, keepdims=True)
    lo = float(N_GROUPS) + float(EXPERTS_PER_GROUP) * grp
    le = jnp.where((lane_f >= lo) & (lane_f < lo + float(EXPERTS_PER_GROUP)), lg, -jnp.inf)
    m1, i1 = _first_lane_of_max(le, lane_f)
    m2, i2 = _first_lane_of_max(jnp.where(lane_f == i1, -jnp.inf, le), lane_f)
    e21 = jnp.exp(m2 - m1)
    w0 = pg / (1.0 + e21)
    w1 = pg * e21 / (1.0 + e21)
    eid0 = i1 - float(N_GROUPS)
    eid1 = i2 - float(N_GROUPS)

    @pl.when(pl.program_id(0) == 0)
    def _():
        cnt_ref[...] = jnp.zeros(cnt_ref.shape, F32)

    hits = jnp.where(lane_f == eid0, 1.0, 0.0) + jnp.where(lane_f == eid1, 1.0, 0.0)
    cnt_ref[...] += jnp.sum(hits, axis=0, keepdims=True)
    fields = jnp.where(lane == 0, eid0, jnp.where(lane == 1, eid1, jnp.where(lane == 2, w0,
                       jnp.where(lane == 3, w1, 0.0))))
    fields_ref[...] = fields
    info_ref[...] = fields.T[:8, :]


def _outproj_ln(o_parts, x2d, w_bf16, ln_g, ln_b, wr, rb, tm=512):
    t, d = x2d.shape
    wr_hi = wr.astype(BF16)
    wr_lo = (wr - wr_hi.astype(F32)).astype(BF16)
    vec = lambda n: pl.BlockSpec((1, n), lambda i: (0, 0))
    widths = [o.shape[1] for o in o_parts]
    offs = np.cumsum([0] + widths)
    w_parts = [w_bf16[offs[k]:offs[k + 1]] for k in range(len(widths))]
    return pl.pallas_call(
        functools.partial(_outproj_ln_kernel, n_parts=len(widths)),
        grid=(t // tm,),
        in_specs=[pl.BlockSpec((tm, wd), lambda i: (i, 0)) for wd in widths]
                 + [pl.BlockSpec((wd, d), lambda i: (0, 0)) for wd in widths]
                 + [pl.BlockSpec((tm, d), lambda i: (i, 0)),
                  vec(d), vec(d),
                  pl.BlockSpec((d, LANES), lambda i: (0, 0)),
                  pl.BlockSpec((d, LANES), lambda i: (0, 0)),
                  vec(LANES)],
        out_specs=[pl.BlockSpec((tm, d), lambda i: (i, 0)),
                   pl.BlockSpec((tm, d), lambda i: (i, 0)),
                   pl.BlockSpec((tm, LANES), lambda i: (i, 0)),
                   pl.BlockSpec((8, tm), lambda i: (0, i)),
                   vec(LANES)],
        out_shape=[jax.ShapeDtypeStruct((t, d), F32),
                   jax.ShapeDtypeStruct((t, d), BF16),
                   jax.ShapeDtypeStruct((t, LANES), F32),
                   jax.ShapeDtypeStruct((8, t), F32),
                   jax.ShapeDtypeStruct((1, LANES), F32)],
        compiler_params=_params("arbitrary"),
        name="outproj_ln",
    )(*o_parts, *w_parts, x2d, ln_g.reshape(1, d), ln_b.reshape(1, d), wr_hi, wr_lo, rb)


def _expert_kernel(te_ref, nv_ref, xs_ref, w1_ref, w3_ref, w2_ref, o_ref, w1_s, w3_s, w2_s):
    i = pl.program_id(0)
    prev = te_ref[jnp.maximum(i - 1, 0)]

    @pl.when((i == 0) | (te_ref[i] != prev))
    def _cast_weights():
        w1_s[...] = w1_ref[0, 0].astype(BF16)
        w3_s[...] = w3_ref[0, 0].astype(BF16)
        w2_s[...] = w2_ref[0, 0].astype(BF16)

    @pl.when(i < nv_ref[0])
    def _ffn():
        for r in range(0, MOE_TILE, MOE_SLAB):
            xb = xs_ref[r:r + MOE_SLAB, :]
            a = jnp.dot(xb, w1_s[...], preferred_element_type=F32)
            u = jnp.dot(xb, w3_s[...], preferred_element_type=F32)
            hb = (a / (1.0 + jnp.exp(-a))) * u
            o_ref[r:r + MOE_SLAB, :] = jnp.dot(hb.astype(BF16), w2_s[...],
                                               preferred_element_type=F32).astype(o_ref.dtype)

    @pl.when(i >= nv_ref[0])
    def _pad():
        o_ref[...] = jnp.zeros(o_ref.shape, o_ref.dtype)


def _expert_ffn(tile_expert, n_valid, xs, w1, w3, w2, layer):
    p, d = xs.shape
    f = w1.shape[3]
    n_tiles = p // MOE_TILE
    grid_spec = pltpu.PrefetchScalarGridSpec(
        num_scalar_prefetch=2,
        grid=(n_tiles,),
        in_specs=[pl.BlockSpec((MOE_TILE, d), lambda i, te, nv: (i, 0)),
                  pl.BlockSpec((1, 1, d, f), lambda i, te, nv: (layer, te[i], 0, 0)),
                  pl.BlockSpec((1, 1, d, f), lambda i, te, nv: (layer, te[i], 0, 0)),
                  pl.BlockSpec((1, 1, f, d), lambda i, te, nv: (layer, te[i], 0, 0))],
        out_specs=pl.BlockSpec((MOE_TILE, d), lambda i, te, nv: (i, 0)),
        scratch_shapes=[pltpu.VMEM((d, f), BF16), pltpu.VMEM((d, f), BF16), pltpu.VMEM((f, d), BF16)],
    )
    return pl.pallas_call(
        _expert_kernel,
        grid_spec=grid_spec,
        out_shape=jax.ShapeDtypeStruct((p, d), BF16),
        compiler_params=_params("arbitrary"),
        name="expert_ffn",
    )(tile_expert, n_valid, xs, w1, w3, w2)


def _combine_ln_kernel(x_ref, y0_ref, y1_ref, fields_ref, g_ref, b_ref, o_ref):
    fields = fields_ref[...]
    ffn = fields[:, 2:3] * y0_ref[...].astype(F32) + fields[:, 3:4] * y1_ref[...].astype(F32)
    o_ref[...] = _layer_norm_rows(DEEPNORM_ALPHA * x_ref[...] + ffn, g_ref[...], b_ref[...])


def _combine_ln(x2d, y0, y1, fields, ln_g, ln_b, tm=512):
    t, d = x2d.shape
    row = pl.BlockSpec((tm, d), lambda i: (i, 0))
    vec = pl.BlockSpec((1, d), lambda i: (0, 0))
    return pl.pallas_call(
        _combine_ln_kernel,
        grid=(t // tm,),
        in_specs=[row, row, row, pl.BlockSpec((tm, LANES), lambda i: (i, 0)), vec, vec],
        out_specs=row,
        out_shape=jax.ShapeDtypeStruct((t, d), F32),
        compiler_params=_params("parallel"),
        name="combine_ln",
    )(x2d, y0, y1, fields, ln_g.reshape(1, d), ln_b.reshape(1, d))


def _dispatch_plan(info, counts_f):
    t = info.shape[1]
    a = t * TOPK_IN_GROUP
    n_fill = N_EXPERTS * MOE_TILE
    n_tiles = a // MOE_TILE + N_EXPERTS
    experts = jnp.arange(N_EXPERTS, dtype=jnp.int32)
    counts = counts_f[0, :N_EXPERTS].astype(jnp.int32)
    padded = (counts + MOE_TILE - 1) // MOE_TILE * MOE_TILE
    pends = jnp.cumsum(padded)
    fill_ends = jnp.cumsum(padded - counts)
    fill = jnp.arange(n_fill, dtype=jnp.int32)
    fill_key = jnp.sum((fill_ends[None, :] <= fill[:, None]).astype(jnp.int32), axis=1)
    tok = jnp.arange(t, dtype=jnp.int32)
    keys = jnp.concatenate([info[0].astype(jnp.int32), info[1].astype(jnp.int32), fill_key])
    toks = jnp.concatenate([tok, tok, fill % t])
    flat = jnp.arange(a + n_fill, dtype=jnp.int32)
    _, slot_tok, slot_flat = lax.sort((keys, toks, flat), num_keys=1, is_stable=True)
    _, slot_of = lax.sort((slot_flat, flat), num_keys=1)
    tile_start = jnp.arange(n_tiles, dtype=jnp.int32) * MOE_TILE
    tile_expert = jnp.sum((pends[None, :] <= tile_start[:, None]).astype(jnp.int32), axis=1)
    last_used = jnp.max(jnp.where(counts > 0, experts, 0))
    tile_expert = jnp.minimum(tile_expert, last_used)
    n_valid = (pends[-1] // MOE_TILE).astype(jnp.int32).reshape(1)
    return tile_expert, n_valid, slot_tok, slot_of[:t], slot_of[t:a]


def _moe_and_norm(x1, x1_bf16, fields, info, counts_f, w1, w3, w2, layer, ln_g, ln_b):
    tile_expert, n_valid, slot_tok, dest0, dest1 = _dispatch_plan(info, counts_f)
    yb = _expert_ffn(tile_expert, n_valid, x1_bf16[slot_tok], w1, w3, w2, layer)
    return _combine_ln(x1, yb[dest0], yb[dest1], fields, ln_g, ln_b)


def kernel(x, ab_w_in, ab_w_out, c_w_in, c_w_out, c_norm_g, hgrn_lb_logits, ln_g, ln_b,
           router_g_w, router_g_b, router_e_w, router_e_b, exp_w1, exp_w3, exp_w2):
    batch, seq, d = x.shape
    t = batch * seq
    tables = _rope_lane_tables(seq)
    bias_tab = jnp.asarray(_dilated_bias_table())
    lb_all = jnp.cumsum(jax.nn.softmax(hgrn_lb_logits.astype(F32), axis=0), axis=0)
    lb_all = lb_all - lb_all[0:1]
    n_pairs = (AB_WIDTH // 2) // LANES

    xc = x.reshape(t, d)
    for l in range(DEPTH):
        j = l // 2
        if l % 2 == 0:
            h = _proj_ab(xc, ab_w_in[j].astype(BF16), tables, seq).reshape(batch, seq, 3 * AB_WIDTH)
            o_parts = [_moba(h, n_pairs).reshape(t, AB_WIDTH // 2),
                       _dilated(h, bias_tab, n_pairs).reshape(t, AB_WIDTH // 2)]
            w_out = ab_w_out[j]
        else:
            q, lf, kk, v, g = _proj_c(xc, c_w_in[j].astype(BF16), lb_all[j])
            o_parts = [_hgrn(q, lf, kk, v, g, c_norm_g[j], batch, seq).reshape(t, d)]
            w_out = c_w_out[j]
        wr = jnp.zeros((d, LANES), F32)
        wr = wr.at[:, :N_GROUPS].set(router_g_w[l]).at[:, N_GROUPS:N_GROUPS + N_EXPERTS].set(router_e_w[l])
        rb = jnp.zeros((1, LANES), F32)
        rb = rb.at[0, :N_GROUPS].set(router_g_b[l]).at[0, N_GROUPS:N_GROUPS + N_EXPERTS].set(router_e_b[l])
        x1, x1_bf16, fields, info, counts_f = _outproj_ln(o_parts, xc, w_out.astype(BF16), ln_g[l, 0], ln_b[l, 0],
                                                          wr, rb)
        xc = _moe_and_norm(x1, x1_bf16, fields, info, counts_f, exp_w1, exp_w3, exp_w2, l,
                           ln_g[l, 1], ln_b[l, 1])
    return xc.reshape(batch, seq, d)
```

```python
import functools
import math

import numpy as np
import jax
import jax.numpy as jnp
from jax import lax
from jax.experimental import pallas as pl
from jax.experimental.pallas import tpu as pltpu

F32 = jnp.float32
BF16 = jnp.bfloat16

D_MODEL = 1024
DEPTH = 4
HEAD_DIM = 64
N_HEADS_A = 8
N_HEADS_B = 8
AB_WIDTH = (N_HEADS_A + N_HEADS_B) * HEAD_DIM
ROT_DIM = HEAD_DIM // 4
ROPE_THETA = 500000.0
MOBA_BLOCK = 256
MOBA_TOPK = 3
DILATED_BRANCHES = ((128, 1), (512, 4), (2048, 16))
HGRN_EXPAND = 128
N_HEADS_C = D_MODEL // HGRN_EXPAND
HGRN_CHUNK = 64
HGRN_SUB = 16
HGRN_UNROLL = 8
HGRN_FAST_LIMIT = 60.0
N_GROUPS = 4
EXPERTS_PER_GROUP = 8
N_EXPERTS = N_GROUPS * EXPERTS_PER_GROUP
TOPK_IN_GROUP = 2
D_EXPERT = D_MODEL // 2
DEEPNORM_ALPHA = (2.0 * DEPTH) ** 0.25
LN_EPS = 1e-5
RMS_EPS = 1e-6

LANES = 128
ATT_BLOCK = 256
MOBA_STEP = 4
MOBA_SEL_ROWS = 16
MOBA_QROWS = 512
SHIFT_LANE = LANES - 2
ATT_FAST_LIMIT = 38.0
DIL_STEP = 3
NEG_BIG = -1e30
MOE_TILE = 512
MOE_SLAB = 512
VMEM_LIMIT = 56 * 1024 * 1024

_NT = (((1,), (1,)), ((), ()))
_TN = (((0,), (0,)), ((), ()))


def _params(*sem):
    return pltpu.CompilerParams(dimension_semantics=sem, vmem_limit_bytes=VMEM_LIMIT)


def _proj_ab_kernel(x_ref, w_ref, cos_ref, sa_ref, sb_ref, o_ref):
    xb = x_ref[...].astype(BF16)
    sec = AB_WIDTH // 2
    for c in range(6):
        acc = jnp.dot(xb, w_ref[:, c * sec:(c + 1) * sec], preferred_element_type=F32)
        if c % 3 == 2:
            o_ref[:, c * sec:(c + 1) * sec] = acc.astype(BF16)
            continue
        cosv = cos_ref[...]
        sa = sa_ref[...]
        sb = sb_ref[...]
        for k in range(sec // LANES):
            a = acc[:, k * LANES:(k + 1) * LANES]
            r = a * cosv + pltpu.roll(a, LANES - ROT_DIM // 2, 1) * sa + pltpu.roll(a, ROT_DIM // 2, 1) * sb
            if c % 3 == 0:
                r = r * (HEAD_DIM ** -0.5)
            o_ref[:, c * sec + k * LANES:c * sec + (k + 1) * LANES] = r.astype(BF16)


def _rope_lane_tables(seq):
    half = ROT_DIM // 2
    inv = ROPE_THETA ** (-jnp.arange(half, dtype=F32) / half)
    ang = jnp.arange(seq, dtype=F32)[:, None] * inv[None, :]
    cos, sin = jnp.cos(ang), jnp.sin(ang)
    ones = jnp.ones((seq, HEAD_DIM - ROT_DIM), F32)
    zeros = jnp.zeros((seq, HEAD_DIM - ROT_DIM), F32)
    zh = jnp.zeros((seq, half), F32)
    cos_t = jnp.concatenate([cos, cos, ones], axis=1)
    sa_t = jnp.concatenate([-sin, zh, zeros], axis=1)
    sb_t = jnp.concatenate([zh, sin, zeros], axis=1)
    rep = LANES // HEAD_DIM
    return tuple(jnp.tile(t, (1, rep)) for t in (cos_t, sa_t, sb_t))


def _proj_ab(x2d, w_bf16, tables, seq, tm=1024):
    t, d = x2d.shape
    n = w_bf16.shape[1]
    nseq = seq // tm
    tab_spec = pl.BlockSpec((tm, LANES), lambda i: (i % nseq, 0))
    return pl.pallas_call(
        _proj_ab_kernel,
        grid=(t // tm,),
        in_specs=[pl.BlockSpec((tm, d), lambda i: (i, 0)),
                  pl.BlockSpec((d, n), lambda i: (0, 0)),
                  tab_spec, tab_spec, tab_spec],
        out_specs=pl.BlockSpec((tm, n), lambda i: (i, 0)),
        out_shape=jax.ShapeDtypeStruct((t, n), BF16),
        compiler_params=_params("parallel"),
        name="proj_ab",
    )(x2d, w_bf16, *tables)


def _softmax_step(parts, vaug, m_scr, l_scr, acc_scr):
    m_prev = m_scr[...]
    mx = functools.reduce(jnp.maximum, parts)
    m_new = jnp.maximum(m_prev, jnp.max(mx, axis=1, keepdims=True))
    alpha = jnp.exp(m_prev - m_new)
    p = jnp.concatenate([jnp.exp(part - m_new).astype(BF16) for part in parts], axis=1)
    pv = jnp.dot(p, vaug, preferred_element_type=F32)
    acc_scr[...] = alpha * acc_scr[...] + pv[:, :LANES]
    l_scr[...] = alpha * l_scr[...] + pv[:, LANES:]
    m_scr[...] = m_new


def _plain_step(parts, vaug, l_scr, acc_scr):
    p = jnp.concatenate([jnp.exp(part).astype(BF16) for part in parts], axis=1)
    pv = jnp.dot(p, vaug, preferred_element_type=F32)
    acc_scr[...] += pv[:, :LANES]
    l_scr[...] += pv[:, LANES:]


def _stacked_q(qaug_scr, q_start, bq):
    return jnp.concatenate([qaug_scr[0, pl.ds(q_start, bq), :], qaug_scr[1, pl.ds(q_start, bq), :]], axis=0)


def _lane_parts(s):
    return [s[:, c * LANES:(c + 1) * LANES] for c in range(s.shape[1] // LANES)]


def _head_mask(lane, h):
    return (lane >= h * HEAD_DIM) & (lane < (h + 1) * HEAD_DIM)


def _head_sq_norms(x, h):
    lane = lax.broadcasted_iota(jnp.int32, x.shape, 1)
    xf = x.astype(F32)
    sq = jnp.where(_head_mask(lane, h), xf * xf, 0.0).astype(BF16)
    return jnp.dot(sq, jnp.ones((LANES, LANES), BF16), preferred_element_type=F32)


def _score_shifts(q, k, flag_ref):
    bounds = [jnp.sqrt(jnp.max(_head_sq_norms(q, h), axis=0, keepdims=True)
                       * jnp.max(_head_sq_norms(k, h), axis=0, keepdims=True)) for h in range(2)]
    fast_ok = jnp.max(jnp.maximum(bounds[0], bounds[1])) <= ATT_FAST_LIMIT
    flag_ref[0] = jnp.where(fast_ok, 1, 0).astype(jnp.int32)
    return [jnp.where(fast_ok, -b, 0.0) for b in bounds]


def _softmax_init(m_scr, l_scr, acc_scr):
    m_scr[...] = jnp.full(m_scr.shape, -jnp.inf, F32)
    l_scr[...] = jnp.zeros(l_scr.shape, F32)
    acc_scr[...] = jnp.zeros(acc_scr.shape, F32)


def _softmax_finish(o_ref, l_scr, acc_scr):
    bq = acc_scr.shape[0] // 2
    lane = lax.broadcasted_iota(jnp.int32, (bq, LANES), 1)
    o0 = acc_scr[0:bq, :] / l_scr[0:bq, :]
    o1 = acc_scr[bq:, :] / l_scr[bq:, :]
    o_ref[0] = jnp.where(lane < HEAD_DIM, o0, o1).astype(o_ref.dtype)


def _moba_kernel(q_ref, k_ref, v_ref, o_ref, kaug_scr, vaug_scr, qaug_scr, flag_ref, m_scr, l_scr, acc_scr, *, seq):
    bq = ATT_BLOCK
    pad = (MOBA_STEP - 1) * bq
    i = pl.program_id(2)

    @pl.when(i == 0)
    def _per_sequence_setup():
        q = q_ref[0]
        k = k_ref[0]
        rows = lax.broadcasted_iota(jnp.int32, (MOBA_SEL_ROWS, seq), 0)
        cols = lax.broadcasted_iota(jnp.int32, (MOBA_SEL_ROWS, seq), 1)
        own = jnp.right_shift(cols, int(math.log2(MOBA_BLOCK)))
        ind = jnp.where(own == rows, 1.0, 0.0).astype(BF16)
        km = jnp.dot(ind, k, preferred_element_type=F32) * (1.0 / MOBA_BLOCK)
        km_hi = km.astype(BF16)
        km_lo = (km - km_hi.astype(F32)).astype(BF16)
        shifts = _score_shifts(q, k, flag_ref)
        blk_lane = lax.broadcasted_iota(jnp.int32, (bq, LANES), 1)
        kaug_scr[0:pad, 0:LANES] = jnp.zeros((pad, LANES), BF16)
        vaug_scr[0:pad, 0:LANES] = jnp.zeros((pad, LANES), BF16)
        for jb in range(-(MOBA_STEP - 1), seq // bq):
            tgt = jb if jb >= 0 else LANES - 1
            kaug_scr[pad + jb * bq:pad + (jb + 1) * bq, LANES:] = jnp.where(
                (blk_lane == tgt) | (blk_lane == SHIFT_LANE), 1.0, 0.0).astype(BF16)
        kaug_scr[pad:, 0:LANES] = k
        vaug_scr[pad:, 0:LANES] = v_ref[0]
        vaug_scr[:, LANES:] = jnp.ones((seq + pad, LANES), BF16)
        lane_all = lax.broadcasted_iota(jnp.int32, (seq, LANES), 1)
        rows_f = rows.astype(F32)
        for h in range(2):
            qh = jnp.where(_head_mask(lane_all, h), q, jnp.zeros_like(q))
            g = (lax.dot_general(km_hi, qh, _NT, preferred_element_type=F32)
                 + lax.dot_general(km_lo, qh, _NT, preferred_element_type=F32))
            g = jnp.where(rows < own, g, -jnp.inf)
            bias = jnp.where(rows == own, 0.0, NEG_BIG)
            for _ in range(MOBA_TOPK):
                mx = jnp.max(g, axis=0, keepdims=True)
                first = jnp.min(jnp.where(g == mx, rows_f, float(MOBA_SEL_ROWS)), axis=0, keepdims=True)
                pick = (rows_f == first) & (mx > -jnp.inf)
                bias = jnp.where(pick, 0.0, bias)
                g = jnp.where(pick, -jnp.inf, g)
            bias = jnp.concatenate([bias, jnp.full((LANES - MOBA_SEL_ROWS, seq), NEG_BIG, F32)], axis=0).T
            qaug_scr[h, :, 0:LANES] = qh
            qaug_scr[h, :, LANES:] = jnp.where(lane_all == SHIFT_LANE, shifts[h], bias).astype(BF16)

    qrows = MOBA_QROWS
    qpb = qrows // bq
    last = (i + 1) * qpb - 1
    lane = lax.broadcasted_iota(jnp.int32, (2 * qrows, LANES), 1)
    row = lax.broadcasted_iota(jnp.int32, (2 * qrows, LANES), 0)
    qi = row & (bq - 1)
    qblk = jnp.right_shift(row, int(math.log2(bq))) & (qpb - 1)
    _softmax_init(m_scr, l_scr, acc_scr)
    parts_per_blk = bq // LANES
    q_start = pl.multiple_of(i * qrows, qrows)

    def step(t, first, fast):
        start = pl.multiple_of((last - MOBA_STEP * t) * bq, bq)
        kaug = kaug_scr[pl.ds(start, MOBA_STEP * bq), :]
        vaug = vaug_scr[pl.ds(start, MOBA_STEP * bq), :]
        parts = _lane_parts(lax.dot_general(_stacked_q(qaug_scr, q_start, qrows), kaug, _NT,
                                            preferred_element_type=F32))
        if first:
            for own in range(qpb):
                for c in range(parts_per_blk):
                    cc = (MOBA_STEP - qpb + own) * parts_per_blk + c
                    parts[cc] = jnp.where((qblk == own) & (lane + c * LANES > qi), NEG_BIG, parts[cc])
        if fast:
            _plain_step(parts, vaug, l_scr, acc_scr)
        else:
            _softmax_step(parts, vaug, m_scr, l_scr, acc_scr)

    def run(fast):
        step(0, True, fast)

        def later(t, carry):
            step(t, False, fast)
            return carry

        lax.fori_loop(1, (last + MOBA_STEP) // MOBA_STEP, later, 0)

    pl.when(flag_ref[0] == 1)(functools.partial(run, True))
    pl.when(flag_ref[0] != 1)(functools.partial(run, False))
    _softmax_finish(o_ref, l_scr, acc_scr)


def _moba(h3, n_pairs):
    b, seq, _ = h3.shape
    sec_blocks = (AB_WIDTH // 2) // LANES
    assert seq % MOBA_QROWS == 0 and seq // ATT_BLOCK <= MOBA_SEL_ROWS and MOBA_QROWS // ATT_BLOCK <= MOBA_STEP
    return pl.pallas_call(
        functools.partial(_moba_kernel, seq=seq),
        grid=(b, n_pairs, seq // MOBA_QROWS),
        in_specs=[pl.BlockSpec((1, seq, LANES), lambda bb, p, i: (bb, 0, p)),
                  pl.BlockSpec((1, seq, LANES), lambda bb, p, i: (bb, 0, sec_blocks + p)),
                  pl.BlockSpec((1, seq, LANES), lambda bb, p, i: (bb, 0, 2 * sec_blocks + p))],
        out_specs=pl.BlockSpec((1, MOBA_QROWS, LANES), lambda bb, p, i: (bb, i, p)),
        out_shape=jax.ShapeDtypeStruct((b, seq, n_pairs * LANES), BF16),
        scratch_shapes=[pltpu.VMEM((seq + (MOBA_STEP - 1) * ATT_BLOCK, 2 * LANES), BF16),
                        pltpu.VMEM((seq + (MOBA_STEP - 1) * ATT_BLOCK, 2 * LANES), BF16),
                        pltpu.VMEM((2, seq, 2 * LANES), BF16),
                        pltpu.SMEM((1,), jnp.int32),
                        pltpu.VMEM((2 * MOBA_QROWS, LANES), F32),
                        pltpu.VMEM((2 * MOBA_QROWS, LANES), F32),
                        pltpu.VMEM((2 * MOBA_QROWS, LANES), F32)],
        compiler_params=_params("parallel", "parallel", "arbitrary"),
        name="moba",
    )(h3, h3, h3)


def _dilated_n_blocks():
    return max(w for w, _ in DILATED_BRANCHES) // ATT_BLOCK + 1


def _dilated_bias_table():
    n_steps = -(-_dilated_n_blocks() // DIL_STEP)
    qi = np.arange(ATT_BLOCK)[:, None]
    ki = np.arange(ATT_BLOCK)[None, :]
    tabs = []
    for t in range(n_steps):
        groups = []
        for g in range(DIL_STEP):
            d = qi - ki + (DIL_STEP * t + DIL_STEP - 1 - g) * ATT_BLOCK
            cnt = np.zeros(d.shape, np.int64)
            for window, dil in DILATED_BRANCHES:
                cnt += ((d >= 0) & (d <= window) & (d % dil == 0)).astype(np.int64)
            groups.append(np.where(cnt > 0, np.log(np.maximum(cnt, 1).astype(np.float64)), NEG_BIG))
        tabs.append(np.concatenate(groups, axis=1))
    return np.stack(tabs).astype(np.float32)


def _dilated_kernel(q_ref, k_ref, v_ref, bias_ref, o_ref, kaug_scr, vaug_scr, qaug_scr, flag_ref, m_scr, l_scr, acc_scr,
                    *, seq):
    bq = ATT_BLOCK
    pad = (DIL_STEP - 1) * bq
    i = pl.program_id(2)

    @pl.when(i == 0)
    def _per_sequence_setup():
        q = q_ref[0]
        k = k_ref[0]
        shifts = _score_shifts(q, k, flag_ref)
        flag_lane = lax.broadcasted_iota(jnp.int32, (pad, LANES), 1)
        lane_all = lax.broadcasted_iota(jnp.int32, (seq, LANES), 1)
        kaug_scr[0:pad, 0:LANES] = jnp.zeros((pad, LANES), BF16)
        kaug_scr[0:pad, LANES:] = jnp.where((flag_lane == 0) | (flag_lane == SHIFT_LANE), 1.0, 0.0).astype(BF16)
        kaug_scr[pad:, 0:LANES] = k
        kaug_scr[pad:, LANES:] = jnp.where(lane_all == SHIFT_LANE, 1.0, 0.0).astype(BF16)
        vaug_scr[0:pad, 0:LANES] = jnp.zeros((pad, LANES), BF16)
        vaug_scr[pad:, 0:LANES] = v_ref[0]
        vaug_scr[:, LANES:] = jnp.ones((seq + pad, LANES), BF16)
        for h in range(2):
            aug = jnp.where(lane_all == SHIFT_LANE, shifts[h], jnp.where(lane_all == 0, NEG_BIG, 0.0))
            qaug_scr[h, :, 0:LANES] = jnp.where(_head_mask(lane_all, h), q, jnp.zeros_like(q))
            qaug_scr[h, :, LANES:] = aug.astype(BF16)

    _softmax_init(m_scr, l_scr, acc_scr)
    q_start = pl.multiple_of(i * bq, bq)

    def step(t, fast):
        start = pl.multiple_of((i - DIL_STEP * t) * bq, bq)
        kaug = kaug_scr[pl.ds(start, DIL_STEP * bq), :]
        vaug = vaug_scr[pl.ds(start, DIL_STEP * bq), :]
        bias = bias_ref[t]
        s = lax.dot_general(_stacked_q(qaug_scr, q_start, bq), kaug, _NT, preferred_element_type=F32)
        parts = _lane_parts(jnp.concatenate([s[0:bq, :] + bias, s[bq:, :] + bias], axis=0))
        if fast:
            _plain_step(parts, vaug, l_scr, acc_scr)
        else:
            _softmax_step(parts, vaug, m_scr, l_scr, acc_scr)

    n_steps = (jnp.minimum(i, _dilated_n_blocks() - 1) + DIL_STEP) // DIL_STEP

    def run(fast):
        def body(t, carry):
            step(t, fast)
            return carry

        lax.fori_loop(0, n_steps, body, 0)

    pl.when(flag_ref[0] == 1)(functools.partial(run, True))
    pl.when(flag_ref[0] != 1)(functools.partial(run, False))
    _softmax_finish(o_ref, l_scr, acc_scr)


def _dilated(h3, bias_tab, n_pairs):
    b, seq, _ = h3.shape
    sec_blocks = (AB_WIDTH // 2) // LANES
    base = 3 * sec_blocks
    nq = seq // ATT_BLOCK
    pad = (DIL_STEP - 1) * ATT_BLOCK
    return pl.pallas_call(
        functools.partial(_dilated_kernel, seq=seq),
        grid=(b, n_pairs, nq),
        in_specs=[pl.BlockSpec((1, seq, LANES), lambda bb, p, i: (bb, 0, base + p)),
                  pl.BlockSpec((1, seq, LANES), lambda bb, p, i: (bb, 0, base + sec_blocks + p)),
                  pl.BlockSpec((1, seq, LANES), lambda bb, p, i: (bb, 0, base + 2 * sec_blocks + p)),
                  pl.BlockSpec(bias_tab.shape, lambda bb, p, i: (0, 0, 0))],
        out_specs=pl.BlockSpec((1, ATT_BLOCK, LANES), lambda bb, p, i: (bb, i, p)),
        out_shape=jax.ShapeDtypeStruct((b, seq, n_pairs * LANES), BF16),
        scratch_shapes=[pltpu.VMEM((seq + pad, 2 * LANES), BF16),
                        pltpu.VMEM((seq + pad, 2 * LANES), BF16),
                        pltpu.VMEM((2, seq, 2 * LANES), BF16),
                        pltpu.SMEM((1,), jnp.int32),
                        pltpu.VMEM((2 * ATT_BLOCK, LANES), F32),
                        pltpu.VMEM((2 * ATT_BLOCK, LANES), F32),
                        pltpu.VMEM((2 * ATT_BLOCK, LANES), F32)],
        compiler_params=_params("parallel", "parallel", "arbitrary"),
        name="dilated",
    )(h3, h3, h3, bias_tab)


def _proj_c_kernel(x_ref, w_ref, loglb_ref, log1mlb_ref, omlb_ref, q_ref, lf_ref, kk_ref, v_ref, g_ref):
    xb = x_ref[...].astype(BF16)
    d = D_MODEL

    def sec(c):
        return jnp.dot(xb, w_ref[:, c * d:(c + 1) * d], preferred_element_type=F32)

    q_ref[...] = sec(0)
    z = sec(1)
    log_sig = jnp.minimum(z, 0.0) - jnp.log1p(jnp.exp(-jnp.abs(z)))
    a = loglb_ref[...]
    c = log1mlb_ref[...] + log_sig
    lf_ref[...] = jnp.maximum(a, c) + jnp.log1p(jnp.exp(-jnp.abs(a - c)))
    kk_ref[...] = omlb_ref[...] / (1.0 + jnp.exp(z))
    v_ref[...] = sec(2).astype(v_ref.dtype)
    g_ref[...] = sec(3)


def _proj_c(x2d, w_bf16, lb, tm=512):
    t, d = x2d.shape
    n = w_bf16.shape[1]
    lb = lb.astype(F32).reshape(1, d)
    vec_spec = pl.BlockSpec((1, d), lambda i: (0, 0))
    out_spec = pl.BlockSpec((tm, d), lambda i: (i, 0))
    sds = jax.ShapeDtypeStruct((t, d), F32)
    return pl.pallas_call(
        _proj_c_kernel,
        grid=(t // tm,),
        in_specs=[pl.BlockSpec((tm, d), lambda i: (i, 0)),
                  pl.BlockSpec((d, n), lambda i: (0, 0)),
                  vec_spec, vec_spec, vec_spec],
        out_specs=[out_spec] * 5,
        out_shape=[sds, sds, sds, jax.ShapeDtypeStruct((t, d), BF16), sds],
        compiler_params=_params("parallel"),
        name="proj_c",
    )(x2d, w_bf16, jnp.log(lb), jnp.log1p(-lb), 1.0 - lb)


def _split3_bf16(x):
    h1 = x.astype(BF16)
    r1 = x - h1.astype(F32)
    h2 = r1.astype(BF16)
    h3 = (r1 - h2.astype(F32)).astype(BF16)
    return h1, h2, h3


def _hgrn_kernel(q_ref, lf_ref, kk_ref, v_ref, g_ref, ng_ref, o_ref, *, seq):
    C, SUB = HGRN_CHUNK, HGRN_SUB
    n_sub = C // SUB
    ri = lax.broadcasted_iota(jnp.int32, (C, C), 0)
    ci = lax.broadcasted_iota(jnp.int32, (C, C), 1)
    tri = jnp.where(ci <= ri, 1.0, 0.0).astype(BF16)
    row_c = lax.broadcasted_iota(jnp.int32, (C, LANES), 0)
    row_s = lax.broadcasted_iota(jnp.int32, (SUB, C), 0)
    lane_s = lax.broadcasted_iota(jnp.int32, (SUB, C), 1)
    ng = ng_ref[...]

    def prefix_shifts(x):
        for sh in (1, 2, 4):
            x = x + jnp.where(row_c >= sh, pltpu.roll(x, sh, 0), 0.0)
        sh = 8
        while sh < C:
            x = x + jnp.concatenate([jnp.zeros((sh, LANES), F32), x[:C - sh, :]], axis=0)
            sh *= 2
        return x

    def prefix_matmul(x):
        l1, l2, l3 = _split3_bf16(x)
        return (jnp.dot(tri, l1, preferred_element_type=F32) + jnp.dot(tri, l2, preferred_element_type=F32)
                + jnp.dot(tri, l3, preferred_element_type=F32))

    def scores_factored(qc, kc, b):
        refs = [jnp.zeros((1, LANES), F32)] + [b[s * SUB - 1:s * SUB, :] for s in range(1, n_sub)]
        ref_rows = jnp.concatenate([jnp.broadcast_to(r, (SUB, LANES)) for r in refs], axis=0)
        qt = (qc * jnp.exp(b - ref_rows)).astype(BF16)
        rows_a = []
        for s in range(n_sub):
            hi = (s + 1) * SUB
            kt = (kc[:hi, :] * jnp.exp(refs[s] - b[:hi, :])).astype(BF16)
            a = lax.dot_general(qt[s * SUB:hi, :], kt, _NT, preferred_element_type=F32)
            if hi < C:
                a = jnp.concatenate([a, jnp.zeros((SUB, C - hi), F32)], axis=1)
            rows_a.append(a)
        return jnp.where(ci <= ri, jnp.concatenate(rows_a, axis=0), 0.0)

    def scores_direct(qc, kc, b):
        rows_a = []
        for sidx in range(n_sub):
            lo = sidx * SUB
            q_i = qc[lo:lo + SUB, :]
            b_i = b[lo:lo + SUB, :]
            if sidx == 0:
                a_blk = jnp.zeros((SUB, C), F32)
            else:
                ref = b[lo - 1:lo, :]
                qt = (q_i * jnp.exp(b_i - ref)).astype(BF16)
                kt = jnp.where(row_c < lo, kc * jnp.exp(jnp.minimum(ref - b, 0.0)), 0.0).astype(BF16)
                a_blk = lax.dot_general(qt, kt, _NT, preferred_element_type=F32)
            for j in range(SUB):
                r_lo = 0 if j < 8 else 8
                bj = b[lo + j:lo + j + 1, :]
                kj = kc[lo + j:lo + j + 1, :]
                pj = q_i[r_lo:, :] * (jnp.exp(jnp.minimum(b_i[r_lo:, :] - bj, 0.0)) * kj)
                col = jnp.sum(pj, axis=1, keepdims=True)
                if r_lo:
                    col = jnp.concatenate([jnp.zeros((r_lo, 1), F32), col], axis=0)
                a_blk = jnp.where((lane_s == lo + j) & (row_s >= j), col, a_blk)
            rows_a.append(a_blk)
        return jnp.concatenate(rows_a, axis=0)

    def make_chunk(scores, prefix):
        def chunk(c, st):
            r0 = pl.multiple_of(c * C, C)
            qc = q_ref[0, pl.ds(r0, C), :]
            kc = kk_ref[0, pl.ds(r0, C), :]
            vb = v_ref[0, pl.ds(r0, C), :].astype(BF16)
            b = prefix(lf_ref[0, pl.ds(r0, C), :])
            b_last = b[C - 1:C, :]
            inter = lax.dot_general((qc * jnp.exp(b)).astype(BF16), st.astype(BF16), _NT,
                                    preferred_element_type=F32)
            o = inter + jnp.dot(scores(qc, kc, b).astype(BF16), vb, preferred_element_type=F32)
            o = o * lax.rsqrt(jnp.mean(o * o, axis=1, keepdims=True) + RMS_EPS) * ng
            gc = g_ref[0, pl.ds(r0, C), :]
            o_ref[0, pl.ds(r0, C), :] = (o * (gc / (1.0 + jnp.exp(-gc)))).astype(o_ref.dtype)
            kd = (kc * jnp.exp(b_last - b)).astype(BF16)
            return st * jnp.exp(b_last) + lax.dot_general(vb, kd, _TN, preferred_element_type=F32)
        return chunk

    sub_decay = jnp.sum(lf_ref[0].reshape(seq // SUB, SUB, LANES), axis=1)
    fast_ok = jnp.min(sub_decay) >= -HGRN_FAST_LIMIT
    st0 = jnp.zeros((LANES, LANES), F32)

    @pl.when(fast_ok)
    def _():
        lax.fori_loop(0, seq // C, make_chunk(scores_factored, prefix_shifts), st0, unroll=HGRN_UNROLL)

    @pl.when(jnp.logical_not(fast_ok))
    def _():
        lax.fori_loop(0, seq // C, make_chunk(scores_direct, prefix_matmul), st0)


def _hgrn(q, lf, kk, v, g, norm_g, batch, seq):
    shp = (batch, seq, D_MODEL)
    args = [a.reshape(shp) for a in (q, lf, kk, v, g)]
    spec = pl.BlockSpec((1, seq, LANES), lambda bb, h: (bb, 0, h))
    return pl.pallas_call(
        functools.partial(_hgrn_kernel, seq=seq),
        grid=(batch, N_HEADS_C),
        in_specs=[spec] * 5 + [pl.BlockSpec((1, LANES), lambda bb, h: (0, 0))],
        out_specs=spec,
        out_shape=jax.ShapeDtypeStruct(shp, BF16),
        compiler_params=_params("parallel", "parallel"),
        name="hgrn",
    )(*args, norm_g.astype(F32).reshape(1, LANES))


def _layer_norm_rows(z, g, b):
    mu = jnp.mean(z, axis=1, keepdims=True)
    zc = z - mu
    var = jnp.mean(zc * zc, axis=1, keepdims=True)
    return zc * lax.rsqrt(var + LN_EPS) * g + b


def _first_lane_of_max(vals, lane_f):
    mx = jnp.max(vals, axis=1, keepdims=True)
    return mx, jnp.min(jnp.where(vals == mx, lane_f, float(LANES)), axis=1, keepdims=True)


def _outproj_ln_kernel(*refs, n_parts):
    o_refs, w_refs = refs[:n_parts], refs[n_parts:2 * n_parts]
    (x_ref, g_ref, b_ref, wr_hi_ref, wr_lo_ref, rb_ref,
     y_ref, yb_ref, fields_ref, info_ref, cnt_ref) = refs[2 * n_parts:]
    mix = jnp.dot(o_refs[0][...], w_refs[0][...], preferred_element_type=F32)
    for o_ref, w_ref in zip(o_refs[1:], w_refs[1:]):
        mix += jnp.dot(o_ref[...], w_ref[...], preferred_element_type=F32)
    y = _layer_norm_rows(DEEPNORM_ALPHA * x_ref[...] + mix, g_ref[...], b_ref[...])
    y_ref[...] = y
    y_hi = y.astype(BF16)
    yb_ref[...] = y_hi
    y_lo = (y - y_hi.astype(F32)).astype(BF16)
    lg = (jnp.dot(y_hi, wr_hi_ref[...], preferred_element_type=F32)
          + jnp.dot(y_lo, wr_hi_ref[...], preferred_element_type=F32)
          + jnp.dot(y_hi, wr_lo_ref[...], preferred_element_type=F32)
          + rb_ref[...])
    lane = lax.broadcasted_iota(jnp.int32, lg.shape, 1)
    lane_f = lane.astype(F32)
    is_g = lane < N_GROUPS
    mg, grp = _first_lane_of_max(jnp.where(is_g, lg, -jnp.inf), lane_f)
    pg = 1.0 / jnp.sum(jnp.where(is_g, jnp.exp(lg - mg), 0.0), axis=1, keepdims=True)
    lo = float(N_GROUPS) + float(EXPERTS_PER_GROUP) * grp
    le = jnp.where((lane_f >= lo) & (lane_f < lo + float(EXPERTS_PER_GROUP)), lg, -jnp.inf)
    m1, i1 = _first_lane_of_max(le, lane_f)
    m2, i2 = _first_lane_of_max(jnp.where(lane_f == i1, -jnp.inf, le), lane_f)
    e21 = jnp.exp(m2 - m1)
    w0 = pg / (1.0 + e21)
    w1 = pg * e21 / (1.0 + e21)
    eid0 = i1 - float(N_GROUPS)
    eid1 = i2 - float(N_GROUPS)

    @pl.when(pl.program_id(0) == 0)
    def _():
        cnt_ref[...] = jnp.zeros(cnt_ref.shape, F32)

    hits = jnp.where(lane_f == eid0, 1.0, 0.0) + jnp.where(lane_f == eid1, 1.0, 0.0)
    cnt_ref[...] += jnp.sum(hits, axis=0, keepdims=True)
    fields = jnp.where(lane == 0, eid0, jnp.where(lane == 1, eid1, jnp.where(lane == 2, w0,
                       jnp.where(lane == 3, w1, 0.0))))
    fields_ref[...] = fields
    info_ref[...] = fields.T[:8, :]


def _outproj_ln(o_parts, x2d, w_bf16, ln_g, ln_b, wr, rb, tm=1024):
    t, d = x2d.shape
    wr_hi = wr.astype(BF16)
    wr_lo = (wr - wr_hi.astype(F32)).astype(BF16)
    vec = lambda n: pl.BlockSpec((1, n), lambda i: (0, 0))
    widths = [o.shape[1] for o in o_parts]
    offs = np.cumsum([0] + widths)
    w_parts = [w_bf16[offs[k]:offs[k + 1]] for k in range(len(widths))]
    return pl.pallas_call(
        functools.partial(_outproj_ln_kernel, n_parts=len(widths)),
        grid=(t // tm,),
        in_specs=[pl.BlockSpec((tm, wd), lambda i: (i, 0)) for wd in widths]
                 + [pl.BlockSpec((wd, d), lambda i: (0, 0)) for wd in widths]
                 + [pl.BlockSpec((tm, d), lambda i: (i, 0)),
                  vec(d), vec(d),
                  pl.BlockSpec((d, LANES), lambda i: (0, 0)),
                  pl.BlockSpec((d, LANES), lambda i: (0, 0)),
                  vec(LANES)],
        out_specs=[pl.BlockSpec((tm, d), lambda i: (i, 0)),
                   pl.BlockSpec((tm, d), lambda i: (i, 0)),
                   pl.BlockSpec((tm, LANES), lambda i: (i, 0)),
                   pl.BlockSpec((8, tm), lambda i: (0, i)),
                   vec(LANES)],
        out_shape=[jax.ShapeDtypeStruct((t, d), F32),
                   jax.ShapeDtypeStruct((t, d), BF16),
                   jax.ShapeDtypeStruct((t, LANES), F32),
                   jax.ShapeDtypeStruct((8, t), F32),
                   jax.ShapeDtypeStruct((1, LANES), F32)],
        compiler_params=_params("arbitrary"),
        name="outproj_ln",
    )(*o_parts, *w_parts, x2d, ln_g.reshape(1, d), ln_b.reshape(1, d), wr_hi, wr_lo, rb)


def _expert_kernel(te_ref, nv_ref, xs_ref, w1_ref, w3_ref, w2_ref, o_ref, w1_s, w3_s, w2_s):
    i = pl.program_id(0)
    prev = te_ref[jnp.maximum(i - 1, 0)]

    @pl.when((i == 0) | (te_ref[i] != prev))
    def _cast_weights():
        w1_s[...] = w1_ref[0, 0].astype(BF16)
        w3_s[...] = w3_ref[0, 0].astype(BF16)
        w2_s[...] = w2_ref[0, 0].astype(BF16)

    @pl.when(i < nv_ref[0])
    def _ffn():
        for r in range(0, MOE_TILE, MOE_SLAB):
            xb = xs_ref[r:r + MOE_SLAB, :]
            a = jnp.dot(xb, w1_s[...], preferred_element_type=F32)
            u = jnp.dot(xb, w3_s[...], preferred_element_type=F32)
            hb = (a / (1.0 + jnp.exp(-a))) * u
            o_ref[r:r + MOE_SLAB, :] = jnp.dot(hb.astype(BF16), w2_s[...],
                                               preferred_element_type=F32).astype(o_ref.dtype)

    @pl.when(i >= nv_ref[0])
    def _pad():
        o_ref[...] = jnp.zeros(o_ref.shape, o_ref.dtype)


def _expert_ffn(tile_expert, n_valid, xs, w1, w3, w2, layer):
    p, d = xs.shape
    f = w1.shape[3]
    n_tiles = p // MOE_TILE
    grid_spec = pltpu.PrefetchScalarGridSpec(
        num_scalar_prefetch=2,
        grid=(n_tiles,),
        in_specs=[pl.BlockSpec((MOE_TILE, d), lambda i, te, nv: (i, 0)),
                  pl.BlockSpec((1, 1, d, f), lambda i, te, nv: (layer, te[i], 0, 0)),
                  pl.BlockSpec((1, 1, d, f), lambda i, te, nv: (layer, te[i], 0, 0)),
                  pl.BlockSpec((1, 1, f, d), lambda i, te, nv: (layer, te[i], 0, 0))],
        out_specs=pl.BlockSpec((MOE_TILE, d), lambda i, te, nv: (i, 0)),
        scratch_shapes=[pltpu.VMEM((d, f), BF16), pltpu.VMEM((d, f), BF16), pltpu.VMEM((f, d), BF16)],
    )
    return pl.pallas_call(
        _expert_kernel,
        grid_spec=grid_spec,
        out_shape=jax.ShapeDtypeStruct((p, d), BF16),
        compiler_params=_params("arbitrary"),
        name="expert_ffn",
    )(tile_expert, n_valid, xs, w1, w3, w2)


def _combine_ln_kernel(x_ref, y0_ref, y1_ref, fields_ref, g_ref, b_ref, o_ref):
    fields = fields_ref[...]
    ffn = fields[:, 2:3] * y0_ref[...].astype(F32) + fields[:, 3:4] * y1_ref[...].astype(F32)
    o_ref[...] = _layer_norm_rows(DEEPNORM_ALPHA * x_ref[...] + ffn, g_ref[...], b_ref[...])


def _combine_ln(x2d, y0, y1, fields, ln_g, ln_b, tm=512):
    t, d = x2d.shape
    row = pl.BlockSpec((tm, d), lambda i: (i, 0))
    vec = pl.BlockSpec((1, d), lambda i: (0, 0))
    return pl.pallas_call(
        _combine_ln_kernel,
        grid=(t // tm,),
        in_specs=[row, row, row, pl.BlockSpec((tm, LANES), lambda i: (i, 0)), vec, vec],
        out_specs=row,
        out_shape=jax.ShapeDtypeStruct((t, d), F32),
        compiler_params=_params("parallel"),
        name="combine_ln",
    )(x2d, y0, y1, fields, ln_g.reshape(1, d), ln_b.reshape(1, d))


def _dispatch_plan(info, counts_f):
    t = info.shape[1]
    a = t * TOPK_IN_GROUP
    n_fill = N_EXPERTS * MOE_TILE
    n_tiles = a // MOE_TILE + N_EXPERTS
    experts = jnp.arange(N_EXPERTS, dtype=jnp.int32)
    counts = counts_f[0, :N_EXPERTS].astype(jnp.int32)
    padded = (counts + MOE_TILE - 1) // MOE_TILE * MOE_TILE
    pends = jnp.cumsum(padded)
    fill_ends = jnp.cumsum(padded - counts)
    fill = jnp.arange(n_fill, dtype=jnp.int32)
    fill_key = jnp.sum((fill_ends[None, :] <= fill[:, None]).astype(jnp.int32), axis=1)
    keys = jnp.concatenate([info[0].astype(jnp.int32), info[1].astype(jnp.int32), fill_key])
    flat = jnp.arange(a + n_fill, dtype=jnp.int32)
    flat_bits = int(a + n_fill - 1).bit_length()
    assert flat_bits + int(N_EXPERTS).bit_length() < 32
    slot_flat = jnp.sort(jnp.left_shift(keys, flat_bits) | flat) & ((1 << flat_bits) - 1)
    slot_tok = jnp.where(slot_flat < a, slot_flat % t, (slot_flat - a) % t)
    _, slot_of = lax.sort((slot_flat, flat), num_keys=1)
    tile_start = jnp.arange(n_tiles, dtype=jnp.int32) * MOE_TILE
    tile_expert = jnp.sum((pends[None, :] <= tile_start[:, None]).astype(jnp.int32), axis=1)
    last_used = jnp.max(jnp.where(counts > 0, experts, 0))
    tile_expert = jnp.minimum(tile_expert, last_used)
    n_valid = (pends[-1] // MOE_TILE).astype(jnp.int32).reshape(1)
    return tile_expert, n_valid, slot_tok, slot_of[:t], slot_of[t:a]


def _moe_and_norm(x1, x1_bf16, fields, info, counts_f, w1, w3, w2, layer, ln_g, ln_b):
    tile_expert, n_valid, slot_tok, dest0, dest1 = _dispatch_plan(info, counts_f)
    yb = _expert_ffn(tile_expert, n_valid, x1_bf16[slot_tok], w1, w3, w2, layer)
    return _combine_ln(x1, yb[dest0], yb[dest1], fields, ln_g, ln_b)


def kernel(x, ab_w_in, ab_w_out, c_w_in, c_w_out, c_norm_g, hgrn_lb_logits, ln_g, ln_b,
           router_g_w, router_g_b, router_e_w, router_e_b, exp_w1, exp_w3, exp_w2):
    batch, seq, d = x.shape
    t = batch * seq
    tables = _rope_lane_tables(seq)
    bias_tab = jnp.asarray(_dilated_bias_table())
    lb_all = jnp.cumsum(jax.nn.softmax(hgrn_lb_logits.astype(F32), axis=0), axis=0)
    lb_all = lb_all - lb_all[0:1]
    n_pairs = (AB_WIDTH // 2) // LANES

    xc = x.reshape(t, d)
    for l in range(DEPTH):
        j = l // 2
        if l % 2 == 0:
            h = _proj_ab(xc, ab_w_in[j].astype(BF16), tables, seq).reshape(batch, seq, 3 * AB_WIDTH)
            o_parts = [_moba(h, n_pairs).reshape(t, AB_WIDTH // 2),
                       _dilated(h, bias_tab, n_pairs).reshape(t, AB_WIDTH // 2)]
            w_out = ab_w_out[j]
        else:
            q, lf, kk, v, g = _proj_c(xc, c_w_in[j].astype(BF16), lb_all[j])
            o_parts = [_hgrn(q, lf, kk, v, g, c_norm_g[j], batch, seq).reshape(t, d)]
            w_out = c_w_out[j]
        wr = jnp.zeros((d, LANES), F32)
        wr = wr.at[:, :N_GROUPS].set(router_g_w[l]).at[:, N_GROUPS:N_GROUPS + N_EXPERTS].set(router_e_w[l])
        rb = jnp.zeros((1, LANES), F32)
        rb = rb.at[0, :N_GROUPS].set(router_g_b[l]).at[0, N_GROUPS:N_GROUPS + N_EXPERTS].set(router_e_b[l])
        x1, x1_bf16, fields, info, counts_f = _outproj_ln(o_parts, xc, w_out.astype(BF16), ln_g[l, 0], ln_b[l, 0],
                                                          wr, rb)
        xc = _moe_and_norm(x1, x1_bf16, fields, info, counts_f, exp_w1, exp_w3, exp_w2, l,
                           ln_g[l, 1], ln_b[l, 1])
    return xc.reshape(batch, seq, d)
```

```python
import functools
import math

import numpy as np
import jax
import jax.numpy as jnp
from jax import lax
from jax.experimental import pallas as pl
from jax.experimental.pallas import tpu as pltpu

F32 = jnp.float32
BF16 = jnp.bfloat16

D_MODEL = 1024
DEPTH = 4
HEAD_DIM = 64
N_HEADS_A = 8
N_HEADS_B = 8
AB_WIDTH = (N_HEADS_A + N_HEADS_B) * HEAD_DIM
ROT_DIM = HEAD_DIM // 4
ROPE_THETA = 500000.0
MOBA_BLOCK = 256
MOBA_TOPK = 3
DILATED_BRANCHES = ((128, 1), (512, 4), (2048, 16))
HGRN_EXPAND = 128
N_HEADS_C = D_MODEL // HGRN_EXPAND
HGRN_CHUNK = 64
HGRN_SUB = 32
HGRN_UNROLL = 8
HGRN_FAST_LIMIT = 80.0
N_GROUPS = 4
EXPERTS_PER_GROUP = 8
N_EXPERTS = N_GROUPS * EXPERTS_PER_GROUP
TOPK_IN_GROUP = 2
D_EXPERT = D_MODEL // 2
DEEPNORM_ALPHA = (2.0 * DEPTH) ** 0.25
LN_EPS = 1e-5
RMS_EPS = 1e-6

LANES = 128
ATT_BLOCK = 256
MOBA_STEP = 4
MOBA_SEL_ROWS = 16
MOBA_QROWS = 512
SHIFT_LANE = LANES - 2
ATT_FAST_LIMIT = 38.0
DIL_STEP = 3
NEG_BIG = -1e30
MOE_TILE = 512
MOE_SLAB = 512
VMEM_LIMIT = 56 * 1024 * 1024

_NT = (((1,), (1,)), ((), ()))
_TN = (((0,), (0,)), ((), ()))


def _params(*sem):
    return pltpu.CompilerParams(dimension_semantics=sem, vmem_limit_bytes=VMEM_LIMIT)


def _proj_ab_kernel(x_ref, w_ref, cos_ref, sa_ref, sb_ref, o_ref):
    xb = x_ref[...].astype(BF16)
    sec = AB_WIDTH // 2
    for c in range(6):
        acc = jnp.dot(xb, w_ref[:, c * sec:(c + 1) * sec], preferred_element_type=F32)
        if c % 3 == 2:
            o_ref[:, c * sec:(c + 1) * sec] = acc.astype(BF16)
            continue
        cosv = cos_ref[...]
        sa = sa_ref[...]
        sb = sb_ref[...]
        for k in range(sec // LANES):
            a = acc[:, k * LANES:(k + 1) * LANES]
            r = a * cosv + pltpu.roll(a, LANES - ROT_DIM // 2, 1) * sa + pltpu.roll(a, ROT_DIM // 2, 1) * sb
            if c % 3 == 0:
                r = r * (HEAD_DIM ** -0.5)
            o_ref[:, c * sec + k * LANES:c * sec + (k + 1) * LANES] = r.astype(BF16)


def _rope_lane_tables(seq):
    half = ROT_DIM // 2
    inv = ROPE_THETA ** (-jnp.arange(half, dtype=F32) / half)
    ang = jnp.arange(seq, dtype=F32)[:, None] * inv[None, :]
    cos, sin = jnp.cos(ang), jnp.sin(ang)
    ones = jnp.ones((seq, HEAD_DIM - ROT_DIM), F32)
    zeros = jnp.zeros((seq, HEAD_DIM - ROT_DIM), F32)
    zh = jnp.zeros((seq, half), F32)
    cos_t = jnp.concatenate([cos, cos, ones], axis=1)
    sa_t = jnp.concatenate([-sin, zh, zeros], axis=1)
    sb_t = jnp.concatenate([zh, sin, zeros], axis=1)
    rep = LANES // HEAD_DIM
    return tuple(jnp.tile(t, (1, rep)) for t in (cos_t, sa_t, sb_t))


def _proj_ab(x2d, w_bf16, tables, seq, tm=1024):
    t, d = x2d.shape
    n = w_bf16.shape[1]
    nseq = seq // tm
    tab_spec = pl.BlockSpec((tm, LANES), lambda i: (i % nseq, 0))
    return pl.pallas_call(
        _proj_ab_kernel,
        grid=(t // tm,),
        in_specs=[pl.BlockSpec((tm, d), lambda i: (i, 0)),
                  pl.BlockSpec((d, n), lambda i: (0, 0)),
                  tab_spec, tab_spec, tab_spec],
        out_specs=pl.BlockSpec((tm, n), lambda i: (i, 0)),
        out_shape=jax.ShapeDtypeStruct((t, n), BF16),
        compiler_params=_params("parallel"),
        name="proj_ab",
    )(x2d, w_bf16, *tables)


def _softmax_step(parts, vaug, m_scr, l_scr, acc_scr):
    m_prev = m_scr[...]
    mx = functools.reduce(jnp.maximum, parts)
    m_new = jnp.maximum(m_prev, jnp.max(mx, axis=1, keepdims=True))
    alpha = jnp.exp(m_prev - m_new)
    p = jnp.concatenate([jnp.exp(part - m_new).astype(BF16) for part in parts], axis=1)
    pv = jnp.dot(p, vaug, preferred_element_type=F32)
    acc_scr[...] = alpha * acc_scr[...] + pv[:, :LANES]
    l_scr[...] = alpha * l_scr[...] + pv[:, LANES:]
    m_scr[...] = m_new


def _plain_step(parts, vaug, l_scr, acc_scr):
    p = jnp.concatenate([jnp.exp(part).astype(BF16) for part in parts], axis=1)
    pv = jnp.dot(p, vaug, preferred_element_type=F32)
    acc_scr[...] += pv[:, :LANES]
    l_scr[...] += pv[:, LANES:]


def _stacked_q(qaug_scr, q_start, bq):
    return jnp.concatenate([qaug_scr[0, pl.ds(q_start, bq), :], qaug_scr[1, pl.ds(q_start, bq), :]], axis=0)


def _lane_parts(s):
    return [s[:, c * LANES:(c + 1) * LANES] for c in range(s.shape[1] // LANES)]


def _head_mask(lane, h):
    return (lane >= h * HEAD_DIM) & (lane < (h + 1) * HEAD_DIM)


def _head_sq_norms(x, h):
    lane = lax.broadcasted_iota(jnp.int32, x.shape, 1)
    xf = x.astype(F32)
    sq = jnp.where(_head_mask(lane, h), xf * xf, 0.0).astype(BF16)
    return jnp.dot(sq, jnp.ones((LANES, LANES), BF16), preferred_element_type=F32)


def _score_shifts(q, k, flag_ref):
    bounds = [jnp.sqrt(jnp.max(_head_sq_norms(q, h), axis=0, keepdims=True)
                       * jnp.max(_head_sq_norms(k, h), axis=0, keepdims=True)) for h in range(2)]
    fast_ok = jnp.max(jnp.maximum(bounds[0], bounds[1])) <= ATT_FAST_LIMIT
    flag_ref[0] = jnp.where(fast_ok, 1, 0).astype(jnp.int32)
    return [jnp.where(fast_ok, -b, 0.0) for b in bounds]


def _softmax_init(m_scr, l_scr, acc_scr):
    m_scr[...] = jnp.full(m_scr.shape, -jnp.inf, F32)
    l_scr[...] = jnp.zeros(l_scr.shape, F32)
    acc_scr[...] = jnp.zeros(acc_scr.shape, F32)


def _softmax_finish(o_ref, l_scr, acc_scr):
    bq = acc_scr.shape[0] // 2
    lane = lax.broadcasted_iota(jnp.int32, (bq, LANES), 1)
    o0 = acc_scr[0:bq, :] / l_scr[0:bq, :]
    o1 = acc_scr[bq:, :] / l_scr[bq:, :]
    o_ref[0] = jnp.where(lane < HEAD_DIM, o0, o1).astype(o_ref.dtype)


def _moba_kernel(q_ref, k_ref, v_ref, o_ref, kaug_scr, vaug_scr, qaug_scr, flag_ref, m_scr, l_scr, acc_scr, *, seq):
    bq = ATT_BLOCK
    pad = (MOBA_STEP - 1) * bq
    i = pl.program_id(2)

    @pl.when(i == 0)
    def _per_sequence_setup():
        q = q_ref[0]
        k = k_ref[0]
        rows = lax.broadcasted_iota(jnp.int32, (MOBA_SEL_ROWS, seq), 0)
        cols = lax.broadcasted_iota(jnp.int32, (MOBA_SEL_ROWS, seq), 1)
        own = jnp.right_shift(cols, int(math.log2(MOBA_BLOCK)))
        ind = jnp.where(own == rows, 1.0, 0.0).astype(BF16)
        km = jnp.dot(ind, k, preferred_element_type=F32) * (1.0 / MOBA_BLOCK)
        km_hi = km.astype(BF16)
        km_lo = (km - km_hi.astype(F32)).astype(BF16)
        shifts = _score_shifts(q, k, flag_ref)
        blk_lane = lax.broadcasted_iota(jnp.int32, (bq, LANES), 1)
        kaug_scr[0:pad, 0:LANES] = jnp.zeros((pad, LANES), BF16)
        vaug_scr[0:pad, 0:LANES] = jnp.zeros((pad, LANES), BF16)
        for jb in range(-(MOBA_STEP - 1), seq // bq):
            tgt = jb if jb >= 0 else LANES - 1
            kaug_scr[pad + jb * bq:pad + (jb + 1) * bq, LANES:] = jnp.where(
                (blk_lane == tgt) | (blk_lane == SHIFT_LANE), 1.0, 0.0).astype(BF16)
        kaug_scr[pad:, 0:LANES] = k
        vaug_scr[pad:, 0:LANES] = v_ref[0]
        vaug_scr[:, LANES:] = jnp.ones((seq + pad, LANES), BF16)
        lane_all = lax.broadcasted_iota(jnp.int32, (seq, LANES), 1)
        rows_f = rows.astype(F32)
        for h in range(2):
            qh = jnp.where(_head_mask(lane_all, h), q, jnp.zeros_like(q))
            g = (lax.dot_general(km_hi, qh, _NT, preferred_element_type=F32)
                 + lax.dot_general(km_lo, qh, _NT, preferred_element_type=F32))
            g = jnp.where(rows < own, g, -jnp.inf)
            bias = jnp.where(rows == own, 0.0, NEG_BIG)
            for _ in range(MOBA_TOPK):
                mx = jnp.max(g, axis=0, keepdims=True)
                first = jnp.min(jnp.where(g == mx, rows_f, float(MOBA_SEL_ROWS)), axis=0, keepdims=True)
                pick = (rows_f == first) & (mx > -jnp.inf)
                bias = jnp.where(pick, 0.0, bias)
                g = jnp.where(pick, -jnp.inf, g)
            bias = jnp.concatenate([bias, jnp.full((LANES - MOBA_SEL_ROWS, seq), NEG_BIG, F32)], axis=0).T
            qaug_scr[h, :, 0:LANES] = qh
            qaug_scr[h, :, LANES:] = jnp.where(lane_all == SHIFT_LANE, shifts[h], bias).astype(BF16)

    qrows = MOBA_QROWS
    qpb = qrows // bq
    last = (i + 1) * qpb - 1
    lane = lax.broadcasted_iota(jnp.int32, (2 * qrows, LANES), 1)
    row = lax.broadcasted_iota(jnp.int32, (2 * qrows, LANES), 0)
    qi = row & (bq - 1)
    qblk = jnp.right_shift(row, int(math.log2(bq))) & (qpb - 1)
    _softmax_init(m_scr, l_scr, acc_scr)
    parts_per_blk = bq // LANES
    q_start = pl.multiple_of(i * qrows, qrows)

    def step(t, first, fast):
        start = pl.multiple_of((last - MOBA_STEP * t) * bq, bq)
        kaug = kaug_scr[pl.ds(start, MOBA_STEP * bq), :]
        vaug = vaug_scr[pl.ds(start, MOBA_STEP * bq), :]
        parts = _lane_parts(lax.dot_general(_stacked_q(qaug_scr, q_start, qrows), kaug, _NT,
                                            preferred_element_type=F32))
        if first:
            for own in range(qpb):
                for c in range(parts_per_blk):
                    cc = (MOBA_STEP - qpb + own) * parts_per_blk + c
                    parts[cc] = jnp.where((qblk == own) & (lane + c * LANES > qi), NEG_BIG, parts[cc])
        if fast:
            _plain_step(parts, vaug, l_scr, acc_scr)
        else:
            _softmax_step(parts, vaug, m_scr, l_scr, acc_scr)

    def run(fast):
        step(0, True, fast)

        def later(t, carry):
            step(t, False, fast)
            return carry

        lax.fori_loop(1, (last + MOBA_STEP) // MOBA_STEP, later, 0)

    pl.when(flag_ref[0] == 1)(functools.partial(run, True))
    pl.when(flag_ref[0] != 1)(functools.partial(run, False))
    _softmax_finish(o_ref, l_scr, acc_scr)


def _moba(h3, n_pairs):
    b, seq, _ = h3.shape
    sec_blocks = (AB_WIDTH // 2) // LANES
    assert seq % MOBA_QROWS == 0 and seq // ATT_BLOCK <= MOBA_SEL_ROWS and MOBA_QROWS // ATT_BLOCK <= MOBA_STEP
    return pl.pallas_call(
        functools.partial(_moba_kernel, seq=seq),
        grid=(b, n_pairs, seq // MOBA_QROWS),
        in_specs=[pl.BlockSpec((1, seq, LANES), lambda bb, p, i: (bb, 0, p)),
                  pl.BlockSpec((1, seq, LANES), lambda bb, p, i: (bb, 0, sec_blocks + p)),
                  pl.BlockSpec((1, seq, LANES), lambda bb, p, i: (bb, 0, 2 * sec_blocks + p))],
        out_specs=pl.BlockSpec((1, MOBA_QROWS, LANES), lambda bb, p, i: (bb, i, p)),
        out_shape=jax.ShapeDtypeStruct((b, seq, n_pairs * LANES), BF16),
        scratch_shapes=[pltpu.VMEM((seq + (MOBA_STEP - 1) * ATT_BLOCK, 2 * LANES), BF16),
                        pltpu.VMEM((seq + (MOBA_STEP - 1) * ATT_BLOCK, 2 * LANES), BF16),
                        pltpu.VMEM((2, seq, 2 * LANES), BF16),
                        pltpu.SMEM((1,), jnp.int32),
                        pltpu.VMEM((2 * MOBA_QROWS, LANES), F32),
                        pltpu.VMEM((2 * MOBA_QROWS, LANES), F32),
                        pltpu.VMEM((2 * MOBA_QROWS, LANES), F32)],
        compiler_params=_params("parallel", "parallel", "arbitrary"),
        name="moba",
    )(h3, h3, h3)


def _dilated_n_blocks():
    return max(w for w, _ in DILATED_BRANCHES) // ATT_BLOCK + 1


def _dilated_bias_table():
    n_steps = -(-_dilated_n_blocks() // DIL_STEP)
    qi = np.arange(ATT_BLOCK)[:, None]
    ki = np.arange(ATT_BLOCK)[None, :]
    tabs = []
    for t in range(n_steps):
        groups = []
        for g in range(DIL_STEP):
            d = qi - ki + (DIL_STEP * t + DIL_STEP - 1 - g) * ATT_BLOCK
            cnt = np.zeros(d.shape, np.int64)
            for window, dil in DILATED_BRANCHES:
                cnt += ((d >= 0) & (d <= window) & (d % dil == 0)).astype(np.int64)
            groups.append(np.where(cnt > 0, np.log(np.maximum(cnt, 1).astype(np.float64)), NEG_BIG))
        tabs.append(np.concatenate(groups, axis=1))
    return np.stack(tabs).astype(np.float32)


def _dilated_kernel(q_ref, k_ref, v_ref, bias_ref, o_ref, kaug_scr, vaug_scr, qaug_scr, flag_ref, m_scr, l_scr, acc_scr,
                    *, seq):
    bq = ATT_BLOCK
    pad = (DIL_STEP - 1) * bq
    i = pl.program_id(2)

    @pl.when(i == 0)
    def _per_sequence_setup():
        q = q_ref[0]
        k = k_ref[0]
        shifts = _score_shifts(q, k, flag_ref)
        flag_lane = lax.broadcasted_iota(jnp.int32, (pad, LANES), 1)
        lane_all = lax.broadcasted_iota(jnp.int32, (seq, LANES), 1)
        kaug_scr[0:pad, 0:LANES] = jnp.zeros((pad, LANES), BF16)
        kaug_scr[0:pad, LANES:] = jnp.where((flag_lane == 0) | (flag_lane == SHIFT_LANE), 1.0, 0.0).astype(BF16)
        kaug_scr[pad:, 0:LANES] = k
        kaug_scr[pad:, LANES:] = jnp.where(lane_all == SHIFT_LANE, 1.0, 0.0).astype(BF16)
        vaug_scr[0:pad, 0:LANES] = jnp.zeros((pad, LANES), BF16)
        vaug_scr[pad:, 0:LANES] = v_ref[0]
        vaug_scr[:, LANES:] = jnp.ones((seq + pad, LANES), BF16)
        for h in range(2):
            aug = jnp.where(lane_all == SHIFT_LANE, shifts[h], jnp.where(lane_all == 0, NEG_BIG, 0.0))
            qaug_scr[h, :, 0:LANES] = jnp.where(_head_mask(lane_all, h), q, jnp.zeros_like(q))
            qaug_scr[h, :, LANES:] = aug.astype(BF16)

    _softmax_init(m_scr, l_scr, acc_scr)
    q_start = pl.multiple_of(i * bq, bq)

    def step(t, fast):
        start = pl.multiple_of((i - DIL_STEP * t) * bq, bq)
        kaug = kaug_scr[pl.ds(start, DIL_STEP * bq), :]
        vaug = vaug_scr[pl.ds(start, DIL_STEP * bq), :]
        bias = bias_ref[t]
        s = lax.dot_general(_stacked_q(qaug_scr, q_start, bq), kaug, _NT, preferred_element_type=F32)
        parts = _lane_parts(jnp.concatenate([s[0:bq, :] + bias, s[bq:, :] + bias], axis=0))
        if fast:
            _plain_step(parts, vaug, l_scr, acc_scr)
        else:
            _softmax_step(parts, vaug, m_scr, l_scr, acc_scr)

    n_steps = (jnp.minimum(i, _dilated_n_blocks() - 1) + DIL_STEP) // DIL_STEP

    def run(fast):
        def body(t, carry):
            step(t, fast)
            return carry

        lax.fori_loop(0, n_steps, body, 0)

    pl.when(flag_ref[0] == 1)(functools.partial(run, True))
    pl.when(flag_ref[0] != 1)(functools.partial(run, False))
    _softmax_finish(o_ref, l_scr, acc_scr)


def _dilated(h3, bias_tab, n_pairs):
    b, seq, _ = h3.shape
    sec_blocks = (AB_WIDTH // 2) // LANES
    base = 3 * sec_blocks
    nq = seq // ATT_BLOCK
    pad = (DIL_STEP - 1) * ATT_BLOCK
    return pl.pallas_call(
        functools.partial(_dilated_kernel, seq=seq),
        grid=(b, n_pairs, nq),
        in_specs=[pl.BlockSpec((1, seq, LANES), lambda bb, p, i: (bb, 0, base + p)),
                  pl.BlockSpec((1, seq, LANES), lambda bb, p, i: (bb, 0, base + sec_blocks + p)),
                  pl.BlockSpec((1, seq, LANES), lambda bb, p, i: (bb, 0, base + 2 * sec_blocks + p)),
                  pl.BlockSpec(bias_tab.shape, lambda bb, p, i: (0, 0, 0))],
        out_specs=pl.BlockSpec((1, ATT_BLOCK, LANES), lambda bb, p, i: (bb, i, p)),
        out_shape=jax.ShapeDtypeStruct((b, seq, n_pairs * LANES), BF16),
        scratch_shapes=[pltpu.VMEM((seq + pad, 2 * LANES), BF16),
                        pltpu.VMEM((seq + pad, 2 * LANES), BF16),
                        pltpu.VMEM((2, seq, 2 * LANES), BF16),
                        pltpu.SMEM((1,), jnp.int32),
                        pltpu.VMEM((2 * ATT_BLOCK, LANES), F32),
                        pltpu.VMEM((2 * ATT_BLOCK, LANES), F32),
                        pltpu.VMEM((2 * ATT_BLOCK, LANES), F32)],
        compiler_params=_params("parallel", "parallel", "arbitrary"),
        name="dilated",
    )(h3, h3, h3, bias_tab)


def _proj_c_kernel(x_ref, w_ref, loglb_ref, log1mlb_ref, omlb_ref, q_ref, lf_ref, kk_ref, v_ref, g_ref):
    xb = x_ref[...].astype(BF16)
    d = D_MODEL

    def sec(c):
        return jnp.dot(xb, w_ref[:, c * d:(c + 1) * d], preferred_element_type=F32)

    q_ref[...] = sec(0)
    z = sec(1)
    log_sig = jnp.minimum(z, 0.0) - jnp.log1p(jnp.exp(-jnp.abs(z)))
    a = loglb_ref[...]
    c = log1mlb_ref[...] + log_sig
    lf_ref[...] = jnp.maximum(a, c) + jnp.log1p(jnp.exp(-jnp.abs(a - c)))
    kk_ref[...] = omlb_ref[...] / (1.0 + jnp.exp(z))
    v_ref[...] = sec(2).astype(v_ref.dtype)
    g_ref[...] = sec(3)


def _proj_c(x2d, w_bf16, lb, tm=512):
    t, d = x2d.shape
    n = w_bf16.shape[1]
    lb = lb.astype(F32).reshape(1, d)
    vec_spec = pl.BlockSpec((1, d), lambda i: (0, 0))
    out_spec = pl.BlockSpec((tm, d), lambda i: (i, 0))
    sds = jax.ShapeDtypeStruct((t, d), F32)
    return pl.pallas_call(
        _proj_c_kernel,
        grid=(t // tm,),
        in_specs=[pl.BlockSpec((tm, d), lambda i: (i, 0)),
                  pl.BlockSpec((d, n), lambda i: (0, 0)),
                  vec_spec, vec_spec, vec_spec],
        out_specs=[out_spec] * 5,
        out_shape=[sds, sds, sds, jax.ShapeDtypeStruct((t, d), BF16), sds],
        compiler_params=_params("parallel"),
        name="proj_c",
    )(x2d, w_bf16, jnp.log(lb), jnp.log1p(-lb), 1.0 - lb)


def _split3_bf16(x):
    h1 = x.astype(BF16)
    r1 = x - h1.astype(F32)
    h2 = r1.astype(BF16)
    h3 = (r1 - h2.astype(F32)).astype(BF16)
    return h1, h2, h3


def _hgrn_kernel(q_ref, lf_ref, kk_ref, v_ref, g_ref, ng_ref, o_ref, *, seq):
    C, SUB = HGRN_CHUNK, HGRN_SUB
    n_sub = C // SUB
    ri = lax.broadcasted_iota(jnp.int32, (C, C), 0)
    ci = lax.broadcasted_iota(jnp.int32, (C, C), 1)
    tri = jnp.where(ci <= ri, 1.0, 0.0).astype(BF16)
    row_c = lax.broadcasted_iota(jnp.int32, (C, LANES), 0)
    row_s = lax.broadcasted_iota(jnp.int32, (SUB, C), 0)
    lane_s = lax.broadcasted_iota(jnp.int32, (SUB, C), 1)
    ng = ng_ref[...]

    def prefix_shifts(x):
        for sh in (1, 2, 4):
            x = x + jnp.where(row_c >= sh, pltpu.roll(x, sh, 0), 0.0)
        sh = 8
        while sh < C:
            x = x + jnp.concatenate([jnp.zeros((sh, LANES), F32), x[:C - sh, :]], axis=0)
            sh *= 2
        return x

    def prefix_matmul(x):
        l1, l2, l3 = _split3_bf16(x)
        return (jnp.dot(tri, l1, preferred_element_type=F32) + jnp.dot(tri, l2, preferred_element_type=F32)
                + jnp.dot(tri, l3, preferred_element_type=F32))

    def scores_factored(qc, kc, b):
        refs = [jnp.zeros((1, LANES), F32)] + [b[s * SUB - 1:s * SUB, :] for s in range(1, n_sub)]
        ref_rows = jnp.concatenate([jnp.broadcast_to(r, (SUB, LANES)) for r in refs], axis=0)
        qt = (qc * jnp.exp(b - ref_rows)).astype(BF16)
        rows_a = []
        for s in range(n_sub):
            hi = (s + 1) * SUB
            kt = (kc[:hi, :] * jnp.exp(refs[s] - b[:hi, :])).astype(BF16)
            a = lax.dot_general(qt[s * SUB:hi, :], kt, _NT, preferred_element_type=F32)
            if hi < C:
                a = jnp.concatenate([a, jnp.zeros((SUB, C - hi), F32)], axis=1)
            rows_a.append(a)
        return jnp.where(ci <= ri, jnp.concatenate(rows_a, axis=0), 0.0)

    def scores_direct(qc, kc, b):
        rows_a = []
        for sidx in range(n_sub):
            lo = sidx * SUB
            q_i = qc[lo:lo + SUB, :]
            b_i = b[lo:lo + SUB, :]
            if sidx == 0:
                a_blk = jnp.zeros((SUB, C), F32)
            else:
                ref = b[lo - 1:lo, :]
                qt = (q_i * jnp.exp(b_i - ref)).astype(BF16)
                kt = jnp.where(row_c < lo, kc * jnp.exp(jnp.minimum(ref - b, 0.0)), 0.0).astype(BF16)
                a_blk = lax.dot_general(qt, kt, _NT, preferred_element_type=F32)
            for j in range(SUB):
                r_lo = j // 8 * 8
                bj = b[lo + j:lo + j + 1, :]
                kj = kc[lo + j:lo + j + 1, :]
                pj = q_i[r_lo:, :] * (jnp.exp(jnp.minimum(b_i[r_lo:, :] - bj, 0.0)) * kj)
                col = jnp.sum(pj, axis=1, keepdims=True)
                if r_lo:
                    col = jnp.concatenate([jnp.zeros((r_lo, 1), F32), col], axis=0)
                a_blk = jnp.where((lane_s == lo + j) & (row_s >= j), col, a_blk)
            rows_a.append(a_blk)
        return jnp.concatenate(rows_a, axis=0)

    def make_chunk(scores, prefix):
        def chunk(c, st):
            r0 = pl.multiple_of(c * C, C)
            qc = q_ref[0, pl.ds(r0, C), :]
            kc = kk_ref[0, pl.ds(r0, C), :]
            vb = v_ref[0, pl.ds(r0, C), :].astype(BF16)
            b = prefix(lf_ref[0, pl.ds(r0, C), :])
            b_last = b[C - 1:C, :]
            inter = lax.dot_general((qc * jnp.exp(b)).astype(BF16), st.astype(BF16), _NT,
                                    preferred_element_type=F32)
            o = inter + jnp.dot(scores(qc, kc, b).astype(BF16), vb, preferred_element_type=F32)
            o = o * lax.rsqrt(jnp.mean(o * o, axis=1, keepdims=True) + RMS_EPS) * ng
            gc = g_ref[0, pl.ds(r0, C), :]
            o_ref[0, pl.ds(r0, C), :] = (o * (gc / (1.0 + jnp.exp(-gc)))).astype(o_ref.dtype)
            kd = (kc * jnp.exp(b_last - b)).astype(BF16)
            return st * jnp.exp(b_last) + lax.dot_general(vb, kd, _TN, preferred_element_type=F32)
        return chunk

    sub_decay = jnp.sum(lf_ref[0].reshape(seq // SUB, SUB, LANES), axis=1)
    fast_ok = jnp.min(sub_decay) >= -HGRN_FAST_LIMIT
    st0 = jnp.zeros((LANES, LANES), F32)

    @pl.when(fast_ok)
    def _():
        lax.fori_loop(0, seq // C, make_chunk(scores_factored, prefix_shifts), st0, unroll=HGRN_UNROLL)

    @pl.when(jnp.logical_not(fast_ok))
    def _():
        lax.fori_loop(0, seq // C, make_chunk(scores_direct, prefix_matmul), st0)


def _hgrn(q, lf, kk, v, g, norm_g, batch, seq):
    shp = (batch, seq, D_MODEL)
    args = [a.reshape(shp) for a in (q, lf, kk, v, g)]
    spec = pl.BlockSpec((1, seq, LANES), lambda bb, h: (bb, 0, h))
    return pl.pallas_call(
        functools.partial(_hgrn_kernel, seq=seq),
        grid=(batch, N_HEADS_C),
        in_specs=[spec] * 5 + [pl.BlockSpec((1, LANES), lambda bb, h: (0, 0))],
        out_specs=spec,
        out_shape=jax.ShapeDtypeStruct(shp, BF16),
        compiler_params=_params("parallel", "parallel"),
        name="hgrn",
    )(*args, norm_g.astype(F32).reshape(1, LANES))


def _layer_norm_rows(z, g, b):
    mu = jnp.mean(z, axis=1, keepdims=True)
    zc = z - mu
    var = jnp.mean(zc * zc, axis=1, keepdims=True)
    return zc * lax.rsqrt(var + LN_EPS) * g + b


def _first_lane_of_max(vals, lane_f):
    mx = jnp.max(vals, axis=1, keepdims=True)
    return mx, jnp.min(jnp.where(vals == mx, lane_f, float(LANES)), axis=1, keepdims=True)


def _outproj_ln_kernel(*refs, n_parts):
    o_refs, w_refs = refs[:n_parts], refs[n_parts:2 * n_parts]
    (x_ref, g_ref, b_ref, wr_hi_ref, wr_lo_ref, rb_ref,
     y_ref, yb_ref, fields_ref, info_ref, cnt_ref) = refs[2 * n_parts:]
    mix = jnp.dot(o_refs[0][...], w_refs[0][...], preferred_element_type=F32)
    for o_ref, w_ref in zip(o_refs[1:], w_refs[1:]):
        mix += jnp.dot(o_ref[...], w_ref[...], preferred_element_type=F32)
    y = _layer_norm_rows(DEEPNORM_ALPHA * x_ref[...] + mix, g_ref[...], b_ref[...])
    y_ref[...] = y
    y_hi = y.astype(BF16)
    yb_ref[...] = y_hi
    y_lo = (y - y_hi.astype(F32)).astype(BF16)
    lg = (jnp.dot(y_hi, wr_hi_ref[...], preferred_element_type=F32)
          + jnp.dot(y_lo, wr_hi_ref[...], preferred_element_type=F32)
          + jnp.dot(y_hi, wr_lo_ref[...], preferred_element_type=F32)
          + rb_ref[...])
    lane = lax.broadcasted_iota(jnp.int32, lg.shape, 1)
    lane_f = lane.astype(F32)
    is_g = lane < N_GROUPS
    mg, grp = _first_lane_of_max(jnp.where(is_g, lg, -jnp.inf), lane_f)
    pg = 1.0 / jnp.sum(jnp.where(is_g, jnp.exp(lg - mg), 0.0), axis=1, keepdims=True)
    lo = float(N_GROUPS) + float(EXPERTS_PER_GROUP) * grp
    le = jnp.where((lane_f >= lo) & (lane_f < lo + float(EXPERTS_PER_GROUP)), lg, -jnp.inf)
    m1, i1 = _first_lane_of_max(le, lane_f)
    m2, i2 = _first_lane_of_max(jnp.where(lane_f == i1, -jnp.inf, le), lane_f)
    e21 = jnp.exp(m2 - m1)
    w0 = pg / (1.0 + e21)
    w1 = pg * e21 / (1.0 + e21)
    eid0 = i1 - float(N_GROUPS)
    eid1 = i2 - float(N_GROUPS)

    @pl.when(pl.program_id(0) == 0)
    def _():
        cnt_ref[...] = jnp.zeros(cnt_ref.shape, F32)

    hits = jnp.where(lane_f == eid0, 1.0, 0.0) + jnp.where(lane_f == eid1, 1.0, 0.0)
    cnt_ref[...] += jnp.sum(hits, axis=0, keepdims=True)
    fields = jnp.where(lane == 0, eid0, jnp.where(lane == 1, eid1, jnp.where(lane == 2, w0,
                       jnp.where(lane == 3, w1, 0.0))))
    fields_ref[...] = fields
    info_ref[...] = fields.T[:8, :]


def _outproj_ln(o_parts, x2d, w_bf16, ln_g, ln_b, wr, rb, tm=1024):
    t, d = x2d.shape
    wr_hi = wr.astype(BF16)
    wr_lo = (wr - wr_hi.astype(F32)).astype(BF16)
    vec = lambda n: pl.BlockSpec((1, n), lambda i: (0, 0))
    widths = [o.shape[1] for o in o_parts]
    offs = np.cumsum([0] + widths)
    w_parts = [w_bf16[offs[k]:offs[k + 1]] for k in range(len(widths))]
    return pl.pallas_call(
        functools.partial(_outproj_ln_kernel, n_parts=len(widths)),
        grid=(t // tm,),
        in_specs=[pl.BlockSpec((tm, wd), lambda i: (i, 0)) for wd in widths]
                 + [pl.BlockSpec((wd, d), lambda i: (0, 0)) for wd in widths]
                 + [pl.BlockSpec((tm, d), lambda i: (i, 0)),
                  vec(d), vec(d),
                  pl.BlockSpec((d, LANES), lambda i: (0, 0)),
                  pl.BlockSpec((d, LANES), lambda i: (0, 0)),
                  vec(LANES)],
        out_specs=[pl.BlockSpec((tm, d), lambda i: (i, 0)),
                   pl.BlockSpec((tm, d), lambda i: (i, 0)),
                   pl.BlockSpec((tm, LANES), lambda i: (i, 0)),
                   pl.BlockSpec((8, tm), lambda i: (0, i)),
                   vec(LANES)],
        out_shape=[jax.ShapeDtypeStruct((t, d), F32),
                   jax.ShapeDtypeStruct((t, d), BF16),
                   jax.ShapeDtypeStruct((t, LANES), F32),
                   jax.ShapeDtypeStruct((8, t), F32),
                   jax.ShapeDtypeStruct((1, LANES), F32)],
        compiler_params=_params("arbitrary"),
        name="outproj_ln",
    )(*o_parts, *w_parts, x2d, ln_g.reshape(1, d), ln_b.reshape(1, d), wr_hi, wr_lo, rb)


def _expert_kernel(te_ref, nv_ref, xs_ref, w1_ref, w3_ref, w2_ref, o_ref, w1_s, w3_s, w2_s):
    i = pl.program_id(0)
    prev = te_ref[jnp.maximum(i - 1, 0)]

    @pl.when((i == 0) | (te_ref[i] != prev))
    def _cast_weights():
        w1_s[...] = w1_ref[0, 0].astype(BF16)
        w3_s[...] = w3_ref[0, 0].astype(BF16)
        w2_s[...] = w2_ref[0, 0].astype(BF16)

    @pl.when(i < nv_ref[0])
    def _ffn():
        for r in range(0, MOE_TILE, MOE_SLAB):
            xb = xs_ref[r:r + MOE_SLAB, :]
            a = jnp.dot(xb, w1_s[...], preferred_element_type=F32)
            u = jnp.dot(xb, w3_s[...], preferred_element_type=F32)
            hb = (a / (1.0 + jnp.exp(-a))) * u
            o_ref[r:r + MOE_SLAB, :] = jnp.dot(hb.astype(BF16), w2_s[...],
                                               preferred_element_type=F32).astype(o_ref.dtype)

    @pl.when(i >= nv_ref[0])
    def _pad():
        o_ref[...] = jnp.zeros(o_ref.shape, o_ref.dtype)


def _expert_ffn(tile_expert, n_valid, xs, w1, w3, w2, layer):
    p, d = xs.shape
    f = w1.shape[3]
    n_tiles = p // MOE_TILE
    grid_spec = pltpu.PrefetchScalarGridSpec(
        num_scalar_prefetch=2,
        grid=(n_tiles,),
        in_specs=[pl.BlockSpec((MOE_TILE, d), lambda i, te, nv: (i, 0)),
                  pl.BlockSpec((1, 1, d, f), lambda i, te, nv: (layer, te[i], 0, 0)),
                  pl.BlockSpec((1, 1, d, f), lambda i, te, nv: (layer, te[i], 0, 0)),
                  pl.BlockSpec((1, 1, f, d), lambda i, te, nv: (layer, te[i], 0, 0))],
        out_specs=pl.BlockSpec((MOE_TILE, d), lambda i, te, nv: (i, 0)),
        scratch_shapes=[pltpu.VMEM((d, f), BF16), pltpu.VMEM((d, f), BF16), pltpu.VMEM((f, d), BF16)],
    )
    return pl.pallas_call(
        _expert_kernel,
        grid_spec=grid_spec,
        out_shape=jax.ShapeDtypeStruct((p, d), BF16),
        compiler_params=_params("arbitrary"),
        name="expert_ffn",
    )(tile_expert, n_valid, xs, w1, w3, w2)


def _combine_ln_kernel(x_ref, y0_ref, y1_ref, fields_ref, g_ref, b_ref, o_ref):
    fields = fields_ref[...]
    ffn = fields[:, 2:3] * y0_ref[...].astype(F32) + fields[:, 3:4] * y1_ref[...].astype(F32)
    o_ref[...] = _layer_norm_rows(DEEPNORM_ALPHA * x_ref[...] + ffn, g_ref[...], b_ref[...])


def _combine_ln(x2d, y0, y1, fields, ln_g, ln_b, tm=512):
    t, d = x2d.shape
    row = pl.BlockSpec((tm, d), lambda i: (i, 0))
    vec = pl.BlockSpec((1, d), lambda i: (0, 0))
    return pl.pallas_call(
        _combine_ln_kernel,
        grid=(t // tm,),
        in_specs=[row, row, row, pl.BlockSpec((tm, LANES), lambda i: (i, 0)), vec, vec],
        out_specs=row,
        out_shape=jax.ShapeDtypeStruct((t, d), F32),
        compiler_params=_params("parallel"),
        name="combine_ln",
    )(x2d, y0, y1, fields, ln_g.reshape(1, d), ln_b.reshape(1, d))


def _dispatch_plan(info, counts_f):
    t = info.shape[1]
    a = t * TOPK_IN_GROUP
    n_fill = N_EXPERTS * MOE_TILE
    n_tiles = a // MOE_TILE + N_EXPERTS
    experts = jnp.arange(N_EXPERTS, dtype=jnp.int32)
    counts = counts_f[0, :N_EXPERTS].astype(jnp.int32)
    padded = (counts + MOE_TILE - 1) // MOE_TILE * MOE_TILE
    pends = jnp.cumsum(padded)
    fill_ends = jnp.cumsum(padded - counts)
    fill = jnp.arange(n_fill, dtype=jnp.int32)
    fill_key = jnp.sum((fill_ends[None, :] <= fill[:, None]).astype(jnp.int32), axis=1)
    keys = jnp.concatenate([info[0].astype(jnp.int32), info[1].astype(jnp.int32), fill_key])
    flat = jnp.arange(a + n_fill, dtype=jnp.int32)
    flat_bits = int(a + n_fill - 1).bit_length()
    assert flat_bits + int(N_EXPERTS).bit_length() < 32
    slot_flat = jnp.sort(jnp.left_shift(keys, flat_bits) | flat) & ((1 << flat_bits) - 1)
    slot_tok = jnp.where(slot_flat < a, slot_flat % t, (slot_flat - a) % t)
    _, slot_of = lax.sort((slot_flat, flat), num_keys=1)
    tile_start = jnp.arange(n_tiles, dtype=jnp.int32) * MOE_TILE
    tile_expert = jnp.sum((pends[None, :] <= tile_start[:, None]).astype(jnp.int32), axis=1)
    last_used = jnp.max(jnp.where(counts > 0, experts, 0))
    tile_expert = jnp.minimum(tile_expert, last_used)
    n_valid = (pends[-1] // MOE_TILE).astype(jnp.int32).reshape(1)
    return tile_expert, n_valid, slot_tok, slot_of[:t], slot_of[t:a]


def _moe_and_norm(x1, x1_bf16, fields, info, counts_f, w1, w3, w2, layer, ln_g, ln_b):
    tile_expert, n_valid, slot_tok, dest0, dest1 = _dispatch_plan(info, counts_f)
    yb = _expert_ffn(tile_expert, n_valid, x1_bf16[slot_tok], w1, w3, w2, layer)
    return _combine_ln(x1, yb[dest0], yb[dest1], fields, ln_g, ln_b)


def kernel(x, ab_w_in, ab_w_out, c_w_in, c_w_out, c_norm_g, hgrn_lb_logits, ln_g, ln_b,
           router_g_w, router_g_b, router_e_w, router_e_b, exp_w1, exp_w3, exp_w2):
    batch, seq, d = x.shape
    t = batch * seq
    tables = _rope_lane_tables(seq)
    bias_tab = jnp.asarray(_dilated_bias_table())
    lb_all = jnp.cumsum(jax.nn.softmax(hgrn_lb_logits.astype(F32), axis=0), axis=0)
    lb_all = lb_all - lb_all[0:1]
    n_pairs = (AB_WIDTH // 2) // LANES

    xc = x.reshape(t, d)
    for l in range(DEPTH):
        j = l // 2
        if l % 2 == 0:
            h = _proj_ab(xc, ab_w_in[j].astype(BF16), tables, seq).reshape(batch, seq, 3 * AB_WIDTH)
            o_parts = [_moba(h, n_pairs).reshape(t, AB_WIDTH // 2),
                       _dilated(h, bias_tab, n_pairs).reshape(t, AB_WIDTH // 2)]
            w_out = ab_w_out[j]
        else:
            q, lf, kk, v, g = _proj_c(xc, c_w_in[j].astype(BF16), lb_all[j])
            o_parts = [_hgrn(q, lf, kk, v, g, c_norm_g[j], batch, seq).reshape(t, d)]
            w_out = c_w_out[j]
        wr = jnp.zeros((d, LANES), F32)
        wr = wr.at[:, :N_GROUPS].set(router_g_w[l]).at[:, N_GROUPS:N_GROUPS + N_EXPERTS].set(router_e_w[l])
        rb = jnp.zeros((1, LANES), F32)
        rb = rb.at[0, :N_GROUPS].set(router_g_b[l]).at[0, N_GROUPS:N_GROUPS + N_EXPERTS].set(router_e_b[l])
        x1, x1_bf16, fields, info, counts_f = _outproj_ln(o_parts, xc, w_out.astype(BF16), ln_g[l, 0], ln_b[l, 0],
                                                          wr, rb)
        xc = _moe_and_norm(x1, x1_bf16, fields, info, counts_f, exp_w1, exp_w3, exp_w2, l,
                           ln_g[l, 1], ln_b[l, 1])
    return xc.reshape(batch, seq, d)
```

```python
import functools
import math

import numpy as np
import jax
import jax.numpy as jnp
from jax import lax
from jax.experimental import pallas as pl
from jax.experimental.pallas import tpu as pltpu

F32 = jnp.float32
BF16 = jnp.bfloat16

D_MODEL = 1024
DEPTH = 4
HEAD_DIM = 64
N_HEADS_A = 8
N_HEADS_B = 8
AB_WIDTH = (N_HEADS_A + N_HEADS_B) * HEAD_DIM
ROT_DIM = HEAD_DIM // 4
ROPE_THETA = 500000.0
MOBA_BLOCK = 256
MOBA_TOPK = 3
DILATED_BRANCHES = ((128, 1), (512, 4), (2048, 16))
HGRN_EXPAND = 128
N_HEADS_C = D_MODEL // HGRN_EXPAND
HGRN_CHUNK = 64
HGRN_SUB = 64
HGRN_UNROLL = 8
HGRN_FAST_LIMIT = 85.0
N_GROUPS = 4
EXPERTS_PER_GROUP = 8
N_EXPERTS = N_GROUPS * EXPERTS_PER_GROUP
TOPK_IN_GROUP = 2
D_EXPERT = D_MODEL // 2
DEEPNORM_ALPHA = (2.0 * DEPTH) ** 0.25
LN_EPS = 1e-5
RMS_EPS = 1e-6

LANES = 128
ATT_BLOCK = 256
MOBA_STEP = 4
MOBA_SEL_ROWS = 16
MOBA_QROWS = 512
SHIFT_LANE = LANES - 2
ATT_FAST_LIMIT = 38.0
DIL_STEP = 3
NEG_BIG = -1e30
MOE_TILE = 512
MOE_SLAB = 512
VMEM_LIMIT = 56 * 1024 * 1024

_NT = (((1,), (1,)), ((), ()))
_TN = (((0,), (0,)), ((), ()))


def _params(*sem):
    return pltpu.CompilerParams(dimension_semantics=sem, vmem_limit_bytes=VMEM_LIMIT)


def _proj_ab_kernel(x_ref, w_ref, cos_ref, sa_ref, sb_ref, o_ref):
    xb = x_ref[...].astype(BF16)
    sec = AB_WIDTH // 2
    for c in range(6):
        acc = jnp.dot(xb, w_ref[:, c * sec:(c + 1) * sec], preferred_element_type=F32)
        if c % 3 == 2:
            o_ref[:, c * sec:(c + 1) * sec] = acc.astype(BF16)
            continue
        cosv = cos_ref[...]
        sa = sa_ref[...]
        sb = sb_ref[...]
        for k in range(sec // LANES):
            a = acc[:, k * LANES:(k + 1) * LANES]
            r = a * cosv + pltpu.roll(a, LANES - ROT_DIM // 2, 1) * sa + pltpu.roll(a, ROT_DIM // 2, 1) * sb
            if c % 3 == 0:
                r = r * (HEAD_DIM ** -0.5)
            o_ref[:, c * sec + k * LANES:c * sec + (k + 1) * LANES] = r.astype(BF16)


def _rope_lane_tables(seq):
    half = ROT_DIM // 2
    inv = ROPE_THETA ** (-jnp.arange(half, dtype=F32) / half)
    ang = jnp.arange(seq, dtype=F32)[:, None] * inv[None, :]
    cos, sin = jnp.cos(ang), jnp.sin(ang)
    ones = jnp.ones((seq, HEAD_DIM - ROT_DIM), F32)
    zeros = jnp.zeros((seq, HEAD_DIM - ROT_DIM), F32)
    zh = jnp.zeros((seq, half), F32)
    cos_t = jnp.concatenate([cos, cos, ones], axis=1)
    sa_t = jnp.concatenate([-sin, zh, zeros], axis=1)
    sb_t = jnp.concatenate([zh, sin, zeros], axis=1)
    rep = LANES // HEAD_DIM
    return tuple(jnp.tile(t, (1, rep)) for t in (cos_t, sa_t, sb_t))


def _proj_ab(x2d, w_bf16, tables, seq, tm=1024):
    t, d = x2d.shape
    n = w_bf16.shape[1]
    nseq = seq // tm
    tab_spec = pl.BlockSpec((tm, LANES), lambda i: (i % nseq, 0))
    return pl.pallas_call(
        _proj_ab_kernel,
        grid=(t // tm,),
        in_specs=[pl.BlockSpec((tm, d), lambda i: (i, 0)),
                  pl.BlockSpec((d, n), lambda i: (0, 0)),
                  tab_spec, tab_spec, tab_spec],
        out_specs=pl.BlockSpec((tm, n), lambda i: (i, 0)),
        out_shape=jax.ShapeDtypeStruct((t, n), BF16),
        compiler_params=_params("parallel"),
        name="proj_ab",
    )(x2d, w_bf16, *tables)


def _softmax_step(parts, vaug, m_scr, l_scr, acc_scr):
    m_prev = m_scr[...]
    mx = functools.reduce(jnp.maximum, parts)
    m_new = jnp.maximum(m_prev, jnp.max(mx, axis=1, keepdims=True))
    alpha = jnp.exp(m_prev - m_new)
    p = jnp.concatenate([jnp.exp(part - m_new).astype(BF16) for part in parts], axis=1)
    pv = jnp.dot(p, vaug, preferred_element_type=F32)
    acc_scr[...] = alpha * acc_scr[...] + pv[:, :LANES]
    l_scr[...] = alpha * l_scr[...] + pv[:, LANES:]
    m_scr[...] = m_new


def _plain_step(parts, vaug, l_scr, acc_scr):
    p = jnp.concatenate([jnp.exp(part).astype(BF16) for part in parts], axis=1)
    pv = jnp.dot(p, vaug, preferred_element_type=F32)
    acc_scr[...] += pv[:, :LANES]
    l_scr[...] += pv[:, LANES:]


def _stacked_q(qaug_scr, q_start, bq):
    return jnp.concatenate([qaug_scr[0, pl.ds(q_start, bq), :], qaug_scr[1, pl.ds(q_start, bq), :]], axis=0)


def _lane_parts(s):
    return [s[:, c * LANES:(c + 1) * LANES] for c in range(s.shape[1] // LANES)]


def _head_mask(lane, h):
    return (lane >= h * HEAD_DIM) & (lane < (h + 1) * HEAD_DIM)


def _head_sq_norms(x, h):
    lane = lax.broadcasted_iota(jnp.int32, x.shape, 1)
    xf = x.astype(F32)
    sq = jnp.where(_head_mask(lane, h), xf * xf, 0.0).astype(BF16)
    return jnp.dot(sq, jnp.ones((LANES, LANES), BF16), preferred_element_type=F32)


def _score_shifts(q, k, flag_ref):
    bounds = [jnp.sqrt(jnp.max(_head_sq_norms(q, h), axis=0, keepdims=True)
                       * jnp.max(_head_sq_norms(k, h), axis=0, keepdims=True)) for h in range(2)]
    fast_ok = jnp.max(jnp.maximum(bounds[0], bounds[1])) <= ATT_FAST_LIMIT
    flag_ref[0] = jnp.where(fast_ok, 1, 0).astype(jnp.int32)
    return [jnp.where(fast_ok, -b, 0.0) for b in bounds]


def _softmax_init(m_scr, l_scr, acc_scr):
    m_scr[...] = jnp.full(m_scr.shape, -jnp.inf, F32)
    l_scr[...] = jnp.zeros(l_scr.shape, F32)
    acc_scr[...] = jnp.zeros(acc_scr.shape, F32)


def _softmax_finish(o_ref, l_scr, acc_scr):
    bq = acc_scr.shape[0] // 2
    lane = lax.broadcasted_iota(jnp.int32, (bq, LANES), 1)
    o0 = acc_scr[0:bq, :] / l_scr[0:bq, :]
    o1 = acc_scr[bq:, :] / l_scr[bq:, :]
    o_ref[0] = jnp.where(lane < HEAD_DIM, o0, o1).astype(o_ref.dtype)


def _moba_kernel(q_ref, k_ref, v_ref, o_ref, kaug_scr, vaug_scr, qaug_scr, flag_ref, m_scr, l_scr, acc_scr, *, seq):
    bq = ATT_BLOCK
    pad = (MOBA_STEP - 1) * bq
    i = pl.program_id(2)

    @pl.when(i == 0)
    def _per_sequence_setup():
        q = q_ref[0]
        k = k_ref[0]
        rows = lax.broadcasted_iota(jnp.int32, (MOBA_SEL_ROWS, seq), 0)
        cols = lax.broadcasted_iota(jnp.int32, (MOBA_SEL_ROWS, seq), 1)
        own = jnp.right_shift(cols, int(math.log2(MOBA_BLOCK)))
        ind = jnp.where(own == rows, 1.0, 0.0).astype(BF16)
        km = jnp.dot(ind, k, preferred_element_type=F32) * (1.0 / MOBA_BLOCK)
        km_hi = km.astype(BF16)
        km_lo = (km - km_hi.astype(F32)).astype(BF16)
        shifts = _score_shifts(q, k, flag_ref)
        blk_lane = lax.broadcasted_iota(jnp.int32, (bq, LANES), 1)
        kaug_scr[0:pad, 0:LANES] = jnp.zeros((pad, LANES), BF16)
        vaug_scr[0:pad, 0:LANES] = jnp.zeros((pad, LANES), BF16)
        for jb in range(-(MOBA_STEP - 1), seq // bq):
            tgt = jb if jb >= 0 else LANES - 1
            kaug_scr[pad + jb * bq:pad + (jb + 1) * bq, LANES:] = jnp.where(
                (blk_lane == tgt) | (blk_lane == SHIFT_LANE), 1.0, 0.0).astype(BF16)
        kaug_scr[pad:, 0:LANES] = k
        vaug_scr[pad:, 0:LANES] = v_ref[0]
        vaug_scr[:, LANES:] = jnp.ones((seq + pad, LANES), BF16)
        lane_all = lax.broadcasted_iota(jnp.int32, (seq, LANES), 1)
        rows_f = rows.astype(F32)
        for h in range(2):
            qh = jnp.where(_head_mask(lane_all, h), q, jnp.zeros_like(q))
            g = (lax.dot_general(km_hi, qh, _NT, preferred_element_type=F32)
                 + lax.dot_general(km_lo, qh, _NT, preferred_element_type=F32))
            g = jnp.where(rows < own, g, -jnp.inf)
            bias = jnp.where(rows == own, 0.0, NEG_BIG)
            for _ in range(MOBA_TOPK):
                mx = jnp.max(g, axis=0, keepdims=True)
                first = jnp.min(jnp.where(g == mx, rows_f, float(MOBA_SEL_ROWS)), axis=0, keepdims=True)
                pick = (rows_f == first) & (mx > -jnp.inf)
                bias = jnp.where(pick, 0.0, bias)
                g = jnp.where(pick, -jnp.inf, g)
            bias = jnp.concatenate([bias, jnp.full((LANES - MOBA_SEL_ROWS, seq), NEG_BIG, F32)], axis=0).T
            qaug_scr[h, :, 0:LANES] = qh
            qaug_scr[h, :, LANES:] = jnp.where(lane_all == SHIFT_LANE, shifts[h], bias).astype(BF16)

    qrows = MOBA_QROWS
    qpb = qrows // bq
    last = (i + 1) * qpb - 1
    lane = lax.broadcasted_iota(jnp.int32, (2 * qrows, LANES), 1)
    row = lax.broadcasted_iota(jnp.int32, (2 * qrows, LANES), 0)
    qi = row & (bq - 1)
    qblk = jnp.right_shift(row, int(math.log2(bq))) & (qpb - 1)
    _softmax_init(m_scr, l_scr, acc_scr)
    parts_per_blk = bq // LANES
    q_start = pl.multiple_of(i * qrows, qrows)

    def step(t, first, fast):
        start = pl.multiple_of((last - MOBA_STEP * t) * bq, bq)
        kaug = kaug_scr[pl.ds(start, MOBA_STEP * bq), :]
        vaug = vaug_scr[pl.ds(start, MOBA_STEP * bq), :]
        parts = _lane_parts(lax.dot_general(_stacked_q(qaug_scr, q_start, qrows), kaug, _NT,
                                            preferred_element_type=F32))
        if first:
            for own in range(qpb):
                for c in range(parts_per_blk):
                    cc = (MOBA_STEP - qpb + own) * parts_per_blk + c
                    parts[cc] = jnp.where((qblk == own) & (lane + c * LANES > qi), NEG_BIG, parts[cc])
        if fast:
            _plain_step(parts, vaug, l_scr, acc_scr)
        else:
            _softmax_step(parts, vaug, m_scr, l_scr, acc_scr)

    def run(fast):
        step(0, True, fast)

        def later(t, carry):
            step(t, False, fast)
            return carry

        lax.fori_loop(1, (last + MOBA_STEP) // MOBA_STEP, later, 0)

    pl.when(flag_ref[0] == 1)(functools.partial(run, True))
    pl.when(flag_ref[0] != 1)(functools.partial(run, False))
    _softmax_finish(o_ref, l_scr, acc_scr)


def _moba(h3, n_pairs):
    b, seq, _ = h3.shape
    sec_blocks = (AB_WIDTH // 2) // LANES
    assert seq % MOBA_QROWS == 0 and seq // ATT_BLOCK <= MOBA_SEL_ROWS and MOBA_QROWS // ATT_BLOCK <= MOBA_STEP
    return pl.pallas_call(
        functools.partial(_moba_kernel, seq=seq),
        grid=(b, n_pairs, seq // MOBA_QROWS),
        in_specs=[pl.BlockSpec((1, seq, LANES), lambda bb, p, i: (bb, 0, p)),
                  pl.BlockSpec((1, seq, LANES), lambda bb, p, i: (bb, 0, sec_blocks + p)),
                  pl.BlockSpec((1, seq, LANES), lambda bb, p, i: (bb, 0, 2 * sec_blocks + p))],
        out_specs=pl.BlockSpec((1, MOBA_QROWS, LANES), lambda bb, p, i: (bb, i, p)),
        out_shape=jax.ShapeDtypeStruct((b, seq, n_pairs * LANES), BF16),
        scratch_shapes=[pltpu.VMEM((seq + (MOBA_STEP - 1) * ATT_BLOCK, 2 * LANES), BF16),
                        pltpu.VMEM((seq + (MOBA_STEP - 1) * ATT_BLOCK, 2 * LANES), BF16),
                        pltpu.VMEM((2, seq, 2 * LANES), BF16),
                        pltpu.SMEM((1,), jnp.int32),
                        pltpu.VMEM((2 * MOBA_QROWS, LANES), F32),
                        pltpu.VMEM((2 * MOBA_QROWS, LANES), F32),
                        pltpu.VMEM((2 * MOBA_QROWS, LANES), F32)],
        compiler_params=_params("parallel", "parallel", "arbitrary"),
        name="moba",
    )(h3, h3, h3)


def _dilated_n_blocks():
    return max(w for w, _ in DILATED_BRANCHES) // ATT_BLOCK + 1


def _dilated_bias_table():
    n_steps = -(-_dilated_n_blocks() // DIL_STEP)
    qi = np.arange(ATT_BLOCK)[:, None]
    ki = np.arange(ATT_BLOCK)[None, :]
    tabs = []
    for t in range(n_steps):
        groups = []
        for g in range(DIL_STEP):
            d = qi - ki + (DIL_STEP * t + DIL_STEP - 1 - g) * ATT_BLOCK
            cnt = np.zeros(d.shape, np.int64)
            for window, dil in DILATED_BRANCHES:
                cnt += ((d >= 0) & (d <= window) & (d % dil == 0)).astype(np.int64)
            groups.append(np.where(cnt > 0, np.log(np.maximum(cnt, 1).astype(np.float64)), NEG_BIG))
        tabs.append(np.concatenate(groups, axis=1))
    return np.stack(tabs).astype(np.float32)


def _dilated_kernel(q_ref, k_ref, v_ref, bias_ref, o_ref, kaug_scr, vaug_scr, qaug_scr, flag_ref, m_scr, l_scr, acc_scr,
                    *, seq):
    bq = ATT_BLOCK
    pad = (DIL_STEP - 1) * bq
    i = pl.program_id(2)

    @pl.when(i == 0)
    def _per_sequence_setup():
        q = q_ref[0]
        k = k_ref[0]
        shifts = _score_shifts(q, k, flag_ref)
        flag_lane = lax.broadcasted_iota(jnp.int32, (pad, LANES), 1)
        lane_all = lax.broadcasted_iota(jnp.int32, (seq, LANES), 1)
        kaug_scr[0:pad, 0:LANES] = jnp.zeros((pad, LANES), BF16)
        kaug_scr[0:pad, LANES:] = jnp.where((flag_lane == 0) | (flag_lane == SHIFT_LANE), 1.0, 0.0).astype(BF16)
        kaug_scr[pad:, 0:LANES] = k
        kaug_scr[pad:, LANES:] = jnp.where(lane_all == SHIFT_LANE, 1.0, 0.0).astype(BF16)
        vaug_scr[0:pad, 0:LANES] = jnp.zeros((pad, LANES), BF16)
        vaug_scr[pad:, 0:LANES] = v_ref[0]
        vaug_scr[:, LANES:] = jnp.ones((seq + pad, LANES), BF16)
        for h in range(2):
            aug = jnp.where(lane_all == SHIFT_LANE, shifts[h], jnp.where(lane_all == 0, NEG_BIG, 0.0))
            qaug_scr[h, :, 0:LANES] = jnp.where(_head_mask(lane_all, h), q, jnp.zeros_like(q))
            qaug_scr[h, :, LANES:] = aug.astype(BF16)

    _softmax_init(m_scr, l_scr, acc_scr)
    q_start = pl.multiple_of(i * bq, bq)

    def step(t, fast):
        start = pl.multiple_of((i - DIL_STEP * t) * bq, bq)
        kaug = kaug_scr[pl.ds(start, DIL_STEP * bq), :]
        vaug = vaug_scr[pl.ds(start, DIL_STEP * bq), :]
        bias = bias_ref[t]
        s = lax.dot_general(_stacked_q(qaug_scr, q_start, bq), kaug, _NT, preferred_element_type=F32)
        parts = _lane_parts(jnp.concatenate([s[0:bq, :] + bias, s[bq:, :] + bias], axis=0))
        if fast:
            _plain_step(parts, vaug, l_scr, acc_scr)
        else:
            _softmax_step(parts, vaug, m_scr, l_scr, acc_scr)

    n_steps = (jnp.minimum(i, _dilated_n_blocks() - 1) + DIL_STEP) // DIL_STEP

    def run(fast):
        def body(t, carry):
            step(t, fast)
            return carry

        lax.fori_loop(0, n_steps, body, 0)

    pl.when(flag_ref[0] == 1)(functools.partial(run, True))
    pl.when(flag_ref[0] != 1)(functools.partial(run, False))
    _softmax_finish(o_ref, l_scr, acc_scr)


def _dilated(h3, bias_tab, n_pairs):
    b, seq, _ = h3.shape
    sec_blocks = (AB_WIDTH // 2) // LANES
    base = 3 * sec_blocks
    nq = seq // ATT_BLOCK
    pad = (DIL_STEP - 1) * ATT_BLOCK
    return pl.pallas_call(
        functools.partial(_dilated_kernel, seq=seq),
        grid=(b, n_pairs, nq),
        in_specs=[pl.BlockSpec((1, seq, LANES), lambda bb, p, i: (bb, 0, base + p)),
                  pl.BlockSpec((1, seq, LANES), lambda bb, p, i: (bb, 0, base + sec_blocks + p)),
                  pl.BlockSpec((1, seq, LANES), lambda bb, p, i: (bb, 0, base + 2 * sec_blocks + p)),
                  pl.BlockSpec(bias_tab.shape, lambda bb, p, i: (0, 0, 0))],
        out_specs=pl.BlockSpec((1, ATT_BLOCK, LANES), lambda bb, p, i: (bb, i, p)),
        out_shape=jax.ShapeDtypeStruct((b, seq, n_pairs * LANES), BF16),
        scratch_shapes=[pltpu.VMEM((seq + pad, 2 * LANES), BF16),
                        pltpu.VMEM((seq + pad, 2 * LANES), BF16),
                        pltpu.VMEM((2, seq, 2 * LANES), BF16),
                        pltpu.SMEM((1,), jnp.int32),
                        pltpu.VMEM((2 * ATT_BLOCK, LANES), F32),
                        pltpu.VMEM((2 * ATT_BLOCK, LANES), F32),
                        pltpu.VMEM((2 * ATT_BLOCK, LANES), F32)],
        compiler_params=_params("parallel", "parallel", "arbitrary"),
        name="dilated",
    )(h3, h3, h3, bias_tab)


def _proj_c_kernel(x_ref, w_ref, loglb_ref, log1mlb_ref, omlb_ref, q_ref, lf_ref, kk_ref, v_ref, g_ref):
    xb = x_ref[...].astype(BF16)
    d = D_MODEL

    def sec(c):
        return jnp.dot(xb, w_ref[:, c * d:(c + 1) * d], preferred_element_type=F32)

    q_ref[...] = sec(0)
    z = sec(1)
    log_sig = jnp.minimum(z, 0.0) - jnp.log1p(jnp.exp(-jnp.abs(z)))
    a = loglb_ref[...]
    c = log1mlb_ref[...] + log_sig
    lf_ref[...] = jnp.maximum(a, c) + jnp.log1p(jnp.exp(-jnp.abs(a - c)))
    kk_ref[...] = omlb_ref[...] / (1.0 + jnp.exp(z))
    v_ref[...] = sec(2).astype(v_ref.dtype)
    g_ref[...] = sec(3)


def _proj_c(x2d, w_bf16, lb, tm=512):
    t, d = x2d.shape
    n = w_bf16.shape[1]
    lb = lb.astype(F32).reshape(1, d)
    vec_spec = pl.BlockSpec((1, d), lambda i: (0, 0))
    out_spec = pl.BlockSpec((tm, d), lambda i: (i, 0))
    sds = jax.ShapeDtypeStruct((t, d), F32)
    return pl.pallas_call(
        _proj_c_kernel,
        grid=(t // tm,),
        in_specs=[pl.BlockSpec((tm, d), lambda i: (i, 0)),
                  pl.BlockSpec((d, n), lambda i: (0, 0)),
                  vec_spec, vec_spec, vec_spec],
        out_specs=[out_spec] * 5,
        out_shape=[sds, sds, sds, jax.ShapeDtypeStruct((t, d), BF16), sds],
        compiler_params=_params("parallel"),
        name="proj_c",
    )(x2d, w_bf16, jnp.log(lb), jnp.log1p(-lb), 1.0 - lb)


def _split3_bf16(x):
    h1 = x.astype(BF16)
    r1 = x - h1.astype(F32)
    h2 = r1.astype(BF16)
    h3 = (r1 - h2.astype(F32)).astype(BF16)
    return h1, h2, h3


def _hgrn_kernel(q_ref, lf_ref, kk_ref, v_ref, g_ref, ng_ref, o_ref, *, seq):
    C, SUB = HGRN_CHUNK, HGRN_SUB
    n_sub = C // SUB
    ri = lax.broadcasted_iota(jnp.int32, (C, C), 0)
    ci = lax.broadcasted_iota(jnp.int32, (C, C), 1)
    tri = jnp.where(ci <= ri, 1.0, 0.0).astype(BF16)
    row_c = lax.broadcasted_iota(jnp.int32, (C, LANES), 0)
    row_s = lax.broadcasted_iota(jnp.int32, (SUB, C), 0)
    lane_s = lax.broadcasted_iota(jnp.int32, (SUB, C), 1)
    ng = ng_ref[...]

    def prefix_shifts(x):
        for sh in (1, 2, 4):
            x = x + jnp.where(row_c >= sh, pltpu.roll(x, sh, 0), 0.0)
        sh = 8
        while sh < C:
            x = x + jnp.concatenate([jnp.zeros((sh, LANES), F32), x[:C - sh, :]], axis=0)
            sh *= 2
        return x

    def prefix_matmul(x):
        l1, l2, l3 = _split3_bf16(x)
        return (jnp.dot(tri, l1, preferred_element_type=F32) + jnp.dot(tri, l2, preferred_element_type=F32)
                + jnp.dot(tri, l3, preferred_element_type=F32))

    def scores_factored(qc, kc, b):
        refs = [jnp.zeros((1, LANES), F32)] + [b[s * SUB - 1:s * SUB, :] for s in range(1, n_sub)]
        ref_rows = jnp.concatenate([jnp.broadcast_to(r, (SUB, LANES)) for r in refs], axis=0)
        qt = (qc * jnp.exp(b - ref_rows)).astype(BF16)
        rows_a = []
        for s in range(n_sub):
            hi = (s + 1) * SUB
            kt = (kc[:hi, :] * jnp.exp(refs[s] - b[:hi, :])).astype(BF16)
            a = lax.dot_general(qt[s * SUB:hi, :], kt, _NT, preferred_element_type=F32)
            if hi < C:
                a = jnp.concatenate([a, jnp.zeros((SUB, C - hi), F32)], axis=1)
            rows_a.append(a)
        return jnp.where(ci <= ri, jnp.concatenate(rows_a, axis=0), 0.0)

    def scores_direct(qc, kc, b):
        rows_a = []
        for sidx in range(n_sub):
            lo = sidx * SUB
            q_i = qc[lo:lo + SUB, :]
            b_i = b[lo:lo + SUB, :]
            if sidx == 0:
                a_blk = jnp.zeros((SUB, C), F32)
            else:
                ref = b[lo - 1:lo, :]
                qt = (q_i * jnp.exp(b_i - ref)).astype(BF16)
                kt = jnp.where(row_c < lo, kc * jnp.exp(jnp.minimum(ref - b, 0.0)), 0.0).astype(BF16)
                a_blk = lax.dot_general(qt, kt, _NT, preferred_element_type=F32)
            for j in range(SUB):
                r_lo = j // 8 * 8
                bj = b[lo + j:lo + j + 1, :]
                kj = kc[lo + j:lo + j + 1, :]
                pj = q_i[r_lo:, :] * (jnp.exp(jnp.minimum(b_i[r_lo:, :] - bj, 0.0)) * kj)
                col = jnp.sum(pj, axis=1, keepdims=True)
                if r_lo:
                    col = jnp.concatenate([jnp.zeros((r_lo, 1), F32), col], axis=0)
                a_blk = jnp.where((lane_s == lo + j) & (row_s >= j), col, a_blk)
            rows_a.append(a_blk)
        return jnp.concatenate(rows_a, axis=0)

    def make_chunk(scores, prefix):
        def chunk(c, st):
            r0 = pl.multiple_of(c * C, C)
            qc = q_ref[0, pl.ds(r0, C), :]
            kc = kk_ref[0, pl.ds(r0, C), :]
            vb = v_ref[0, pl.ds(r0, C), :].astype(BF16)
            b = prefix(lf_ref[0, pl.ds(r0, C), :])
            b_last = b[C - 1:C, :]
            inter = lax.dot_general((qc * jnp.exp(b)).astype(BF16), st.astype(BF16), _NT,
                                    preferred_element_type=F32)
            o = inter + jnp.dot(scores(qc, kc, b).astype(BF16), vb, preferred_element_type=F32)
            o = o * lax.rsqrt(jnp.mean(o * o, axis=1, keepdims=True) + RMS_EPS) * ng
            gc = g_ref[0, pl.ds(r0, C), :]
            o_ref[0, pl.ds(r0, C), :] = (o * (gc / (1.0 + jnp.exp(-gc)))).astype(o_ref.dtype)
            kd = (kc * jnp.exp(b_last - b)).astype(BF16)
            return st * jnp.exp(b_last) + lax.dot_general(vb, kd, _TN, preferred_element_type=F32)
        return chunk

    sub_decay = jnp.sum(lf_ref[0].reshape(seq // SUB, SUB, LANES), axis=1)
    fast_ok = jnp.min(sub_decay) >= -HGRN_FAST_LIMIT
    st0 = jnp.zeros((LANES, LANES), F32)

    @pl.when(fast_ok)
    def _():
        lax.fori_loop(0, seq // C, make_chunk(scores_factored, prefix_shifts), st0, unroll=HGRN_UNROLL)

    @pl.when(jnp.logical_not(fast_ok))
    def _():
        lax.fori_loop(0, seq // C, make_chunk(scores_direct, prefix_matmul), st0)


def _hgrn(q, lf, kk, v, g, norm_g, batch, seq):
    shp = (batch, seq, D_MODEL)
    args = [a.reshape(shp) for a in (q, lf, kk, v, g)]
    spec = pl.BlockSpec((1, seq, LANES), lambda bb, h: (bb, 0, h))
    return pl.pallas_call(
        functools.partial(_hgrn_kernel, seq=seq),
        grid=(batch, N_HEADS_C),
        in_specs=[spec] * 5 + [pl.BlockSpec((1, LANES), lambda bb, h: (0, 0))],
        out_specs=spec,
        out_shape=jax.ShapeDtypeStruct(shp, BF16),
        compiler_params=_params("parallel", "parallel"),
        name="hgrn",
    )(*args, norm_g.astype(F32).reshape(1, LANES))


def _layer_norm_rows(z, g, b):
    mu = jnp.mean(z, axis=1, keepdims=True)
    zc = z - mu
    var = jnp.mean(zc * zc, axis=1, keepdims=True)
    return zc * lax.rsqrt(var + LN_EPS) * g + b


def _first_lane_of_max(vals, lane_f):
    mx = jnp.max(vals, axis=1, keepdims=True)
    return mx, jnp.min(jnp.where(vals == mx, lane_f, float(LANES)), axis=1, keepdims=True)


def _outproj_ln_kernel(*refs, n_parts):
    o_refs, w_refs = refs[:n_parts], refs[n_parts:2 * n_parts]
    (x_ref, g_ref, b_ref, wr_hi_ref, wr_lo_ref, rb_ref,
     y_ref, yb_ref, fields_ref, info_ref, cnt_ref) = refs[2 * n_parts:]
    mix = jnp.dot(o_refs[0][...], w_refs[0][...], preferred_element_type=F32)
    for o_ref, w_ref in zip(o_refs[1:], w_refs[1:]):
        mix += jnp.dot(o_ref[...], w_ref[...], preferred_element_type=F32)
    y = _layer_norm_rows(DEEPNORM_ALPHA * x_ref[...] + mix, g_ref[...], b_ref[...])
    y_ref[...] = y
    y_hi = y.astype(BF16)
    yb_ref[...] = y_hi
    y_lo = (y - y_hi.astype(F32)).astype(BF16)
    lg = (jnp.dot(y_hi, wr_hi_ref[...], preferred_element_type=F32)
          + jnp.dot(y_lo, wr_hi_ref[...], preferred_element_type=F32)
          + jnp.dot(y_hi, wr_lo_ref[...], preferred_element_type=F32)
          + rb_ref[...])
    lane = lax.broadcasted_iota(jnp.int32, lg.shape, 1)
    lane_f = lane.astype(F32)
    is_g = lane < N_GROUPS
    mg, grp = _first_lane_of_max(jnp.where(is_g, lg, -jnp.inf), lane_f)
    pg = 1.0 / jnp.sum(jnp.where(is_g, jnp.exp(lg - mg), 0.0), axis=1, keepdims=True)
    lo = float(N_GROUPS) + float(EXPERTS_PER_GROUP) * grp
    le = jnp.where((lane_f >= lo) & (lane_f < lo + float(EXPERTS_PER_GROUP)), lg, -jnp.inf)
    m1, i1 = _first_lane_of_max(le, lane_f)
    m2, i2 = _first_lane_of_max(jnp.where(lane_f == i1, -jnp.inf, le), lane_f)
    e21 = jnp.exp(m2 - m1)
    w0 = pg / (1.0 + e21)
    w1 = pg * e21 / (1.0 + e21)
    eid0 = i1 - float(N_GROUPS)
    eid1 = i2 - float(N_GROUPS)

    @pl.when(pl.program_id(0) == 0)
    def _():
        cnt_ref[...] = jnp.zeros(cnt_ref.shape, F32)

    hits = jnp.where(lane_f == eid0, 1.0, 0.0) + jnp.where(lane_f == eid1, 1.0, 0.0)
    cnt_ref[...] += jnp.sum(hits, axis=0, keepdims=True)
    fields = jnp.where(lane == 0, eid0, jnp.where(lane == 1, eid1, jnp.where(lane == 2, w0,
                       jnp.where(lane == 3, w1, 0.0))))
    fields_ref[...] = fields
    info_ref[...] = fields.T[:8, :]


def _outproj_ln(o_parts, x2d, w_bf16, ln_g, ln_b, wr, rb, tm=1024):
    t, d = x2d.shape
    wr_hi = wr.astype(BF16)
    wr_lo = (wr - wr_hi.astype(F32)).astype(BF16)
    vec = lambda n: pl.BlockSpec((1, n), lambda i: (0, 0))
    widths = [o.shape[1] for o in o_parts]
    offs = np.cumsum([0] + widths)
    w_parts = [w_bf16[offs[k]:offs[k + 1]] for k in range(len(widths))]
    return pl.pallas_call(
        functools.partial(_outproj_ln_kernel, n_parts=len(widths)),
        grid=(t // tm,),
        in_specs=[pl.BlockSpec((tm, wd), lambda i: (i, 0)) for wd in widths]
                 + [pl.BlockSpec((wd, d), lambda i: (0, 0)) for wd in widths]
                 + [pl.BlockSpec((tm, d), lambda i: (i, 0)),
                  vec(d), vec(d),
                  pl.BlockSpec((d, LANES), lambda i: (0, 0)),
                  pl.BlockSpec((d, LANES), lambda i: (0, 0)),
                  vec(LANES)],
        out_specs=[pl.BlockSpec((tm, d), lambda i: (i, 0)),
                   pl.BlockSpec((tm, d), lambda i: (i, 0)),
                   pl.BlockSpec((tm, LANES), lambda i: (i, 0)),
                   pl.BlockSpec((8, tm), lambda i: (0, i)),
                   vec(LANES)],
        out_shape=[jax.ShapeDtypeStruct((t, d), F32),
                   jax.ShapeDtypeStruct((t, d), BF16),
                   jax.ShapeDtypeStruct((t, LANES), F32),
                   jax.ShapeDtypeStruct((8, t), F32),
                   jax.ShapeDtypeStruct((1, LANES), F32)],
        compiler_params=_params("arbitrary"),
        name="outproj_ln",
    )(*o_parts, *w_parts, x2d, ln_g.reshape(1, d), ln_b.reshape(1, d), wr_hi, wr_lo, rb)


def _expert_kernel(te_ref, nv_ref, xs_ref, w1_ref, w3_ref, w2_ref, o_ref, w1_s, w3_s, w2_s):
    i = pl.program_id(0)
    prev = te_ref[jnp.maximum(i - 1, 0)]

    @pl.when((i == 0) | (te_ref[i] != prev))
    def _cast_weights():
        w1_s[...] = w1_ref[0, 0].astype(BF16)
        w3_s[...] = w3_ref[0, 0].astype(BF16)
        w2_s[...] = w2_ref[0, 0].astype(BF16)

    @pl.when(i < nv_ref[0])
    def _ffn():
        for r in range(0, MOE_TILE, MOE_SLAB):
            xb = xs_ref[r:r + MOE_SLAB, :]
            a = jnp.dot(xb, w1_s[...], preferred_element_type=F32)
            u = jnp.dot(xb, w3_s[...], preferred_element_type=F32)
            hb = (a / (1.0 + jnp.exp(-a))) * u
            o_ref[r:r + MOE_SLAB, :] = jnp.dot(hb.astype(BF16), w2_s[...],
                                               preferred_element_type=F32).astype(o_ref.dtype)

    @pl.when(i >= nv_ref[0])
    def _pad():
        o_ref[...] = jnp.zeros(o_ref.shape, o_ref.dtype)


def _expert_ffn(tile_expert, n_valid, xs, w1, w3, w2, layer):
    p, d = xs.shape
    f = w1.shape[3]
    n_tiles = p // MOE_TILE
    grid_spec = pltpu.PrefetchScalarGridSpec(
        num_scalar_prefetch=2,
        grid=(n_tiles,),
        in_specs=[pl.BlockSpec((MOE_TILE, d), lambda i, te, nv: (i, 0)),
                  pl.BlockSpec((1, 1, d, f), lambda i, te, nv: (layer, te[i], 0, 0)),
                  pl.BlockSpec((1, 1, d, f), lambda i, te, nv: (layer, te[i], 0, 0)),
                  pl.BlockSpec((1, 1, f, d), lambda i, te, nv: (layer, te[i], 0, 0))],
        out_specs=pl.BlockSpec((MOE_TILE, d), lambda i, te, nv: (i, 0)),
        scratch_shapes=[pltpu.VMEM((d, f), BF16), pltpu.VMEM((d, f), BF16), pltpu.VMEM((f, d), BF16)],
    )
    return pl.pallas_call(
        _expert_kernel,
        grid_spec=grid_spec,
        out_shape=jax.ShapeDtypeStruct((p, d), BF16),
        compiler_params=_params("arbitrary"),
        name="expert_ffn",
    )(tile_expert, n_valid, xs, w1, w3, w2)


def _combine_ln_kernel(x_ref, y0_ref, y1_ref, fields_ref, g_ref, b_ref, o_ref):
    fields = fields_ref[...]
    ffn = fields[:, 2:3] * y0_ref[...].astype(F32) + fields[:, 3:4] * y1_ref[...].astype(F32)
    o_ref[...] = _layer_norm_rows(DEEPNORM_ALPHA * x_ref[...] + ffn, g_ref[...], b_ref[...])


def _combine_ln(x2d, y0, y1, fields, ln_g, ln_b, tm=512):
    t, d = x2d.shape
    row = pl.BlockSpec((tm, d), lambda i: (i, 0))
    vec = pl.BlockSpec((1, d), lambda i: (0, 0))
    return pl.pallas_call(
        _combine_ln_kernel,
        grid=(t // tm,),
        in_specs=[row, row, row, pl.BlockSpec((tm, LANES), lambda i: (i, 0)), vec, vec],
        out_specs=row,
        out_shape=jax.ShapeDtypeStruct((t, d), F32),
        compiler_params=_params("parallel"),
        name="combine_ln",
    )(x2d, y0, y1, fields, ln_g.reshape(1, d), ln_b.reshape(1, d))


def _dispatch_plan(info, counts_f):
    t = info.shape[1]
    a = t * TOPK_IN_GROUP
    n_fill = N_EXPERTS * MOE_TILE
    n_tiles = a // MOE_TILE + N_EXPERTS
    experts = jnp.arange(N_EXPERTS, dtype=jnp.int32)
    counts = counts_f[0, :N_EXPERTS].astype(jnp.int32)
    padded = (counts + MOE_TILE - 1) // MOE_TILE * MOE_TILE
    pends = jnp.cumsum(padded)
    fill_ends = jnp.cumsum(padded - counts)
    fill = jnp.arange(n_fill, dtype=jnp.int32)
    fill_key = jnp.sum((fill_ends[None, :] <= fill[:, None]).astype(jnp.int32), axis=1)
    keys = jnp.concatenate([info[0].astype(jnp.int32), info[1].astype(jnp.int32), fill_key])
    flat = jnp.arange(a + n_fill, dtype=jnp.int32)
    flat_bits = int(a + n_fill - 1).bit_length()
    assert flat_bits + int(N_EXPERTS).bit_length() < 32
    slot_flat = jnp.sort(jnp.left_shift(keys, flat_bits) | flat) & ((1 << flat_bits) - 1)
    slot_tok = jnp.where(slot_flat < a, slot_flat % t, (slot_flat - a) % t)
    _, slot_of = lax.sort((slot_flat, flat), num_keys=1)
    tile_start = jnp.arange(n_tiles, dtype=jnp.int32) * MOE_TILE
    tile_expert = jnp.sum((pends[None, :] <= tile_start[:, None]).astype(jnp.int32), axis=1)
    last_used = jnp.max(jnp.where(counts > 0, experts, 0))
    tile_expert = jnp.minimum(tile_expert, last_used)
    n_valid = (pends[-1] // MOE_TILE).astype(jnp.int32).reshape(1)
    return tile_expert, n_valid, slot_tok, slot_of[:t], slot_of[t:a]


def _moe_and_norm(x1, x1_bf16, fields, info, counts_f, w1, w3, w2, layer, ln_g, ln_b):
    tile_expert, n_valid, slot_tok, dest0, dest1 = _dispatch_plan(info, counts_f)
    yb = _expert_ffn(tile_expert, n_valid, x1_bf16[slot_tok], w1, w3, w2, layer)
    return _combine_ln(x1, yb[dest0], yb[dest1], fields, ln_g, ln_b)


def kernel(x, ab_w_in, ab_w_out, c_w_in, c_w_out, c_norm_g, hgrn_lb_logits, ln_g, ln_b,
           router_g_w, router_g_b, router_e_w, router_e_b, exp_w1, exp_w3, exp_w2):
    batch, seq, d = x.shape
    t = batch * seq
    tables = _rope_lane_tables(seq)
    bias_tab = jnp.asarray(_dilated_bias_table())
    lb_all = jnp.cumsum(jax.nn.softmax(hgrn_lb_logits.astype(F32), axis=0), axis=0)
    lb_all = lb_all - lb_all[0:1]
    n_pairs = (AB_WIDTH // 2) // LANES

    xc = x.reshape(t, d)
    for l in range(DEPTH):
        j = l // 2
        if l % 2 == 0:
            h = _proj_ab(xc, ab_w_in[j].astype(BF16), tables, seq).reshape(batch, seq, 3 * AB_WIDTH)
            o_parts = [_moba(h, n_pairs).reshape(t, AB_WIDTH // 2),
                       _dilated(h, bias_tab, n_pairs).reshape(t, AB_WIDTH // 2)]
            w_out = ab_w_out[j]
        else:
            q, lf, kk, v, g = _proj_c(xc, c_w_in[j].astype(BF16), lb_all[j])
            o_parts = [_hgrn(q, lf, kk, v, g, c_norm_g[j], batch, seq).reshape(t, d)]
            w_out = c_w_out[j]
        wr = jnp.zeros((d, LANES), F32)
        wr = wr.at[:, :N_GROUPS].set(router_g_w[l]).at[:, N_GROUPS:N_GROUPS + N_EXPERTS].set(router_e_w[l])
        rb = jnp.zeros((1, LANES), F32)
        rb = rb.at[0, :N_GROUPS].set(router_g_b[l]).at[0, N_GROUPS:N_GROUPS + N_EXPERTS].set(router_e_b[l])
        x1, x1_bf16, fields, info, counts_f = _outproj_ln(o_parts, xc, w_out.astype(BF16), ln_g[l, 0], ln_b[l, 0],
                                                          wr, rb)
        xc = _moe_and_norm(x1, x1_bf16, fields, info, counts_f, exp_w1, exp_w3, exp_w2, l,
                           ln_g[l, 1], ln_b[l, 1])
    return xc.reshape(batch, seq, d)
```

```python
import functools
import math

import numpy as np
import jax
import jax.numpy as jnp
from jax import lax
from jax.experimental import pallas as pl
from jax.experimental.pallas import tpu as pltpu

F32 = jnp.float32
BF16 = jnp.bfloat16

D_MODEL = 1024
DEPTH = 4
HEAD_DIM = 64
N_HEADS_A = 8
N_HEADS_B = 8
AB_WIDTH = (N_HEADS_A + N_HEADS_B) * HEAD_DIM
ROT_DIM = HEAD_DIM // 4
ROPE_THETA = 500000.0
MOBA_BLOCK = 256
MOBA_TOPK = 3
DILATED_BRANCHES = ((128, 1), (512, 4), (2048, 16))
HGRN_EXPAND = 128
N_HEADS_C = D_MODEL // HGRN_EXPAND
HGRN_CHUNK = 64
HGRN_SUB = 32
HGRN_UNROLL = 8
HGRN_FAST_LIMIT = 80.0
N_GROUPS = 4
EXPERTS_PER_GROUP = 8
N_EXPERTS = N_GROUPS * EXPERTS_PER_GROUP
TOPK_IN_GROUP = 2
D_EXPERT = D_MODEL // 2
DEEPNORM_ALPHA = (2.0 * DEPTH) ** 0.25
LN_EPS = 1e-5
RMS_EPS = 1e-6

LANES = 128
ATT_BLOCK = 256
MOBA_STEP = 4
MOBA_SEL_ROWS = 16
MOBA_QROWS = 512
SHIFT_LANE = LANES - 2
ATT_FAST_LIMIT = 38.0
DIL_STEP = 3
NEG_BIG = -1e30
MOE_TILE = 512
MOE_SLAB = 512
VMEM_LIMIT = 56 * 1024 * 1024

_NT = (((1,), (1,)), ((), ()))
_TN = (((0,), (0,)), ((), ()))


def _params(*sem):
    return pltpu.CompilerParams(dimension_semantics=sem, vmem_limit_bytes=VMEM_LIMIT)


def _proj_ab_kernel(x_ref, w_ref, cos_ref, sa_ref, sb_ref, o_ref):
    xb = x_ref[...].astype(BF16)
    sec = AB_WIDTH // 2
    for c in range(6):
        acc = jnp.dot(xb, w_ref[:, c * sec:(c + 1) * sec], preferred_element_type=F32)
        if c % 3 == 2:
            o_ref[:, c * sec:(c + 1) * sec] = acc.astype(BF16)
            continue
        cosv = cos_ref[...]
        sa = sa_ref[...]
        sb = sb_ref[...]
        for k in range(sec // LANES):
            a = acc[:, k * LANES:(k + 1) * LANES]
            r = a * cosv + pltpu.roll(a, LANES - ROT_DIM // 2, 1) * sa + pltpu.roll(a, ROT_DIM // 2, 1) * sb
            if c % 3 == 0:
                r = r * (HEAD_DIM ** -0.5)
            o_ref[:, c * sec + k * LANES:c * sec + (k + 1) * LANES] = r.astype(BF16)


def _rope_lane_tables(seq):
    half = ROT_DIM // 2
    inv = ROPE_THETA ** (-jnp.arange(half, dtype=F32) / half)
    ang = jnp.arange(seq, dtype=F32)[:, None] * inv[None, :]
    cos, sin = jnp.cos(ang), jnp.sin(ang)
    ones = jnp.ones((seq, HEAD_DIM - ROT_DIM), F32)
    zeros = jnp.zeros((seq, HEAD_DIM - ROT_DIM), F32)
    zh = jnp.zeros((seq, half), F32)
    cos_t = jnp.concatenate([cos, cos, ones], axis=1)
    sa_t = jnp.concatenate([-sin, zh, zeros], axis=1)
    sb_t = jnp.concatenate([zh, sin, zeros], axis=1)
    rep = LANES // HEAD_DIM
    return tuple(jnp.tile(t, (1, rep)) for t in (cos_t, sa_t, sb_t))


def _proj_ab(x2d, w_bf16, tables, seq, tm=1024):
    t, d = x2d.shape
    n = w_bf16.shape[1]
    nseq = seq // tm
    tab_spec = pl.BlockSpec((tm, LANES), lambda i: (i % nseq, 0))
    return pl.pallas_call(
        _proj_ab_kernel,
        grid=(t // tm,),
        in_specs=[pl.BlockSpec((tm, d), lambda i: (i, 0)),
                  pl.BlockSpec((d, n), lambda i: (0, 0)),
                  tab_spec, tab_spec, tab_spec],
        out_specs=pl.BlockSpec((tm, n), lambda i: (i, 0)),
        out_shape=jax.ShapeDtypeStruct((t, n), BF16),
        compiler_params=_params("parallel"),
        name="proj_ab",
    )(x2d, w_bf16, *tables)


def _softmax_step(parts, vaug, m_scr, l_scr, acc_scr):
    m_prev = m_scr[...]
    mx = functools.reduce(jnp.maximum, parts)
    m_new = jnp.maximum(m_prev, jnp.max(mx, axis=1, keepdims=True))
    alpha = jnp.exp(m_prev - m_new)
    p = jnp.concatenate([jnp.exp(part - m_new).astype(BF16) for part in parts], axis=1)
    pv = jnp.dot(p, vaug, preferred_element_type=F32)
    acc_scr[...] = alpha * acc_scr[...] + pv[:, :LANES]
    l_scr[...] = alpha * l_scr[...] + pv[:, LANES:]
    m_scr[...] = m_new


def _plain_step(parts, vaug, l_scr, acc_scr):
    p = jnp.concatenate([jnp.exp(part).astype(BF16) for part in parts], axis=1)
    pv = jnp.dot(p, vaug, preferred_element_type=F32)
    acc_scr[...] += pv[:, :LANES]
    l_scr[...] += pv[:, LANES:]


def _stacked_q(qaug_scr, q_start, bq):
    return jnp.concatenate([qaug_scr[0, pl.ds(q_start, bq), :], qaug_scr[1, pl.ds(q_start, bq), :]], axis=0)


def _lane_parts(s):
    return [s[:, c * LANES:(c + 1) * LANES] for c in range(s.shape[1] // LANES)]


def _head_mask(lane, h):
    return (lane >= h * HEAD_DIM) & (lane < (h + 1) * HEAD_DIM)


def _head_sq_norms(x, h):
    lane = lax.broadcasted_iota(jnp.int32, x.shape, 1)
    xf = x.astype(F32)
    sq = jnp.where(_head_mask(lane, h), xf * xf, 0.0).astype(BF16)
    return jnp.dot(sq, jnp.ones((LANES, LANES), BF16), preferred_element_type=F32)


def _score_shifts(q, k, flag_ref):
    bounds = [jnp.sqrt(jnp.max(_head_sq_norms(q, h), axis=0, keepdims=True)
                       * jnp.max(_head_sq_norms(k, h), axis=0, keepdims=True)) for h in range(2)]
    fast_ok = jnp.max(jnp.maximum(bounds[0], bounds[1])) <= ATT_FAST_LIMIT
    flag_ref[0] = jnp.where(fast_ok, 1, 0).astype(jnp.int32)
    return [jnp.where(fast_ok, -b, 0.0) for b in bounds]


def _softmax_init(m_scr, l_scr, acc_scr):
    m_scr[...] = jnp.full(m_scr.shape, -jnp.inf, F32)
    l_scr[...] = jnp.zeros(l_scr.shape, F32)
    acc_scr[...] = jnp.zeros(acc_scr.shape, F32)


def _softmax_finish(o_ref, l_scr, acc_scr):
    bq = acc_scr.shape[0] // 2
    lane = lax.broadcasted_iota(jnp.int32, (bq, LANES), 1)
    o0 = acc_scr[0:bq, :] / l_scr[0:bq, :]
    o1 = acc_scr[bq:, :] / l_scr[bq:, :]
    o_ref[0] = jnp.where(lane < HEAD_DIM, o0, o1).astype(o_ref.dtype)


def _moba_kernel(q_ref, k_ref, v_ref, o_ref, kaug_scr, vaug_scr, qaug_scr, flag_ref, m_scr, l_scr, acc_scr, *, seq):
    bq = ATT_BLOCK
    pad = (MOBA_STEP - 1) * bq
    i = pl.program_id(2)

    @pl.when(i == 0)
    def _per_sequence_setup():
        q = q_ref[0]
        k = k_ref[0]
        rows = lax.broadcasted_iota(jnp.int32, (MOBA_SEL_ROWS, seq), 0)
        cols = lax.broadcasted_iota(jnp.int32, (MOBA_SEL_ROWS, seq), 1)
        own = jnp.right_shift(cols, int(math.log2(MOBA_BLOCK)))
        ind = jnp.where(own == rows, 1.0, 0.0).astype(BF16)
        km = jnp.dot(ind, k, preferred_element_type=F32) * (1.0 / MOBA_BLOCK)
        km_hi = km.astype(BF16)
        km_lo = (km - km_hi.astype(F32)).astype(BF16)
        shifts = _score_shifts(q, k, flag_ref)
        blk_lane = lax.broadcasted_iota(jnp.int32, (bq, LANES), 1)
        kaug_scr[0:pad, 0:LANES] = jnp.zeros((pad, LANES), BF16)
        vaug_scr[0:pad, 0:LANES] = jnp.zeros((pad, LANES), BF16)
        for jb in range(-(MOBA_STEP - 1), seq // bq):
            tgt = jb if jb >= 0 else LANES - 1
            kaug_scr[pad + jb * bq:pad + (jb + 1) * bq, LANES:] = jnp.where(
                (blk_lane == tgt) | (blk_lane == SHIFT_LANE), 1.0, 0.0).astype(BF16)
        kaug_scr[pad:, 0:LANES] = k
        vaug_scr[pad:, 0:LANES] = v_ref[0]
        vaug_scr[:, LANES:] = jnp.ones((seq + pad, LANES), BF16)
        lane_all = lax.broadcasted_iota(jnp.int32, (seq, LANES), 1)
        rows_f = rows.astype(F32)
        for h in range(2):
            qh = jnp.where(_head_mask(lane_all, h), q, jnp.zeros_like(q))
            g = (lax.dot_general(km_hi, qh, _NT, preferred_element_type=F32)
                 + lax.dot_general(km_lo, qh, _NT, preferred_element_type=F32))
            g = jnp.where(rows < own, g, -jnp.inf)
            bias = jnp.where(rows == own, 0.0, NEG_BIG)
            for _ in range(MOBA_TOPK):
                mx = jnp.max(g, axis=0, keepdims=True)
                first = jnp.min(jnp.where(g == mx, rows_f, float(MOBA_SEL_ROWS)), axis=0, keepdims=True)
                pick = (rows_f == first) & (mx > -jnp.inf)
                bias = jnp.where(pick, 0.0, bias)
                g = jnp.where(pick, -jnp.inf, g)
            bias = jnp.concatenate([bias, jnp.full((LANES - MOBA_SEL_ROWS, seq), NEG_BIG, F32)], axis=0).T
            qaug_scr[h, :, 0:LANES] = qh
            qaug_scr[h, :, LANES:] = jnp.where(lane_all == SHIFT_LANE, shifts[h], bias).astype(BF16)

    qrows = MOBA_QROWS
    qpb = qrows // bq
    last = (i + 1) * qpb - 1
    lane = lax.broadcasted_iota(jnp.int32, (2 * qrows, LANES), 1)
    row = lax.broadcasted_iota(jnp.int32, (2 * qrows, LANES), 0)
    qi = row & (bq - 1)
    qblk = jnp.right_shift(row, int(math.log2(bq))) & (qpb - 1)
    _softmax_init(m_scr, l_scr, acc_scr)
    parts_per_blk = bq // LANES
    q_start = pl.multiple_of(i * qrows, qrows)

    def step(t, first, fast):
        start = pl.multiple_of((last - MOBA_STEP * t) * bq, bq)
        kaug = kaug_scr[pl.ds(start, MOBA_STEP * bq), :]
        vaug = vaug_scr[pl.ds(start, MOBA_STEP * bq), :]
        parts = _lane_parts(lax.dot_general(_stacked_q(qaug_scr, q_start, qrows), kaug, _NT,
                                            preferred_element_type=F32))
        if first:
            for own in range(qpb):
                for c in range(parts_per_blk):
                    cc = (MOBA_STEP - qpb + own) * parts_per_blk + c
                    parts[cc] = jnp.where((qblk == own) & (lane + c * LANES > qi), NEG_BIG, parts[cc])
        if fast:
            _plain_step(parts, vaug, l_scr, acc_scr)
        else:
            _softmax_step(parts, vaug, m_scr, l_scr, acc_scr)

    def run(fast):
        step(0, True, fast)

        def later(t, carry):
            step(t, False, fast)
            return carry

        lax.fori_loop(1, (last + MOBA_STEP) // MOBA_STEP, later, 0)

    pl.when(flag_ref[0] == 1)(functools.partial(run, True))
    pl.when(flag_ref[0] != 1)(functools.partial(run, False))
    _softmax_finish(o_ref, l_scr, acc_scr)


def _moba(h3, n_pairs):
    b, seq, _ = h3.shape
    sec_blocks = (AB_WIDTH // 2) // LANES
    assert seq % MOBA_QROWS == 0 and seq // ATT_BLOCK <= MOBA_SEL_ROWS and MOBA_QROWS // ATT_BLOCK <= MOBA_STEP
    return pl.pallas_call(
        functools.partial(_moba_kernel, seq=seq),
        grid=(b, n_pairs, seq // MOBA_QROWS),
        in_specs=[pl.BlockSpec((1, seq, LANES), lambda bb, p, i: (bb, 0, p)),
                  pl.BlockSpec((1, seq, LANES), lambda bb, p, i: (bb, 0, sec_blocks + p)),
                  pl.BlockSpec((1, seq, LANES), lambda bb, p, i: (bb, 0, 2 * sec_blocks + p))],
        out_specs=pl.BlockSpec((1, MOBA_QROWS, LANES), lambda bb, p, i: (bb, i, p)),
        out_shape=jax.ShapeDtypeStruct((b, seq, n_pairs * LANES), BF16),
        scratch_shapes=[pltpu.VMEM((seq + (MOBA_STEP - 1) * ATT_BLOCK, 2 * LANES), BF16),
                        pltpu.VMEM((seq + (MOBA_STEP - 1) * ATT_BLOCK, 2 * LANES), BF16),
                        pltpu.VMEM((2, seq, 2 * LANES), BF16),
                        pltpu.SMEM((1,), jnp.int32),
                        pltpu.VMEM((2 * MOBA_QROWS, LANES), F32),
                        pltpu.VMEM((2 * MOBA_QROWS, LANES), F32),
                        pltpu.VMEM((2 * MOBA_QROWS, LANES), F32)],
        compiler_params=_params("parallel", "parallel", "arbitrary"),
        name="moba",
    )(h3, h3, h3)


def _dilated_n_blocks():
    return max(w for w, _ in DILATED_BRANCHES) // ATT_BLOCK + 1


def _dilated_bias_table():
    n_steps = -(-_dilated_n_blocks() // DIL_STEP)
    qi = np.arange(ATT_BLOCK)[:, None]
    ki = np.arange(ATT_BLOCK)[None, :]
    tabs = []
    for t in range(n_steps):
        groups = []
        for g in range(DIL_STEP):
            d = qi - ki + (DIL_STEP * t + DIL_STEP - 1 - g) * ATT_BLOCK
            cnt = np.zeros(d.shape, np.int64)
            for window, dil in DILATED_BRANCHES:
                cnt += ((d >= 0) & (d <= window) & (d % dil == 0)).astype(np.int64)
            groups.append(np.where(cnt > 0, np.log(np.maximum(cnt, 1).astype(np.float64)), NEG_BIG))
        tabs.append(np.concatenate(groups, axis=1))
    return np.stack(tabs).astype(np.float32)


def _dilated_kernel(q_ref, k_ref, v_ref, bias_ref, o_ref, kaug_scr, vaug_scr, qaug_scr, flag_ref, m_scr, l_scr, acc_scr,
                    *, seq):
    bq = ATT_BLOCK
    pad = (DIL_STEP - 1) * bq
    i = pl.program_id(2)

    @pl.when(i == 0)
    def _per_sequence_setup():
        q = q_ref[0]
        k = k_ref[0]
        shifts = _score_shifts(q, k, flag_ref)
        flag_lane = lax.broadcasted_iota(jnp.int32, (pad, LANES), 1)
        lane_all = lax.broadcasted_iota(jnp.int32, (seq, LANES), 1)
        kaug_scr[0:pad, 0:LANES] = jnp.zeros((pad, LANES), BF16)
        kaug_scr[0:pad, LANES:] = jnp.where((flag_lane == 0) | (flag_lane == SHIFT_LANE), 1.0, 0.0).astype(BF16)
        kaug_scr[pad:, 0:LANES] = k
        kaug_scr[pad:, LANES:] = jnp.where(lane_all == SHIFT_LANE, 1.0, 0.0).astype(BF16)
        vaug_scr[0:pad, 0:LANES] = jnp.zeros((pad, LANES), BF16)
        vaug_scr[pad:, 0:LANES] = v_ref[0]
        vaug_scr[:, LANES:] = jnp.ones((seq + pad, LANES), BF16)
        for h in range(2):
            aug = jnp.where(lane_all == SHIFT_LANE, shifts[h], jnp.where(lane_all == 0, NEG_BIG, 0.0))
            qaug_scr[h, :, 0:LANES] = jnp.where(_head_mask(lane_all, h), q, jnp.zeros_like(q))
            qaug_scr[h, :, LANES:] = aug.astype(BF16)

    _softmax_init(m_scr, l_scr, acc_scr)
    q_start = pl.multiple_of(i * bq, bq)

    def step(t, fast):
        start = pl.multiple_of((i - DIL_STEP * t) * bq, bq)
        kaug = kaug_scr[pl.ds(start, DIL_STEP * bq), :]
        vaug = vaug_scr[pl.ds(start, DIL_STEP * bq), :]
        bias = bias_ref[t]
        s = lax.dot_general(_stacked_q(qaug_scr, q_start, bq), kaug, _NT, preferred_element_type=F32)
        parts = _lane_parts(jnp.concatenate([s[0:bq, :] + bias, s[bq:, :] + bias], axis=0))
        if fast:
            _plain_step(parts, vaug, l_scr, acc_scr)
        else:
            _softmax_step(parts, vaug, m_scr, l_scr, acc_scr)

    n_steps = (jnp.minimum(i, _dilated_n_blocks() - 1) + DIL_STEP) // DIL_STEP

    def run(fast):
        def body(t, carry):
            step(t, fast)
            return carry

        lax.fori_loop(0, n_steps, body, 0)

    pl.when(flag_ref[0] == 1)(functools.partial(run, True))
    pl.when(flag_ref[0] != 1)(functools.partial(run, False))
    _softmax_finish(o_ref, l_scr, acc_scr)


def _dilated(h3, bias_tab, n_pairs):
    b, seq, _ = h3.shape
    sec_blocks = (AB_WIDTH // 2) // LANES
    base = 3 * sec_blocks
    nq = seq // ATT_BLOCK
    pad = (DIL_STEP - 1) * ATT_BLOCK
    return pl.pallas_call(
        functools.partial(_dilated_kernel, seq=seq),
        grid=(b, n_pairs, nq),
        in_specs=[pl.BlockSpec((1, seq, LANES), lambda bb, p, i: (bb, 0, base + p)),
                  pl.BlockSpec((1, seq, LANES), lambda bb, p, i: (bb, 0, base + sec_blocks + p)),
                  pl.BlockSpec((1, seq, LANES), lambda bb, p, i: (bb, 0, base + 2 * sec_blocks + p)),
                  pl.BlockSpec(bias_tab.shape, lambda bb, p, i: (0, 0, 0))],
        out_specs=pl.BlockSpec((1, ATT_BLOCK, LANES), lambda bb, p, i: (bb, i, p)),
        out_shape=jax.ShapeDtypeStruct((b, seq, n_pairs * LANES), BF16),
        scratch_shapes=[pltpu.VMEM((seq + pad, 2 * LANES), BF16),
                        pltpu.VMEM((seq + pad, 2 * LANES), BF16),
                        pltpu.VMEM((2, seq, 2 * LANES), BF16),
                        pltpu.SMEM((1,), jnp.int32),
                        pltpu.VMEM((2 * ATT_BLOCK, LANES), F32),
                        pltpu.VMEM((2 * ATT_BLOCK, LANES), F32),
                        pltpu.VMEM((2 * ATT_BLOCK, LANES), F32)],
        compiler_params=_params("parallel", "parallel", "arbitrary"),
        name="dilated",
    )(h3, h3, h3, bias_tab)


def _proj_c_kernel(x_ref, w_ref, loglb_ref, log1mlb_ref, omlb_ref, q_ref, lf_ref, kk_ref, v_ref, g_ref):
    xb = x_ref[...].astype(BF16)
    d = D_MODEL

    def sec(c):
        return jnp.dot(xb, w_ref[:, c * d:(c + 1) * d], preferred_element_type=F32)

    q_ref[...] = sec(0)
    z = sec(1)
    log_sig = jnp.minimum(z, 0.0) - jnp.log1p(jnp.exp(-jnp.abs(z)))
    a = loglb_ref[...]
    c = log1mlb_ref[...] + log_sig
    lf_ref[...] = jnp.maximum(a, c) + jnp.log1p(jnp.exp(-jnp.abs(a - c)))
    kk_ref[...] = omlb_ref[...] / (1.0 + jnp.exp(z))
    v_ref[...] = sec(2).astype(v_ref.dtype)
    g_ref[...] = sec(3)


def _proj_c(x2d, w_bf16, lb, tm=512):
    t, d = x2d.shape
    n = w_bf16.shape[1]
    lb = lb.astype(F32).reshape(1, d)
    vec_spec = pl.BlockSpec((1, d), lambda i: (0, 0))
    out_spec = pl.BlockSpec((tm, d), lambda i: (i, 0))
    sds = jax.ShapeDtypeStruct((t, d), F32)
    return pl.pallas_call(
        _proj_c_kernel,
        grid=(t // tm,),
        in_specs=[pl.BlockSpec((tm, d), lambda i: (i, 0)),
                  pl.BlockSpec((d, n), lambda i: (0, 0)),
                  vec_spec, vec_spec, vec_spec],
        out_specs=[out_spec] * 5,
        out_shape=[sds, sds, sds, jax.ShapeDtypeStruct((t, d), BF16), sds],
        compiler_params=_params("parallel"),
        name="proj_c",
    )(x2d, w_bf16, jnp.log(lb), jnp.log1p(-lb), 1.0 - lb)


def _split3_bf16(x):
    h1 = x.astype(BF16)
    r1 = x - h1.astype(F32)
    h2 = r1.astype(BF16)
    h3 = (r1 - h2.astype(F32)).astype(BF16)
    return h1, h2, h3


def _hgrn_kernel(q_ref, lf_ref, kk_ref, v_ref, g_ref, ng_ref, o_ref, *, seq):
    C, SUB = HGRN_CHUNK, HGRN_SUB
    n_sub = C // SUB
    ri = lax.broadcasted_iota(jnp.int32, (C, C), 0)
    ci = lax.broadcasted_iota(jnp.int32, (C, C), 1)
    tri = jnp.where(ci <= ri, 1.0, 0.0).astype(BF16)
    row_c = lax.broadcasted_iota(jnp.int32, (C, LANES), 0)
    row_s = lax.broadcasted_iota(jnp.int32, (SUB, C), 0)
    lane_s = lax.broadcasted_iota(jnp.int32, (SUB, C), 1)
    ng = ng_ref[...]

    def prefix_shifts(x):
        for sh in (1, 2, 4):
            x = x + jnp.where(row_c >= sh, pltpu.roll(x, sh, 0), 0.0)
        sh = 8
        while sh < C:
            x = x + jnp.concatenate([jnp.zeros((sh, LANES), F32), x[:C - sh, :]], axis=0)
            sh *= 2
        return x

    def prefix_matmul(x):
        l1, l2, l3 = _split3_bf16(x)
        return (jnp.dot(tri, l1, preferred_element_type=F32) + jnp.dot(tri, l2, preferred_element_type=F32)
                + jnp.dot(tri, l3, preferred_element_type=F32))

    def scores_factored(qc, kc, b):
        refs = [jnp.zeros((1, LANES), F32)] + [b[s * SUB - 1:s * SUB, :] for s in range(1, n_sub)]
        ref_rows = jnp.concatenate([jnp.broadcast_to(r, (SUB, LANES)) for r in refs], axis=0)
        qt = (qc * jnp.exp(b - ref_rows)).astype(BF16)
        rows_a = []
        for s in range(n_sub):
            hi = (s + 1) * SUB
            kt = (kc[:hi, :] * jnp.exp(refs[s] - b[:hi, :])).astype(BF16)
            a = lax.dot_general(qt[s * SUB:hi, :], kt, _NT, preferred_element_type=F32)
            if hi < C:
                a = jnp.concatenate([a, jnp.zeros((SUB, C - hi), F32)], axis=1)
            rows_a.append(a)
        return jnp.where(ci <= ri, jnp.concatenate(rows_a, axis=0), 0.0)

    def scores_direct(qc, kc, b):
        rows_a = []
        for sidx in range(n_sub):
            lo = sidx * SUB
            q_i = qc[lo:lo + SUB, :]
            b_i = b[lo:lo + SUB, :]
            if sidx == 0:
                a_blk = jnp.zeros((SUB, C), F32)
            else:
                ref = b[lo - 1:lo, :]
                qt = (q_i * jnp.exp(b_i - ref)).astype(BF16)
                kt = jnp.where(row_c < lo, kc * jnp.exp(jnp.minimum(ref - b, 0.0)), 0.0).astype(BF16)
                a_blk = lax.dot_general(qt, kt, _NT, preferred_element_type=F32)
            for j in range(SUB):
                r_lo = j // 8 * 8
                bj = b[lo + j:lo + j + 1, :]
                kj = kc[lo + j:lo + j + 1, :]
                pj = q_i[r_lo:, :] * (jnp.exp(jnp.minimum(b_i[r_lo:, :] - bj, 0.0)) * kj)
                col = jnp.sum(pj, axis=1, keepdims=True)
                if r_lo:
                    col = jnp.concatenate([jnp.zeros((r_lo, 1), F32), col], axis=0)
                a_blk = jnp.where((lane_s == lo + j) & (row_s >= j), col, a_blk)
            rows_a.append(a_blk)
        return jnp.concatenate(rows_a, axis=0)

    def make_chunk(scores, prefix):
        def chunk(c, st):
            r0 = pl.multiple_of(c * C, C)
            qc = q_ref[0, pl.ds(r0, C), :]
            kc = kk_ref[0, pl.ds(r0, C), :]
            vb = v_ref[0, pl.ds(r0, C), :].astype(BF16)
            b = prefix(lf_ref[0, pl.ds(r0, C), :])
            b_last = b[C - 1:C, :]
            inter = lax.dot_general((qc * jnp.exp(b)).astype(BF16), st.astype(BF16), _NT,
                                    preferred_element_type=F32)
            o = inter + jnp.dot(scores(qc, kc, b).astype(BF16), vb, preferred_element_type=F32)
            o = o * lax.rsqrt(jnp.mean(o * o, axis=1, keepdims=True) + RMS_EPS) * ng
            gc = g_ref[0, pl.ds(r0, C), :]
            o_ref[0, pl.ds(r0, C), :] = (o * (gc / (1.0 + jnp.exp(-gc)))).astype(o_ref.dtype)
            kd = (kc * jnp.exp(b_last - b)).astype(BF16)
            return st * jnp.exp(b_last) + lax.dot_general(vb, kd, _TN, preferred_element_type=F32)
        return chunk

    sub_decay = jnp.sum(lf_ref[0].reshape(seq // SUB, SUB, LANES), axis=1)
    fast_ok = jnp.min(sub_decay) >= -HGRN_FAST_LIMIT
    st0 = jnp.zeros((LANES, LANES), F32)

    @pl.when(fast_ok)
    def _():
        lax.fori_loop(0, seq // C, make_chunk(scores_factored, prefix_shifts), st0, unroll=HGRN_UNROLL)

    @pl.when(jnp.logical_not(fast_ok))
    def _():
        lax.fori_loop(0, seq // C, make_chunk(scores_direct, prefix_matmul), st0)


def _hgrn(q, lf, kk, v, g, norm_g, batch, seq):
    shp = (batch, seq, D_MODEL)
    args = [a.reshape(shp) for a in (q, lf, kk, v, g)]
    spec = pl.BlockSpec((1, seq, LANES), lambda bb, h: (bb, 0, h))
    return pl.pallas_call(
        functools.partial(_hgrn_kernel, seq=seq),
        grid=(batch, N_HEADS_C),
        in_specs=[spec] * 5 + [pl.BlockSpec((1, LANES), lambda bb, h: (0, 0))],
        out_specs=spec,
        out_shape=jax.ShapeDtypeStruct(shp, BF16),
        compiler_params=_params("parallel", "parallel"),
        name="hgrn",
    )(*args, norm_g.astype(F32).reshape(1, LANES))


def _layer_norm_rows(z, g, b):
    mu = jnp.mean(z, axis=1, keepdims=True)
    zc = z - mu
    var = jnp.mean(zc * zc, axis=1, keepdims=True)
    return zc * lax.rsqrt(var + LN_EPS) * g + b


def _first_lane_of_max(vals, lane_f):
    mx = jnp.max(vals, axis=1, keepdims=True)
    return mx, jnp.min(jnp.where(vals == mx, lane_f, float(LANES)), axis=1, keepdims=True)


def _outproj_ln_kernel(*refs, n_parts):
    o_refs, w_refs = refs[:n_parts], refs[n_parts:2 * n_parts]
    (x_ref, g_ref, b_ref, wr_hi_ref, wr_lo_ref, rb_ref,
     y_ref, yb_ref, fields_ref, info_ref, cnt_ref) = refs[2 * n_parts:]
    mix = jnp.dot(o_refs[0][...], w_refs[0][...], preferred_element_type=F32)
    for o_ref, w_ref in zip(o_refs[1:], w_refs[1:]):
        mix += jnp.dot(o_ref[...], w_ref[...], preferred_element_type=F32)
    y = _layer_norm_rows(DEEPNORM_ALPHA * x_ref[...] + mix, g_ref[...], b_ref[...])
    y_ref[...] = y
    y_hi = y.astype(BF16)
    yb_ref[...] = y_hi
    y_lo = (y - y_hi.astype(F32)).astype(BF16)
    lg = (jnp.dot(y_hi, wr_hi_ref[...], preferred_element_type=F32)
          + jnp.dot(y_lo, wr_hi_ref[...], preferred_element_type=F32)
          + jnp.dot(y_hi, wr_lo_ref[...], preferred_element_type=F32)
          + rb_ref[...])
    lane = lax.broadcasted_iota(jnp.int32, lg.shape, 1)
    lane_f = lane.astype(F32)
    is_g = lane < N_GROUPS
    mg, grp = _first_lane_of_max(jnp.where(is_g, lg, -jnp.inf), lane_f)
    pg = 1.0 / jnp.sum(jnp.where(is_g, jnp.exp(lg - mg), 0.0), axis=1, keepdims=True)
    lo = float(N_GROUPS) + float(EXPERTS_PER_GROUP) * grp
    le = jnp.where((lane_f >= lo) & (lane_f < lo + float(EXPERTS_PER_GROUP)), lg, -jnp.inf)
    m1, i1 = _first_lane_of_max(le, lane_f)
    m2, i2 = _first_lane_of_max(jnp.where(lane_f == i1, -jnp.inf, le), lane_f)
    e21 = jnp.exp(m2 - m1)
    w0 = pg / (1.0 + e21)
    w1 = pg * e21 / (1.0 + e21)
    eid0 = i1 - float(N_GROUPS)
    eid1 = i2 - float(N_GROUPS)

    @pl.when(pl.program_id(0) == 0)
    def _():
        cnt_ref[...] = jnp.zeros(cnt_ref.shape, F32)

    hits = jnp.where(lane_f == eid0, 1.0, 0.0) + jnp.where(lane_f == eid1, 1.0, 0.0)
    cnt_ref[...] += jnp.sum(hits, axis=0, keepdims=True)
    fields = jnp.where(lane == 0, eid0, jnp.where(lane == 1, eid1, jnp.where(lane == 2, w0,
                       jnp.where(lane == 3, w1, 0.0))))
    fields_ref[...] = fields
    info_ref[...] = fields.T[:8, :]


def _outproj_ln(o_parts, x2d, w_bf16, ln_g, ln_b, wr, rb, tm=1024):
    t, d = x2d.shape
    wr_hi = wr.astype(BF16)
    wr_lo = (wr - wr_hi.astype(F32)).astype(BF16)
    vec = lambda n: pl.BlockSpec((1, n), lambda i: (0, 0))
    widths = [o.shape[1] for o in o_parts]
    offs = np.cumsum([0] + widths)
    w_parts = [w_bf16[offs[k]:offs[k + 1]] for k in range(len(widths))]
    return pl.pallas_call(
        functools.partial(_outproj_ln_kernel, n_parts=len(widths)),
        grid=(t // tm,),
        in_specs=[pl.BlockSpec((tm, wd), lambda i: (i, 0)) for wd in widths]
                 + [pl.BlockSpec((wd, d), lambda i: (0, 0)) for wd in widths]
                 + [pl.BlockSpec((tm, d), lambda i: (i, 0)),
                  vec(d), vec(d),
                  pl.BlockSpec((d, LANES), lambda i: (0, 0)),
                  pl.BlockSpec((d, LANES), lambda i: (0, 0)),
                  vec(LANES)],
        out_specs=[pl.BlockSpec((tm, d), lambda i: (i, 0)),
                   pl.BlockSpec((tm, d), lambda i: (i, 0)),
                   pl.BlockSpec((tm, LANES), lambda i: (i, 0)),
                   pl.BlockSpec((8, tm), lambda i: (0, i)),
                   vec(LANES)],
        out_shape=[jax.ShapeDtypeStruct((t, d), F32),
                   jax.ShapeDtypeStruct((t, d), BF16),
                   jax.ShapeDtypeStruct((t, LANES), F32),
                   jax.ShapeDtypeStruct((8, t), F32),
                   jax.ShapeDtypeStruct((1, LANES), F32)],
        compiler_params=_params("arbitrary"),
        name="outproj_ln",
    )(*o_parts, *w_parts, x2d, ln_g.reshape(1, d), ln_b.reshape(1, d), wr_hi, wr_lo, rb)


def _expert_kernel(te_ref, nv_ref, xs_ref, w1_ref, w3_ref, w2_ref, o_ref, w1_s, w3_s, w2_s):
    i = pl.program_id(0)
    prev = te_ref[jnp.maximum(i - 1, 0)]

    @pl.when((i == 0) | (te_ref[i] != prev))
    def _cast_weights():
        w1_s[...] = w1_ref[0, 0].astype(BF16)
        w3_s[...] = w3_ref[0, 0].astype(BF16)
        w2_s[...] = w2_ref[0, 0].astype(BF16)

    @pl.when(i < nv_ref[0])
    def _ffn():
        for r in range(0, MOE_TILE, MOE_SLAB):
            xb = xs_ref[r:r + MOE_SLAB, :]
            a = jnp.dot(xb, w1_s[...], preferred_element_type=F32)
            u = jnp.dot(xb, w3_s[...], preferred_element_type=F32)
            hb = (a / (1.0 + jnp.exp(-a))) * u
            o_ref[r:r + MOE_SLAB, :] = jnp.dot(hb.astype(BF16), w2_s[...],
                                               preferred_element_type=F32).astype(o_ref.dtype)

    @pl.when(i >= nv_ref[0])
    def _pad():
        o_ref[...] = jnp.zeros(o_ref.shape, o_ref.dtype)


def _expert_ffn(tile_expert, n_valid, xs, w1, w3, w2, layer):
    p, d = xs.shape
    f = w1.shape[3]
    n_tiles = p // MOE_TILE
    grid_spec = pltpu.PrefetchScalarGridSpec(
        num_scalar_prefetch=2,
        grid=(n_tiles,),
        in_specs=[pl.BlockSpec((MOE_TILE, d), lambda i, te, nv: (i, 0)),
                  pl.BlockSpec((1, 1, d, f), lambda i, te, nv: (layer, te[i], 0, 0)),
                  pl.BlockSpec((1, 1, d, f), lambda i, te, nv: (layer, te[i], 0, 0)),
                  pl.BlockSpec((1, 1, f, d), lambda i, te, nv: (layer, te[i], 0, 0))],
        out_specs=pl.BlockSpec((MOE_TILE, d), lambda i, te, nv: (i, 0)),
        scratch_shapes=[pltpu.VMEM((d, f), BF16), pltpu.VMEM((d, f), BF16), pltpu.VMEM((f, d), BF16)],
    )
    return pl.pallas_call(
        _expert_kernel,
        grid_spec=grid_spec,
        out_shape=jax.ShapeDtypeStruct((p, d), BF16),
        compiler_params=_params("arbitrary"),
        name="expert_ffn",
    )(tile_expert, n_valid, xs, w1, w3, w2)


def _combine_ln_kernel(x_ref, y0_ref, y1_ref, fields_ref, g_ref, b_ref, o_ref):
    fields = fields_ref[...]
    ffn = fields[:, 2:3] * y0_ref[...].astype(F32) + fields[:, 3:4] * y1_ref[...].astype(F32)
    o_ref[...] = _layer_norm_rows(DEEPNORM_ALPHA * x_ref[...] + ffn, g_ref[...], b_ref[...])


def _combine_ln(x2d, y01, fields, ln_g, ln_b, tm=512):
    t, d = x2d.shape
    nt = t // tm
    row = pl.BlockSpec((tm, d), lambda i: (i, 0))
    vec = pl.BlockSpec((1, d), lambda i: (0, 0))
    return pl.pallas_call(
        _combine_ln_kernel,
        grid=(nt,),
        in_specs=[row, row, pl.BlockSpec((tm, d), lambda i: (i + nt, 0)),
                  pl.BlockSpec((tm, LANES), lambda i: (i, 0)), vec, vec],
        out_specs=row,
        out_shape=jax.ShapeDtypeStruct((t, d), F32),
        compiler_params=_params("parallel"),
        name="combine_ln",
    )(x2d, y01, y01, fields, ln_g.reshape(1, d), ln_b.reshape(1, d))


def _dispatch_plan(info, counts_f):
    t = info.shape[1]
    a = t * TOPK_IN_GROUP
    n_fill = N_EXPERTS * MOE_TILE
    n_tiles = a // MOE_TILE + N_EXPERTS
    experts = jnp.arange(N_EXPERTS, dtype=jnp.int32)
    counts = counts_f[0, :N_EXPERTS].astype(jnp.int32)
    padded = (counts + MOE_TILE - 1) // MOE_TILE * MOE_TILE
    pends = jnp.cumsum(padded)
    fill_ends = jnp.cumsum(padded - counts)
    fill = jnp.arange(n_fill, dtype=jnp.int32)
    fill_key = jnp.sum((fill_ends[None, :] <= fill[:, None]).astype(jnp.int32), axis=1)
    keys = jnp.concatenate([info[0].astype(jnp.int32), info[1].astype(jnp.int32), fill_key])
    flat = jnp.arange(a + n_fill, dtype=jnp.int32)
    flat_bits = int(a + n_fill - 1).bit_length()
    assert flat_bits + int(N_EXPERTS).bit_length() < 32
    slot_flat = jnp.sort(jnp.left_shift(keys, flat_bits) | flat) & ((1 << flat_bits) - 1)
    slot_tok = jnp.where(slot_flat < a, slot_flat % t, (slot_flat - a) % t)
    _, slot_of = lax.sort((slot_flat, flat), num_keys=1)
    tile_start = jnp.arange(n_tiles, dtype=jnp.int32) * MOE_TILE
    tile_expert = jnp.sum((pends[None, :] <= tile_start[:, None]).astype(jnp.int32), axis=1)
    last_used = jnp.max(jnp.where(counts > 0, experts, 0))
    tile_expert = jnp.minimum(tile_expert, last_used)
    n_valid = (pends[-1] // MOE_TILE).astype(jnp.int32).reshape(1)
    return tile_expert, n_valid, slot_tok, slot_of[:a]


def _moe_and_norm(x1, x1_bf16, fields, info, counts_f, w1, w3, w2, layer, ln_g, ln_b):
    tile_expert, n_valid, slot_tok, dest = _dispatch_plan(info, counts_f)
    yb = _expert_ffn(tile_expert, n_valid, x1_bf16[slot_tok], w1, w3, w2, layer)
    return _combine_ln(x1, yb[dest], fields, ln_g, ln_b)


def kernel(x, ab_w_in, ab_w_out, c_w_in, c_w_out, c_norm_g, hgrn_lb_logits, ln_g, ln_b,
           router_g_w, router_g_b, router_e_w, router_e_b, exp_w1, exp_w3, exp_w2):
    batch, seq, d = x.shape
    t = batch * seq
    tables = _rope_lane_tables(seq)
    bias_tab = jnp.asarray(_dilated_bias_table())
    lb_all = jnp.cumsum(jax.nn.softmax(hgrn_lb_logits.astype(F32), axis=0), axis=0)
    lb_all = lb_all - lb_all[0:1]
    n_pairs = (AB_WIDTH // 2) // LANES

    xc = x.reshape(t, d)
    for l in range(DEPTH):
        j = l // 2
        if l % 2 == 0:
            h = _proj_ab(xc, ab_w_in[j].astype(BF16), tables, seq).reshape(batch, seq, 3 * AB_WIDTH)
            o_parts = [_moba(h, n_pairs).reshape(t, AB_WIDTH // 2),
                       _dilated(h, bias_tab, n_pairs).reshape(t, AB_WIDTH // 2)]
            w_out = ab_w_out[j]
        else:
            q, lf, kk, v, g = _proj_c(xc, c_w_in[j].astype(BF16), lb_all[j])
            o_parts = [_hgrn(q, lf, kk, v, g, c_norm_g[j], batch, seq).reshape(t, d)]
            w_out = c_w_out[j]
        wr = jnp.zeros((d, LANES), F32)
        wr = wr.at[:, :N_GROUPS].set(router_g_w[l]).at[:, N_GROUPS:N_GROUPS + N_EXPERTS].set(router_e_w[l])
        rb = jnp.zeros((1, LANES), F32)
        rb = rb.at[0, :N_GROUPS].set(router_g_b[l]).at[0, N_GROUPS:N_GROUPS + N_EXPERTS].set(router_e_b[l])
        x1, x1_bf16, fields, info, counts_f = _outproj_ln(o_parts, xc, w_out.astype(BF16), ln_g[l, 0], ln_b[l, 0],
                                                          wr, rb)
        xc = _moe_and_norm(x1, x1_bf16, fields, info, counts_f, exp_w1, exp_w3, exp_w2, l,
                           ln_g[l, 1], ln_b[l, 1])
    return xc.reshape(batch, seq, d)
```
